```python
import math
import jax, jax.numpy as jnp
from jax import lax
import numpy as np

D_MODEL = 1024
BATCH = 2
SEQ = 8192
DEPTH = 1
DEC_BATCH = 128
DEC_SEQ = 1
PAST_LEN = 2048
PAGE_SIZE = 128

A_HEADS = 4
A_QK_DIM = 64
A_V_DIM = 2 * A_QK_DIM
A_K_ROW = 2 * A_QK_DIM
A_WIDTH = A_HEADS * A_V_DIM
ROPE_THETA = 500000.0
ROPE_DIM = A_QK_DIM // 4
Q_BLOCK = 128
R_HEADS = 4
R_QK_DIM = 64
R_V_DIM = 2 * R_QK_DIM
R_WIDTH = R_HEADS * R_V_DIM
R_THETA = 10000.0
R_CHUNK = 128
MIX_WIDTH = A_WIDTH + R_WIDTH
N_GROUPS = 4
EXPERTS_PER_GROUP = 8
TOP_K = 2
D_EXPERT = 512
EPS = 1e-6
NEG = -1e30

kernel_name = 'hymba_diffattn_retnet_hmoe_step'


def rms_norm(x, g):
    xf = x.astype(jnp.float32)
    y = xf * lax.rsqrt(jnp.mean(xf * xf, axis=-1, keepdims=True) + EPS)
    return (y * g.astype(jnp.float32)).astype(x.dtype)


def partial_rope(x, pos):
    half = ROPE_DIM // 2
    inv = ROPE_THETA ** (-jnp.arange(half, dtype=jnp.float32) / half)
    ang = pos.astype(jnp.float32)[:, None] * inv[None, :]
    bshape = (pos.shape[0],) + (1,) * (x.ndim - 3) + (half,)
    cos, sin = jnp.cos(ang).reshape(bshape), jnp.sin(ang).reshape(bshape)
    xr = x[..., :ROPE_DIM].astype(jnp.float32)
    x1, x2 = xr[..., :half], xr[..., half:]
    rot = jnp.concatenate([x1 * cos - x2 * sin, x2 * cos + x1 * sin], axis=-1).astype(x.dtype)
    return jnp.concatenate([rot, x[..., ROPE_DIM:]], axis=-1)


def retention_rotate(x, pos):
    half = R_QK_DIM // 2
    inv = 1.0 / (R_THETA ** jnp.linspace(0.0, 1.0, half, dtype=jnp.float32))
    ang = pos.astype(jnp.float32)[:, None] * inv[None, :]
    cos, sin = jnp.cos(ang)[:, None, :], jnp.sin(ang)[:, None, :]
    xf = x.astype(jnp.float32).reshape(x.shape[:-1] + (half, 2))
    xe, xo = xf[..., 0], xf[..., 1]
    return jnp.stack([xe * cos - xo * sin, xo * cos + xe * sin], axis=-1).reshape(x.shape)


def project(n, w_in, q_norm, k_norm, pos):
    B, S = n.shape[:2]
    h = n @ w_in
    sizes = (A_HEADS * A_K_ROW, A_HEADS * A_K_ROW, A_WIDTH,
             R_HEADS * R_QK_DIM, R_HEADS * R_QK_DIM, R_WIDTH, R_WIDTH)
    aq, ak, av, rq, rk, rv, rg = jnp.split(h, np.cumsum(sizes)[:-1].tolist(), axis=-1)
    aq = partial_rope(rms_norm(aq.reshape(B, S, A_HEADS, 2, A_QK_DIM), q_norm), pos)
    ak = partial_rope(rms_norm(ak.reshape(B, S, A_HEADS, 2, A_QK_DIM), k_norm), pos)
    av = av.reshape(B, S, A_HEADS, A_V_DIM)
    rq = retention_rotate(rq.reshape(B, S, R_HEADS, R_QK_DIM), pos)
    rk = retention_rotate(rk.reshape(B, S, R_HEADS, R_QK_DIM), pos) * (R_QK_DIM ** -0.5)
    rv = rv.reshape(B, S, R_HEADS, R_V_DIM).astype(jnp.float32)
    return aq, ak, av, rq, rk, rv, rg


def diff_attn_prompt(q, k, v, lam):
    B, S = q.shape[:2]
    nb = S // Q_BLOCK
    scale = A_QK_DIM ** -0.5
    qb = jnp.moveaxis(q.reshape(B, nb, Q_BLOCK, A_HEADS, 2, A_QK_DIM), 1, 0)
    key_pos = jnp.arange(S)

    def block(args):
        qi, bi = args
        s = jnp.einsum('bqhcd,bkhcd->bhcqk', qi, k, preferred_element_type=jnp.float32) * scale
        qpos = bi * Q_BLOCK + jnp.arange(Q_BLOCK)
        s = jnp.where(key_pos[None, :] <= qpos[:, None], s, NEG)
        p = jax.nn.softmax(s, axis=-1)
        w = p[:, :, 0] - lam * p[:, :, 1]
        return jnp.einsum('bhqk,bkhd->bqhd', w.astype(v.dtype), v)

    o = lax.map(block, (qb, jnp.arange(nb)))
    return jnp.moveaxis(o, 0, 1).reshape(B, S, A_HEADS, A_V_DIM)


def diff_attn_sample(q, k_new, v_new, k_past, v_past, lam):
    T = q.shape[1]
    P = k_past.shape[1]
    scale = A_QK_DIM ** -0.5
    s_past = jnp.einsum('bqhcd,bkhcd->bhcqk', q, k_past, preferred_element_type=jnp.float32) * scale
    s_new = jnp.einsum('bqhcd,bkhcd->bhcqk', q, k_new, preferred_element_type=jnp.float32) * scale
    causal = jnp.arange(T)[None, :] <= jnp.arange(T)[:, None]
    s_new = jnp.where(causal, s_new, NEG)
    p = jax.nn.softmax(jnp.concatenate([s_past, s_new], axis=-1), axis=-1)
    w = (p[:, :, 0] - lam * p[:, :, 1]).astype(v_new.dtype)
    return (jnp.einsum('bhqk,bkhd->bqhd', w[..., :P], v_past)
            + jnp.einsum('bhqk,bkhd->bqhd', w[..., P:], v_new))


def retention_chunk(state, q, k, v):
    L = q.shape[1]
    log_g = jnp.log(1.0 - 2.0 ** (-5.0 - jnp.arange(R_HEADS, dtype=jnp.float32)))
    idx = jnp.arange(L, dtype=jnp.float32)
    diff = idx[:, None] - idx[None, :]
    dmask = jnp.where(diff >= 0, jnp.exp(jnp.maximum(diff, 0.0)[None] * log_g[:, None, None]), 0.0)
    inner = jnp.einsum('bihd,bjhd->bhij', q, k) * dmask
    o = jnp.einsum('bhij,bjhv->bihv', inner, v)
    q_dec = jnp.exp((idx + 1.0)[:, None] * log_g[None, :])
    o = o + jnp.einsum('bihd,bhdv->bihv', q * q_dec[None, :, :, None], state)
    k_dec = jnp.exp((L - 1.0 - idx)[:, None] * log_g[None, :])
    new_state = (jnp.exp(L * log_g)[None, :, None, None] * state
                 + jnp.einsum('bjhd,bjhv->bhdv', k * k_dec[None, :, :, None], v))
    return new_state, o


def retention_prompt(q, k, v):
    B, S = q.shape[:2]
    nc = S // R_CHUNK
    to_chunks = lambda t: jnp.moveaxis(t.reshape((B, nc, R_CHUNK) + t.shape[2:]), 1, 0)
    state0 = jnp.zeros((B, R_HEADS, R_QK_DIM, R_V_DIM), jnp.float32)
    state, o = lax.scan(lambda s, c: retention_chunk(s, *c), state0,
                        (to_chunks(q), to_chunks(k), to_chunks(v)))
    return jnp.moveaxis(o, 0, 1).reshape(B, S, R_HEADS, R_V_DIM), state


def merge_groups(a_o, r_o, r_gate, a_subln, r_norm, lam_init):
    B, S = a_o.shape[:2]
    a = (rms_norm(a_o, a_subln) * (1.0 - lam_init)).reshape(B, S, A_WIDTH).astype(r_gate.dtype)
    r = rms_norm(r_o, r_norm).reshape(B, S, R_WIDTH).astype(r_gate.dtype) * jax.nn.silu(r_gate)
    return jnp.concatenate([a, r], axis=-1)


def hier_moe(x, w_gr, b_gr, w_er, b_er, w_gate, w_up, w_down):
    N = x.shape[0]
    xf = x.astype(jnp.float32)
    g_logits = xf @ w_gr.astype(jnp.float32) + b_gr.astype(jnp.float32)
    g_idx = jnp.argmax(g_logits, axis=-1)
    g_prob = jnp.take_along_axis(jax.nn.softmax(g_logits, axis=-1), g_idx[:, None], axis=1)
    e_logits = (xf @ w_er.astype(jnp.float32) + b_er.astype(jnp.float32)).reshape(N, N_GROUPS, EXPERTS_PER_GROUP)
    e_sel = jnp.take_along_axis(e_logits, g_idx[:, None, None], axis=1)[:, 0]
    top_v, top_i = lax.top_k(e_sel, TOP_K)
    top_w = jax.nn.softmax(top_v, axis=-1) * g_prob
    within = jnp.sum(jax.nn.one_hot(top_i, EXPERTS_PER_GROUP, dtype=jnp.float32) * top_w[..., None], axis=1)
    combine = (jax.nn.one_hot(g_idx, N_GROUPS, dtype=jnp.float32)[:, :, None] * within[:, None, :]).astype(x.dtype)
    out = jnp.zeros(x.shape, jnp.float32)
    for g in range(N_GROUPS):
        a = jnp.einsum('nd,edf->nef', x, w_gate[g])
        u = jnp.einsum('nd,edf->nef', x, w_up[g])
        h = jax.nn.silu(a) * u * combine[:, g, :, None]
        out = out + jnp.einsum('nef,efd->nd', h, w_down[g], preferred_element_type=jnp.float32)
    return out.astype(x.dtype)


def setup_inputs(seed: int = 0) -> dict:
    key = jax.random.key(seed)
    ks = jax.random.split(key, 24)
    f32 = jnp.float32
    n_pages = PAST_LEN // PAGE_SIZE
    n_used = DEC_BATCH * n_pages
    n_phys = n_used + max(n_used // 4, 1)
    in_cols = 2 * A_HEADS * A_K_ROW + A_WIDTH + 2 * R_HEADS * R_QK_DIM + 2 * R_WIDTH
    nrm = lambda k, shape, s: jax.random.normal(k, shape, f32) * s
    gain = lambda k, shape: 1.0 + 0.02 * jax.random.normal(k, shape, f32)
    n_exp = N_GROUPS * EXPERTS_PER_GROUP
    page_table = jax.random.permutation(ks[5], n_phys)[:n_used].reshape(DEC_BATCH, n_pages).astype(jnp.int32)
    return {
        'x_prompt': nrm(ks[0], (BATCH, SEQ, D_MODEL), 1.0),
        'x_sample': nrm(ks[1], (DEC_BATCH, DEC_SEQ, D_MODEL), 1.0),
        'cache_k': nrm(ks[2], (DEPTH, n_phys, PAGE_SIZE, A_HEADS, A_K_ROW), 1.0),
        'cache_v': nrm(ks[3], (DEPTH, n_phys, PAGE_SIZE, A_HEADS, A_V_DIM), 1.0),
        'state_ret': nrm(ks[4], (DEPTH, DEC_BATCH, R_HEADS, R_QK_DIM, R_V_DIM), 1.0),
        'page_table': page_table,
        'norm1': gain(ks[6], (DEPTH, D_MODEL)),
        'w_in': nrm(ks[7], (DEPTH, D_MODEL, in_cols), D_MODEL ** -0.5),
        'a_q_norm': gain(ks[8], (DEPTH, A_QK_DIM)),
        'a_k_norm': gain(ks[9], (DEPTH, A_QK_DIM)),
        'a_lambda_q1': nrm(ks[10], (DEPTH, A_QK_DIM), 0.1),
        'a_lambda_k1': nrm(ks[11], (DEPTH, A_QK_DIM), 0.1),
        'a_lambda_q2': nrm(ks[12], (DEPTH, A_QK_DIM), 0.1),
        'a_lambda_k2': nrm(ks[13], (DEPTH, A_QK_DIM), 0.1),
        'a_subln': gain(ks[14], (DEPTH, A_V_DIM)),
        'r_norm': gain(ks[15], (DEPTH, R_V_DIM)),
        'w_o': nrm(ks[16], (DEPTH, MIX_WIDTH, D_MODEL), MIX_WIDTH ** -0.5),
        'norm2': gain(ks[17], (DEPTH, D_MODEL)),
        'w_group_router': nrm(ks[18], (DEPTH, D_MODEL, N_GROUPS), D_MODEL ** -0.5),
        'b_group_router': nrm(ks[19], (DEPTH, N_GROUPS), 0.01),
        'w_expert_router': nrm(ks[20], (DEPTH, D_MODEL, n_exp), D_MODEL ** -0.5),
        'b_expert_router': nrm(ks[21], (DEPTH, n_exp), 0.01),
        'w_gate': nrm(ks[22], (DEPTH, N_GROUPS, EXPERTS_PER_GROUP, D_MODEL, D_EXPERT), D_MODEL ** -0.5),
        'w_up': nrm(jax.random.fold_in(ks[22], 1), (DEPTH, N_GROUPS, EXPERTS_PER_GROUP, D_MODEL, D_EXPERT), D_MODEL ** -0.5),
        'w_down': nrm(ks[23], (DEPTH, N_GROUPS, EXPERTS_PER_GROUP, D_EXPERT, D_MODEL), D_EXPERT ** -0.5),
    }


def reference(x_prompt, x_sample, cache_k, cache_v, state_ret, page_table,
              norm1, w_in, a_q_norm, a_k_norm, a_lambda_q1, a_lambda_k1, a_lambda_q2, a_lambda_k2,
              a_subln, r_norm, w_o, norm2, w_group_router, b_group_router, w_expert_router,
              b_expert_router, w_gate, w_up, w_down):
    B, S, D = x_prompt.shape
    Bd, T, _ = x_sample.shape
    past = page_table.shape[1] * cache_k.shape[2]
    pos_p = jnp.arange(S)
    pos_s = past + jnp.arange(T)
    hp, hs = x_prompt, x_sample
    kp_l, vp_l, rp_l, ks_l, vs_l, rs_l = [], [], [], [], [], []
    for l in range(DEPTH):
        lam_init = 0.8 - 0.6 * math.exp(-0.3 * l)
        lam = (jnp.exp(jnp.sum(a_lambda_q1[l].astype(jnp.float32) * a_lambda_k1[l].astype(jnp.float32)))
               - jnp.exp(jnp.sum(a_lambda_q2[l].astype(jnp.float32) * a_lambda_k2[l].astype(jnp.float32)))
               + lam_init)
        aq, ak, av, rq, rk, rv, rg = project(rms_norm(hp, norm1[l]), w_in[l], a_q_norm[l], a_k_norm[l], pos_p)
        a_o = diff_attn_prompt(aq, ak, av, lam)
        r_o, r_state_p = retention_prompt(rq, rk, rv)
        hp = hp + merge_groups(a_o, r_o, rg, a_subln[l], r_norm[l], lam_init) @ w_o[l]
        kp_l.append(ak.reshape(B, S, A_HEADS, A_K_ROW))
        vp_l.append(av)
        rp_l.append(r_state_p.astype(state_ret.dtype))
        sq, sk, sv, srq, srk, srv, srg = project(rms_norm(hs, norm1[l]), w_in[l], a_q_norm[l], a_k_norm[l], pos_s)
        k_past = cache_k[l, page_table].reshape(Bd, past, A_HEADS, 2, A_QK_DIM)
        v_past = cache_v[l, page_table].reshape(Bd, past, A_HEADS, A_V_DIM)
        a_o_s = diff_attn_sample(sq, sk, sv, k_past, v_past, lam)
        r_state_s, r_o_s = retention_chunk(state_ret[l].astype(jnp.float32), srq, srk, srv)
        hs = hs + merge_groups(a_o_s, r_o_s, srg, a_subln[l], r_norm[l], lam_init) @ w_o[l]
        ks_l.append(sk.reshape(Bd, T, A_HEADS, A_K_ROW))
        vs_l.append(sv)
        rs_l.append(r_state_s.astype(state_ret.dtype))
        tokens = jnp.concatenate([hp.reshape(B * S, D), hs.reshape(Bd * T, D)], axis=0)
        m = hier_moe(rms_norm(tokens, norm2[l]), w_group_router[l], b_group_router[l],
                     w_expert_router[l], b_expert_router[l], w_gate[l], w_up[l], w_down[l])
        hp = hp + m[:B * S].reshape(B, S, D)
        hs = hs + m[B * S:].reshape(Bd, T, D)
    return (hp, hs, jnp.stack(kp_l), jnp.stack(vp_l), jnp.stack(rp_l),
            jnp.stack(ks_l), jnp.stack(vs_l), jnp.stack(rs_l))
```

```python
import functools
import math

import numpy as np
import jax
import jax.numpy as jnp
from jax import lax
from jax.experimental import pallas as pl
from jax.experimental.pallas import tpu as pltpu

F32 = jnp.float32
BF16 = jnp.bfloat16
I32 = jnp.int32

LANES = 128
SUBLANES = 8
CHUNKS = 8

D_MODEL = 1024
A_HEADS = 4
A_QK = 64
A_ROW = 2 * A_QK
A_WIDTH = A_HEADS * A_ROW
ROPE_THETA = 500000.0
ROPE_DIM = A_QK // 4
R_HEADS = 4
R_QK = 64
R_V = 128
R_QW = R_HEADS * R_QK
R_WIDTH = R_HEADS * R_V
R_THETA = 10000.0
N_GROUPS = 4
EXPERTS_PER_GROUP = 8
N_EXPERTS = N_GROUPS * EXPERTS_PER_GROUP
D_EXPERT = 512
EPS = 1e-6
NEG = -1e30

TOKEN_TILE = 256
ATTN_TILE = 256
RET_CHUNK = 128
PLAN_TILE = 128
SLOT_TILE = 256
SAMPLE_BLOCK = 8
VMEM_LIMIT = 56 * 1024 * 1024


def _cparams(sem, vmem=VMEM_LIMIT):
    return pltpu.CompilerParams(dimension_semantics=sem, vmem_limit_bytes=vmem)


def _rope_tables(pos):
    half = ROPE_DIM // 2
    inv = ROPE_THETA ** (-jnp.arange(half, dtype=F32) / half)
    ang = pos.astype(F32)[:, None] * inv[None, :]
    cos, sin = jnp.cos(ang), jnp.sin(ang)
    n = pos.shape[0]
    ones = jnp.ones((n, A_QK - ROPE_DIM), F32)
    zeros = jnp.zeros((n, A_QK - ROPE_DIM), F32)
    zh = jnp.zeros((n, half), F32)
    c = jnp.concatenate([cos, cos, ones], axis=1)
    s_next = jnp.concatenate([-sin, zh, zeros], axis=1)
    s_prev = jnp.concatenate([zh, sin, zeros], axis=1)
    rep = LANES // A_QK
    return tuple(jnp.tile(t, (1, rep)) for t in (c, s_next, s_prev))


def _ret_tables(pos):
    half = R_QK // 2
    inv = 1.0 / (R_THETA ** jnp.linspace(0.0, 1.0, half, dtype=F32))
    ang = pos.astype(F32)[:, None] * inv[None, :]
    cos, sin = jnp.cos(ang), jnp.sin(ang)
    z = jnp.zeros_like(sin)
    c = jnp.stack([cos, cos], axis=-1).reshape(-1, R_QK)
    s_next = jnp.stack([-sin, z], axis=-1).reshape(-1, R_QK)
    s_prev = jnp.stack([z, sin], axis=-1).reshape(-1, R_QK)
    rep = LANES // R_QK
    return tuple(jnp.tile(t, (1, rep)) for t in (c, s_next, s_prev))


def _segment_ones():
    seg = np.arange(LANES) // A_QK
    return jnp.asarray((seg[:, None] == seg[None, :]).astype(np.float32), dtype=BF16)


def _proj_kernel(x_ref, g1_ref, w_ref, qg_ref, kg_ref, rc_ref, rn_ref, rp_ref,
                 tc_ref, tn_ref, tp_ref, ones_ref,
                 qbf_ref, k_ref, kbf_ref, v_ref, vbf_ref, rq_ref, rk_ref, rv_ref, rg_ref):
    x = x_ref[...]
    ms = jnp.mean(x * x, axis=-1, keepdims=True)
    n = (x * lax.rsqrt(ms + EPS) * g1_ref[...]).astype(BF16)
    h = jnp.dot(n, w_ref[...], preferred_element_type=F32)
    ones = ones_ref[...]
    rc, rn, rp = rc_ref[...], rn_ref[...], rp_ref[...]

    def head_norm_rope(xh, gain):
        sq = xh * xh
        hi = sq.astype(BF16)
        lo = (sq - hi.astype(F32)).astype(BF16)
        ssq = (jnp.dot(hi, ones, preferred_element_type=F32)
               + jnp.dot(lo, ones, preferred_element_type=F32))
        y = xh * lax.rsqrt(ssq * (1.0 / A_QK) + EPS) * gain
        half = ROPE_DIM // 2
        return (y * rc + pltpu.roll(y, LANES - half, 1) * rn + pltpu.roll(y, half, 1) * rp)

    for hd in range(A_HEADS):
        sl = slice(hd * A_ROW, (hd + 1) * A_ROW)
        q = head_norm_rope(h[:, sl], qg_ref[...])
        qbf_ref[:, sl] = (q * (A_QK ** -0.5)).astype(BF16)
        k = head_norm_rope(h[:, A_WIDTH + hd * A_ROW:A_WIDTH + (hd + 1) * A_ROW], kg_ref[...])
        k_ref[:, sl] = k
        kbf_ref[:, sl] = k.astype(BF16)
    o = 2 * A_WIDTH
    v = h[:, o:o + A_WIDTH]
    v_ref[...] = v
    vbf_ref[...] = v.astype(BF16)
    o += A_WIDTH
    tc, tn, tp = tc_ref[...], tn_ref[...], tp_ref[...]

    def pair_rotate(xs):
        return xs * tc + pltpu.roll(xs, LANES - 1, 1) * tn + pltpu.roll(xs, 1, 1) * tp

    for j in range(R_QW // LANES):
        sl = slice(j * LANES, (j + 1) * LANES)
        rq_ref[:, sl] = pair_rotate(h[:, o + j * LANES:o + (j + 1) * LANES])
        rk_ref[:, sl] = pair_rotate(h[:, o + R_QW + j * LANES:o + R_QW + (j + 1) * LANES]) * (R_QK ** -0.5)
    o += 2 * R_QW
    rv_ref[...] = h[:, o:o + R_WIDTH]
    rg_ref[...] = h[:, o + R_WIDTH:o + 2 * R_WIDTH]


def _proj(x2d, g1, w_bf, qg, kg, rope_t, ret_t, ones, tm):
    t = x2d.shape[0]
    cols = w_bf.shape[1]
    row = lambda w: pl.BlockSpec((tm, w), lambda i: (i, 0))
    full = lambda a: pl.BlockSpec(a.shape, lambda i: (0,) * a.ndim)
    out_shape = (
        jax.ShapeDtypeStruct((t, A_WIDTH), BF16),
        jax.ShapeDtypeStruct((t, A_WIDTH), F32),
        jax.ShapeDtypeStruct((t, A_WIDTH), BF16),
        jax.ShapeDtypeStruct((t, A_WIDTH), F32),
        jax.ShapeDtypeStruct((t, A_WIDTH), BF16),
        jax.ShapeDtypeStruct((t, R_QW), F32),
        jax.ShapeDtypeStruct((t, R_QW), F32),
        jax.ShapeDtypeStruct((t, R_WIDTH), F32),
        jax.ShapeDtypeStruct((t, R_WIDTH), F32),
    )
    return pl.pallas_call(
        _proj_kernel,
        grid=(t // tm,),
        in_specs=[row(D_MODEL), full(g1), full(w_bf), full(qg), full(kg)]
                 + [row(LANES)] * 6 + [full(ones)],
        out_specs=tuple(row(s.shape[1]) for s in out_shape),
        out_shape=out_shape,
        compiler_params=_cparams(("parallel",)),
        name="proj",
    )(x2d, g1, w_bf, qg, kg, *rope_t, *ret_t, ones)


def _lambda(lq1_ref, lk1_ref, lq2_ref, lk2_ref, lam_init):
    s1 = jnp.sum(lq1_ref[...] * lk1_ref[...], axis=-1, keepdims=True)
    s2 = jnp.sum(lq2_ref[...] * lk2_ref[...], axis=-1, keepdims=True)
    return jnp.exp(s1) - jnp.exp(s2) + lam_init


def _attn_kernel(q_ref, k_ref, v_ref, lq1_ref, lk1_ref, lq2_ref, lk2_ref, o_ref,
                 m_ref, l_ref, acc_ref, *, tile, lam_init):
    i = pl.program_id(2)
    q = q_ref[...]
    lane = lax.broadcasted_iota(I32, q.shape, 1)
    zero = jnp.zeros_like(q)
    qq = jnp.concatenate([jnp.where(lane < A_QK, q, zero), jnp.where(lane >= A_QK, q, zero)], axis=0)
    m_ref[...] = jnp.full(m_ref.shape, NEG, F32)
    l_ref[...] = jnp.zeros(l_ref.shape, F32)
    acc_ref[...] = jnp.zeros(acc_ref.shape, F32)

    def step(j, masked):
        start = pl.multiple_of(j * tile, tile)
        k = k_ref[pl.ds(start, tile), :]
        v = v_ref[pl.ds(start, tile), :]
        s = lax.dot_general(qq, k, (((1,), (1,)), ((), ())), preferred_element_type=F32)
        if masked:
            row = lax.broadcasted_iota(I32, s.shape, 0)
            col = lax.broadcasted_iota(I32, s.shape, 1)
            row = jnp.where(row >= tile, row - tile, row)
            s = jnp.where(col <= row, s, NEG)
        m_prev = m_ref[...]
        m_new = jnp.maximum(m_prev, jnp.max(s, axis=-1, keepdims=True))
        alpha = jnp.exp(m_prev - m_new)
        p = jnp.exp(s - m_new)
        l_ref[...] = alpha * l_ref[...] + jnp.sum(p, axis=-1, keepdims=True)
        acc_ref[...] = alpha * acc_ref[...] + jnp.dot(p.astype(BF16), v, preferred_element_type=F32)
        m_ref[...] = m_new

    def body(j, carry):
        step(j, False)
        return carry

    lax.fori_loop(0, i, body, 0)
    step(i, True)
    lam = _lambda(lq1_ref, lk1_ref, lq2_ref, lk2_ref, lam_init)
    o1 = acc_ref[:tile, :] / l_ref[:tile, :]
    o2 = acc_ref[tile:, :] / l_ref[tile:, :]
    o_ref[...] = o1 - lam * o2


def _attn(q_bf, k_bf, v_bf, lams, batch, seq, lam_init):
    tile = ATTN_TILE
    nq = seq // tile
    lam_spec = pl.BlockSpec((1, A_QK), lambda b, h, i: (0, 0))
    kernel = functools.partial(_attn_kernel, tile=tile, lam_init=lam_init)
    return pl.pallas_call(
        kernel,
        grid=(batch, A_HEADS, nq),
        in_specs=[pl.BlockSpec((tile, A_ROW), lambda b, h, i: (b * nq + i, h)),
                  pl.BlockSpec((seq, A_ROW), lambda b, h, i: (b, h)),
                  pl.BlockSpec((seq, A_ROW), lambda b, h, i: (b, h))] + [lam_spec] * 4,
        out_specs=pl.BlockSpec((tile, A_ROW), lambda b, h, i: (b * nq + i, h)),
        out_shape=jax.ShapeDtypeStruct((batch * seq, A_WIDTH), F32),
        scratch_shapes=[pltpu.VMEM((2 * tile, 1), F32), pltpu.VMEM((2 * tile, 1), F32),
                        pltpu.VMEM((2 * tile, A_ROW), F32)],
        compiler_params=_cparams(("parallel", "parallel", "arbitrary")),
        name="attn",
    )(q_bf, k_bf, v_bf, *lams)


def _ret_decay():
    return [math.log(1.0 - 2.0 ** (-5.0 - h)) for h in range(R_HEADS)]


def _ret_tables_chunk(chunk):
    log_g = jnp.log(1.0 - 2.0 ** (-5.0 - jnp.arange(R_HEADS, dtype=F32)))
    idx = jnp.arange(chunk, dtype=F32)
    diff = idx[:, None] - idx[None, :]
    dmask = jnp.where(diff >= 0, jnp.exp(jnp.maximum(diff, 0.0)[None] * log_g[:, None, None]), 0.0)
    q_dec = jnp.exp((idx + 1.0)[:, None] * log_g[None, :])
    k_dec = jnp.exp((chunk - 1.0 - idx)[:, None] * log_g[None, :])
    q_dec = jnp.repeat(q_dec, R_QK, axis=1)
    k_dec = jnp.repeat(k_dec, R_QK, axis=1)
    g_chunk = jnp.exp(chunk * log_g)
    g_rows = jnp.broadcast_to(jnp.repeat(g_chunk, R_QK)[:, None], (R_QW, R_V))
    return dmask, q_dec, k_dec, g_rows


def _ret_kernel(q_ref, k_ref, v_ref, dmask_ref, qdec_ref, kdec_ref, grow_ref,
                o_ref, st_ref, state_ref):
    c = pl.program_id(1)

    @pl.when(c == 0)
    def _():
        state_ref[...] = jnp.zeros(state_ref.shape, F32)

    q = q_ref[...]
    k = k_ref[...]
    qd = (q * qdec_ref[...]).astype(BF16)
    kd = (k * kdec_ref[...]).astype(BF16)
    qb = q.astype(BF16)
    kb = k.astype(BF16)
    vb = v_ref[...].astype(BF16)
    for h in range(R_HEADS):
        ks = slice(h * R_QK, (h + 1) * R_QK)
        vs = slice(h * R_V, (h + 1) * R_V)
        state = state_ref[ks, :]
        inner = lax.dot_general(qb[:, ks], kb[:, ks], (((1,), (1,)), ((), ())),
                                preferred_element_type=F32) * dmask_ref[h]
        o = (jnp.dot(inner.astype(BF16), vb[:, vs], preferred_element_type=F32)
             + jnp.dot(qd[:, ks], state.astype(BF16), preferred_element_type=F32))
        o_ref[:, vs] = o
        upd = lax.dot_general(kd[:, ks], vb[:, vs], (((0,), (0,)), ((), ())),
                              preferred_element_type=F32)
        state_ref[ks, :] = grow_ref[ks, :] * state + upd
    st_ref[...] = state_ref[...]


def _retention(rq, rk, rv, batch, seq):
    chunk = RET_CHUNK
    nc = seq // chunk
    dmask, q_dec, k_dec, g_rows = _ret_tables_chunk(chunk)
    row = lambda w: pl.BlockSpec((chunk, w), lambda b, c: (b * nc + c, 0))
    full = lambda a: pl.BlockSpec(a.shape, lambda b, c: (0,) * a.ndim)
    return pl.pallas_call(
        _ret_kernel,
        grid=(batch, nc),
        in_specs=[row(R_QW), row(R_QW), row(R_WIDTH), full(dmask), full(q_dec), full(k_dec), full(g_rows)],
        out_specs=(row(R_WIDTH), pl.BlockSpec((None, R_QW, R_V), lambda b, c: (b, 0, 0))),
        out_shape=(jax.ShapeDtypeStruct((batch * seq, R_WIDTH), F32),
                   jax.ShapeDtypeStruct((batch, R_QW, R_V), F32)),
        scratch_shapes=[pltpu.VMEM((R_QW, R_V), F32)],
        compiler_params=_cparams(("parallel", "arbitrary")),
        name="retention",
    )(rq, rk, rv, dmask, q_dec, k_dec, g_rows)


def _ret_step_kernel(q_ref, k_ref, v_ref, state_ref, grow_ref, o_ref, ns_ref):
    g = grow_ref[...]
    for t in range(SAMPLE_BLOCK):
        kcol = jnp.broadcast_to(k_ref[t:t + 1, :], (LANES, R_QW)).T
        qcol = jnp.broadcast_to(q_ref[t:t + 1, :], (LANES, R_QW)).T
        vrows = jnp.concatenate(
            [jnp.broadcast_to(v_ref[t:t + 1, h * R_V:(h + 1) * R_V], (R_QK, R_V)) for h in range(R_HEADS)],
            axis=0)
        new = g * state_ref[t] + kcol * vrows
        ns_ref[t] = new
        qn = qcol * new
        for h in range(R_HEADS):
            o_ref[t:t + 1, h * R_V:(h + 1) * R_V] = jnp.sum(qn[h * R_QK:(h + 1) * R_QK, :], axis=0, keepdims=True)


def _ret_step(rq, rk, rv, state):
    n = rq.shape[0]
    bb = SAMPLE_BLOCK
    log_g = jnp.log(1.0 - 2.0 ** (-5.0 - jnp.arange(R_HEADS, dtype=F32)))
    g_rows = jnp.broadcast_to(jnp.repeat(jnp.exp(log_g), R_QK)[:, None], (R_QW, R_V))
    row = lambda w: pl.BlockSpec((bb, w), lambda i: (i, 0))
    st = pl.BlockSpec((bb, R_QW, R_V), lambda i: (i, 0, 0))
    return pl.pallas_call(
        _ret_step_kernel,
        grid=(n // bb,),
        in_specs=[row(R_QW), row(R_QW), row(R_WIDTH), st, pl.BlockSpec((R_QW, R_V), lambda i: (0, 0))],
        out_specs=(row(R_WIDTH), st),
        out_shape=(jax.ShapeDtypeStruct((n, R_WIDTH), F32), jax.ShapeDtypeStruct((n, R_QW, R_V), F32)),
        compiler_params=_cparams(("parallel",)),
        name="ret_step",
    )(rq, rk, rv, state, g_rows)


def _paged_kernel(pt_ref, q_ref, kn_ref, vn_ref, bias_ref, nbias_ref,
                  lq1_ref, lk1_ref, lq2_ref, lk2_ref, *rest, npages, lam_init):
    k_refs = rest[:npages]
    v_refs = rest[npages:2 * npages]
    o_ref = rest[2 * npages]
    q = q_ref[...]
    nrow = 2 * A_HEADS

    def head_rows(x, n):
        row = lax.broadcasted_iota(I32, (n, A_ROW), 0)
        out = jnp.zeros((n, A_ROW), F32)
        for h in range(A_HEADS):
            out = jnp.where(row == h, jnp.broadcast_to(x[:, h * A_ROW:(h + 1) * A_ROW], (n, A_ROW)), out)
        return out

    row8 = lax.broadcasted_iota(I32, (nrow, A_ROW), 0)
    lane8 = lax.broadcasted_iota(I32, (nrow, A_ROW), 1)
    q4 = head_rows(q, nrow)
    q8 = q4 + pltpu.roll(q4, A_HEADS, 0)
    qm = jnp.where((row8 < A_HEADS) == (lane8 < A_QK), q8, 0.0).astype(BF16)

    nt = (((1,), (1,)), ((), ()))
    bias = bias_ref[...]
    s = [lax.dot_general(qm, k_refs[j][...].astype(BF16), nt, preferred_element_type=F32) + bias
         for j in range(npages)]
    kn = head_rows(kn_ref[...], LANES).astype(BF16)
    s.append(lax.dot_general(qm, kn, nt, preferred_element_type=F32) + nbias_ref[...])
    m = functools.reduce(jnp.maximum, [jnp.max(x, axis=-1, keepdims=True) for x in s])
    p = [jnp.exp(x - m) for x in s]
    l = functools.reduce(lambda a, b: a + b, [jnp.sum(x, axis=-1, keepdims=True) for x in p])
    inv = 1.0 / l
    lam = _lambda(lq1_ref, lk1_ref, lq2_ref, lk2_ref, lam_init)
    vs = [v_refs[j][...].astype(BF16) for j in range(npages)] + [head_rows(vn_ref[...], LANES).astype(BF16)]
    out = jnp.zeros((nrow, A_ROW), F32)
    for pj, vj in zip(p, vs):
        pn = pj * inv
        first = lax.broadcasted_iota(I32, pn.shape, 0) < A_HEADS
        w8 = jnp.where(first, pn - lam * pltpu.roll(pn, A_HEADS, 0), 0.0).astype(BF16)
        out = out + jnp.dot(w8, vj, preferred_element_type=F32)
    for h in range(A_HEADS):
        o_ref[:, h * A_ROW:(h + 1) * A_ROW] = out[h:h + 1, :]


def _paged_attn(page_table, q, k_new, v_new, cache_k, cache_v, lams, lam_init):
    nseq, npages = page_table.shape
    n_phys, page = cache_k.shape[0], cache_k.shape[1]
    prow = page * A_HEADS
    ck = cache_k.reshape(n_phys, prow, A_ROW)
    cv = cache_v.reshape(n_phys, prow, A_ROW)
    r = np.arange(2 * A_HEADS)[:, None] % A_HEADS
    j = np.arange(prow)[None, :]
    bias = jnp.asarray(np.where(j % A_HEADS == r, 0.0, NEG).astype(np.float32))
    jn = np.arange(LANES)[None, :]
    nbias = jnp.asarray(np.where(jn == r, 0.0, NEG).astype(np.float32))
    tok = lambda w: pl.BlockSpec((None, 1, w), lambda b, pt: (b, 0, 0))
    full = lambda a: pl.BlockSpec(a.shape, lambda b, pt: (0,) * a.ndim)

    def page_spec(jj):
        return pl.BlockSpec((None, prow, A_ROW), lambda b, pt: (pt[b * npages + jj], 0, 0))

    kernel = functools.partial(_paged_kernel, npages=npages, lam_init=lam_init)
    grid_spec = pltpu.PrefetchScalarGridSpec(
        num_scalar_prefetch=1,
        grid=(nseq,),
        in_specs=[tok(A_WIDTH), tok(A_WIDTH), tok(A_WIDTH), full(bias), full(nbias)]
                 + [pl.BlockSpec((1, A_QK), lambda b, pt: (0, 0))] * 4
                 + [page_spec(jj) for jj in range(npages)] * 2,
        out_specs=tok(A_WIDTH),
    )
    out = pl.pallas_call(
        kernel,
        grid_spec=grid_spec,
        out_shape=jax.ShapeDtypeStruct((nseq, 1, A_WIDTH), F32),
        compiler_params=_cparams(("arbitrary",)),
        name="paged_attn",
    )(page_table.reshape(-1), q.reshape(nseq, 1, A_WIDTH), k_new.reshape(nseq, 1, A_WIDTH),
      v_new.reshape(nseq, 1, A_WIDTH), bias, nbias, *lams, *([ck] * npages), *([cv] * npages))
    return out.reshape(nseq, A_WIDTH)


def _mix_tile(x_ref, a_ref, r_ref, g_ref, asub_ref, rnorm_ref, wo_ref, n2_ref,
              wrh_ref, wrl_ref, br_ref, h_ref, xn_ref, route_ref, *, lam_init):
    tm = x_ref.shape[0]
    parts = []
    for hd in range(A_HEADS):
        a = a_ref[:, hd * A_ROW:(hd + 1) * A_ROW]
        ms = jnp.mean(a * a, axis=-1, keepdims=True)
        parts.append((a * lax.rsqrt(ms + EPS) * asub_ref[...] * (1.0 - lam_init)).astype(BF16))
    for hd in range(R_HEADS):
        sl = slice(hd * R_V, (hd + 1) * R_V)
        r = r_ref[:, sl]
        ms = jnp.mean(r * r, axis=-1, keepdims=True)
        gate = g_ref[:, sl]
        gate = gate * (1.0 / (1.0 + jnp.exp(-gate)))
        parts.append((r * lax.rsqrt(ms + EPS) * rnorm_ref[...] * gate).astype(BF16))
    merged = jnp.concatenate(parts, axis=1)
    h = x_ref[...] + jnp.dot(merged, wo_ref[...], preferred_element_type=F32)
    h_ref[...] = h
    ms = jnp.mean(h * h, axis=-1, keepdims=True)
    xn = h * lax.rsqrt(ms + EPS) * n2_ref[...]
    for c in range(CHUNKS):
        xn_ref[pl.ds(c, tm, stride=CHUNKS), :] = xn[:, c * LANES:(c + 1) * LANES]
    xh = xn.astype(BF16)
    xl = (xn - xh.astype(F32)).astype(BF16)
    logits = (jnp.dot(xh, wrh_ref[...], preferred_element_type=F32)
              + jnp.dot(xl, wrh_ref[...], preferred_element_type=F32)
              + jnp.dot(xh, wrl_ref[...], preferred_element_type=F32)) + br_ref[...]
    lane = lax.broadcasted_iota(I32, logits.shape, 1)
    big = jnp.int32(LANES)
    gl = jnp.where(lane < N_GROUPS, logits, NEG)
    gmax = jnp.max(gl, axis=-1, keepdims=True)
    gidx = jnp.min(jnp.where(gl == gmax, lane, big), axis=-1, keepdims=True)
    gsum = jnp.sum(jnp.where(lane < N_GROUPS, jnp.exp(gl - gmax), 0.0), axis=-1, keepdims=True)
    gprob = 1.0 / gsum
    lo = N_GROUPS + EXPERTS_PER_GROUP * gidx
    el = jnp.where((lane >= lo) & (lane < lo + EXPERTS_PER_GROUP), logits, NEG)
    v1 = jnp.max(el, axis=-1, keepdims=True)
    i1 = jnp.min(jnp.where(el == v1, lane, big), axis=-1, keepdims=True)
    el2 = jnp.where(lane == i1, NEG, el)
    v2 = jnp.max(el2, axis=-1, keepdims=True)
    i2 = jnp.min(jnp.where(el2 == v2, lane, big), axis=-1, keepdims=True)
    e = jnp.exp(v2 - v1)
    w1 = gprob / (1.0 + e)
    w2 = gprob * e / (1.0 + e)
    e1 = (i1 - N_GROUPS).astype(F32)
    e2 = (i2 - N_GROUPS).astype(F32)
    route_ref[...] = jnp.where(lane == 0, e1, jnp.where(lane == 1, e2, jnp.where(
        lane == 2, w1, jnp.where(lane == 3, w2, 0.0))))


def _mix_kernel(*refs, lam_init, n_tiles, has_tail):
    if not has_tail:
        _mix_tile(*refs, lam_init=lam_init)
        return
    ins, (th_ref, txn_ref, troute_ref), outs = refs[:11], refs[11:14], refs[14:]
    h_ref, xn_ref, route_ref = outs
    i = pl.program_id(0)

    @pl.when(i < n_tiles)
    def _():
        _mix_tile(*ins, *outs, lam_init=lam_init)

    @pl.when(i == n_tiles)
    def _():
        tm = h_ref.shape[0]
        nt = th_ref.shape[0]
        h_ref[:nt, :] = th_ref[...]
        h_ref[nt:, :] = jnp.zeros((tm - nt, D_MODEL), F32)
        xn_ref[:nt * CHUNKS, :] = txn_ref[...]
        xn_ref[nt * CHUNKS:, :] = jnp.zeros(((tm - nt) * CHUNKS, LANES), F32)
        route_ref[:nt, :] = troute_ref[...]
        route_ref[nt:, :] = jnp.zeros((tm - nt, LANES), F32)


def _mix_out(x2d, a_o, r_o, rg, asub, rnorm, wo_bf, n2, wr_hi, wr_lo, br, lam_init, tm, tail=None):
    t = x2d.shape[0]
    n_tiles = t // tm
    has_tail = tail is not None
    n_out = n_tiles + (1 if has_tail else 0)
    row = lambda w: pl.BlockSpec((tm, w), lambda i: (jnp.minimum(i, n_tiles - 1), 0))
    full = lambda a: pl.BlockSpec(a.shape, lambda i: (0,) * a.ndim)
    out_shape = (jax.ShapeDtypeStruct((n_out * tm, D_MODEL), F32),
                 jax.ShapeDtypeStruct((n_out * tm * CHUNKS, LANES), F32),
                 jax.ShapeDtypeStruct((n_out * tm, LANES), F32))
    out_specs = (pl.BlockSpec((tm, D_MODEL), lambda i: (i, 0)),
                 pl.BlockSpec((tm * CHUNKS, LANES), lambda i: (i, 0)),
                 pl.BlockSpec((tm, LANES), lambda i: (i, 0)))
    in_specs = [row(D_MODEL), row(A_WIDTH), row(R_WIDTH), row(R_WIDTH), full(asub), full(rnorm),
                full(wo_bf), full(n2), full(wr_hi), full(wr_lo), full(br)]
    args = [x2d, a_o, r_o, rg, asub, rnorm, wo_bf, n2, wr_hi, wr_lo, br]
    if has_tail:
        assert tail[0].shape[0] <= tm
        in_specs += [full(a) for a in tail]
        args += list(tail)
    kernel = functools.partial(_mix_kernel, lam_init=lam_init, n_tiles=n_tiles, has_tail=has_tail)
    return pl.pallas_call(
        kernel,
        grid=(n_out,),
        in_specs=in_specs,
        out_specs=out_specs,
        out_shape=out_shape,
        compiler_params=_cparams(("arbitrary",)),
        name="mix_out",
    )(*args)


def _plan_kernel(route_ref, ltri_ref, utri_ref, dest_ref, te_ref, cnt_ref, base_ref, *, slot_tile):
    ph = pl.program_id(0)
    j = pl.program_id(1)
    route = route_ref[...]
    lane = lax.broadcasted_iota(I32, route.shape, 1)
    e1 = route[:, 0:1].astype(I32)
    e2 = route[:, 1:2].astype(I32)
    onehot = jnp.where((lane == e1) | (lane == e2), 1.0, 0.0)
    colsum = jnp.sum(onehot, axis=0, keepdims=True)

    @pl.when((ph == 0) & (j == 0))
    def _():
        cnt_ref[...] = jnp.zeros(cnt_ref.shape, F32)

    @pl.when(ph == 0)
    def _():
        cnt_ref[...] = cnt_ref[...] + colsum

    @pl.when((ph == 1) & (j == 0))
    def _():
        ntile = jnp.floor((cnt_ref[...] + (slot_tile - 1)) * (1.0 / slot_tile))
        nt8 = jnp.broadcast_to(ntile, (SUBLANES, LANES)).astype(BF16)
        base_t = jnp.dot(nt8, utri_ref[...], preferred_element_type=F32)[0:1, :]
        base_ref[...] = base_t * slot_tile
        cnt_ref[...] = jnp.zeros(cnt_ref.shape, F32)
        ends = base_t + ntile
        tl = lax.broadcasted_iota(I32, te_ref.shape, 0).astype(F32)
        el = lax.broadcasted_iota(I32, te_ref.shape, 1)
        hit = jnp.where((el < N_EXPERTS) & (ends <= tl), 1.0, 0.0)
        te = jnp.sum(hit, axis=-1, keepdims=True)
        te_ref[...] = jnp.broadcast_to(te, te_ref.shape).astype(I32)

    @pl.when(ph == 1)
    def _():
        rank = jnp.dot(ltri_ref[...], onehot.astype(BF16), preferred_element_type=F32)
        pos = base_ref[...] + cnt_ref[...] + rank
        d1 = jnp.sum(jnp.where(lane == e1, pos, 0.0), axis=-1, keepdims=True)
        d2 = jnp.sum(jnp.where(lane == e2, pos, 0.0), axis=-1, keepdims=True)
        dest_ref[...] = jnp.where(lane == 0, d1, jnp.where(lane == 1, d2, 0.0)).astype(I32)
        cnt_ref[...] = cnt_ref[...] + colsum


def _plan(route, n, n_slot_tiles):
    tile = PLAN_TILE
    te_rows = -(-n_slot_tiles // SUBLANES) * SUBLANES
    ii = np.arange(tile)
    ltri = jnp.asarray((ii[None, :] < ii[:, None]).astype(np.float32), dtype=BF16)
    ee = np.arange(LANES)
    utri = jnp.asarray((ee[:, None] < ee[None, :]).astype(np.float32), dtype=BF16)
    kernel = functools.partial(_plan_kernel, slot_tile=SLOT_TILE)
    return pl.pallas_call(
        kernel,
        grid=(2, n // tile),
        in_specs=[pl.BlockSpec((tile, LANES), lambda p, j: (j, 0)),
                  pl.BlockSpec((tile, tile), lambda p, j: (0, 0)),
                  pl.BlockSpec((LANES, LANES), lambda p, j: (0, 0))],
        out_specs=(pl.BlockSpec((tile, LANES), lambda p, j: (j * p, 0)),
                   pl.BlockSpec((te_rows, LANES), lambda p, j: (0, 0))),
        out_shape=(jax.ShapeDtypeStruct((n, LANES), I32),
                   jax.ShapeDtypeStruct((te_rows, LANES), I32)),
        scratch_shapes=[pltpu.VMEM((1, LANES), F32), pltpu.VMEM((1, LANES), F32)],
        compiler_params=_cparams(("arbitrary", "arbitrary")),
        name="plan",
    )(route, ltri, utri)


def _invert_kernel(dest_ref, src_ref, *, n_assign, n_slots):
    def init(s, c):
        src_ref[s] = 0
        return c

    lax.fori_loop(0, n_slots, init, 0, unroll=8)

    def scatter(a, c):
        src_ref[dest_ref[a]] = a
        return c

    lax.fori_loop(0, n_assign, scatter, 0, unroll=8)


def _invert(dest_flat, n_slots):
    n_assign = dest_flat.shape[0]
    kernel = functools.partial(_invert_kernel, n_assign=n_assign, n_slots=n_slots)
    return pl.pallas_call(
        kernel,
        in_specs=[pl.BlockSpec(memory_space=pltpu.SMEM)],
        out_specs=pl.BlockSpec(memory_space=pltpu.SMEM),
        out_shape=jax.ShapeDtypeStruct((n_slots,), I32),
        name="invert",
    )(dest_flat)


def _row_gather(src_hbm, idx_of_row, buf, sem, n_rows):
    def body(r, c):
        pltpu.make_async_copy(src_hbm.at[idx_of_row(r)],
                              buf.at[pl.ds(pl.multiple_of(r * CHUNKS, CHUNKS), CHUNKS), :], sem).start()
        return c

    lax.fori_loop(0, n_rows, body, 0, unroll=8)


def _row_wait(src_hbm, buf, sem, n_rows):
    def body(r, c):
        pltpu.make_async_copy(src_hbm.at[0], buf.at[pl.ds(0, CHUNKS), :], sem).wait()
        return c

    lax.fori_loop(0, n_rows, body, 0, unroll=8)


def _gathered_rows(buf, n_rows):
    return jnp.concatenate([buf[pl.ds(c, n_rows, stride=CHUNKS), :] for c in range(CHUNKS)], axis=1)


def _expert_kernel(te_ref, src_ref, xn_hbm, wg_ref, wu_ref, wd_ref, ys_ref,
                   xbuf, sem, wg_bf, wu_bf, wd_bf, *, n_tiles):
    i = pl.program_id(0)
    ts = SLOT_TILE
    slot = i % 2

    def start(tile, sl):
        base = tile * ts
        _row_gather(xn_hbm, lambda r: src_ref[base + r] >> 1, xbuf.at[sl], sem.at[sl], ts)

    @pl.when(i == 0)
    def _():
        start(0, 0)

    @pl.when(i + 1 < n_tiles)
    def _():
        start(i + 1, 1 - slot)

    e = te_ref[i]
    prev = te_ref[jnp.maximum(i - 1, 0)]

    @pl.when((i == 0) | (e != prev))
    def _():
        wg_bf[...] = wg_ref[...].astype(BF16)
        wu_bf[...] = wu_ref[...].astype(BF16)
        wd_bf[...] = wd_ref[...].astype(BF16)

    _row_wait(xn_hbm, xbuf.at[slot], sem.at[slot], ts)

    @pl.when(e < N_EXPERTS)
    def _():
        x = _gathered_rows(xbuf.at[slot], ts).astype(BF16)
        a = jnp.dot(x, wg_bf[...], preferred_element_type=F32)
        u = jnp.dot(x, wu_bf[...], preferred_element_type=F32)
        hmid = (a * (1.0 / (1.0 + jnp.exp(-a))) * u).astype(BF16)
        y = jnp.dot(hmid, wd_bf[...], preferred_element_type=F32)
        for c in range(CHUNKS):
            ys_ref[pl.ds(c, ts, stride=CHUNKS), :] = y[:, c * LANES:(c + 1) * LANES]

    @pl.when(e >= N_EXPERTS)
    def _():
        ys_ref[...] = jnp.zeros(ys_ref.shape, F32)


def _experts(te, src, xn3d, w_gate, w_up, w_down, n_tiles):
    ts = SLOT_TILE
    wsel = lambda i, te, src: (jnp.minimum(te[i], N_EXPERTS - 1), 0, 0)
    kernel = functools.partial(_expert_kernel, n_tiles=n_tiles)
    grid_spec = pltpu.PrefetchScalarGridSpec(
        num_scalar_prefetch=2,
        grid=(n_tiles,),
        in_specs=[pl.BlockSpec(memory_space=pl.ANY),
                  pl.BlockSpec((None, D_MODEL, D_EXPERT), wsel),
                  pl.BlockSpec((None, D_MODEL, D_EXPERT), wsel),
                  pl.BlockSpec((None, D_EXPERT, D_MODEL), wsel)],
        out_specs=pl.BlockSpec((ts * CHUNKS, LANES), lambda i, te, src: (i, 0)),
        scratch_shapes=[pltpu.VMEM((2, ts * CHUNKS, LANES), F32),
                        pltpu.SemaphoreType.DMA((2,)),
                        pltpu.VMEM((D_MODEL, D_EXPERT), BF16),
                        pltpu.VMEM((D_MODEL, D_EXPERT), BF16),
                        pltpu.VMEM((D_EXPERT, D_MODEL), BF16)],
    )
    return pl.pallas_call(
        kernel,
        grid_spec=grid_spec,
        out_shape=jax.ShapeDtypeStruct((n_tiles * ts * CHUNKS, LANES), F32),
        compiler_params=_cparams(("arbitrary",)),
        name="experts",
    )(te, src, xn3d, w_gate, w_up, w_down)


def _combine_kernel(dest_ref, ys_hbm, h_ref, route_ref, y_ref, gbuf, sem, *, tm, n_steps, tok_off):
    i = pl.program_id(0)
    slot = i % 2
    nrow = 2 * tm

    def start(step, sl):
        base = (tok_off + step * tm) * 2
        _row_gather(ys_hbm, lambda r: dest_ref[base + r], gbuf.at[sl], sem.at[sl], nrow)

    @pl.when(i == 0)
    def _():
        start(0, 0)

    @pl.when(i + 1 < n_steps)
    def _():
        start(i + 1, 1 - slot)

    _row_wait(ys_hbm, gbuf.at[slot], sem.at[slot], nrow)
    buf = gbuf.at[slot]
    route = route_ref[...]
    w1 = route[:, 2:3]
    w2 = route[:, 3:4]
    for c in range(CHUNKS):
        g1 = buf[pl.ds(c, tm, stride=2 * CHUNKS), :]
        g2 = buf[pl.ds(CHUNKS + c, tm, stride=2 * CHUNKS), :]
        sl = slice(c * LANES, (c + 1) * LANES)
        y_ref[:, sl] = h_ref[:, sl] + w1 * g1 + w2 * g2


def _combine(dest_flat, ys3d, h_pool, route, tm, tok_off, n_tok):
    n_steps = n_tok // tm
    boff = tok_off // tm
    kernel = functools.partial(_combine_kernel, tm=tm, n_steps=n_steps, tok_off=tok_off)
    grid_spec = pltpu.PrefetchScalarGridSpec(
        num_scalar_prefetch=1,
        grid=(n_steps,),
        in_specs=[pl.BlockSpec(memory_space=pl.ANY),
                  pl.BlockSpec((tm, D_MODEL), lambda i, d: (i + boff, 0)),
                  pl.BlockSpec((tm, LANES), lambda i, d: (i + boff, 0))],
        out_specs=pl.BlockSpec((tm, D_MODEL), lambda i, d: (i, 0)),
        scratch_shapes=[pltpu.VMEM((2, 2 * tm * CHUNKS, LANES), F32), pltpu.SemaphoreType.DMA((2,))],
    )
    return pl.pallas_call(
        kernel,
        grid_spec=grid_spec,
        out_shape=jax.ShapeDtypeStruct((n_tok, D_MODEL), F32),
        compiler_params=_cparams(("arbitrary",)),
        name="combine",
    )(dest_flat, ys3d, h_pool, route)


def kernel(x_prompt, x_sample, cache_k, cache_v, state_ret, page_table, norm1, w_in, a_q_norm, a_k_norm,
           a_lambda_q1, a_lambda_k1, a_lambda_q2, a_lambda_k2, a_subln, r_norm, w_o, norm2,
           w_group_router, b_group_router, w_expert_router, b_expert_router, w_gate, w_up, w_down):
    depth = norm1.shape[0]
    assert depth == 1, "single-layer step"
    batch, seq, d = x_prompt.shape
    nsamp, tdec, _ = x_sample.shape
    assert d == D_MODEL and tdec == 1
    assert seq % TOKEN_TILE == 0 and seq % ATTN_TILE == 0 and seq % RET_CHUNK == 0
    assert nsamp % PLAN_TILE == 0 and nsamp % SAMPLE_BLOCK == 0
    past = page_table.shape[1] * cache_k.shape[2]
    n_prompt = batch * seq
    n_total = n_prompt + nsamp
    lam_init = 0.8 - 0.6 * math.exp(-0.3 * 0)
    l = 0

    lams = tuple(a[l].reshape(1, A_QK) for a in (a_lambda_q1, a_lambda_k1, a_lambda_q2, a_lambda_k2))
    w_in_bf = w_in[l].astype(BF16)
    w_o_bf = w_o[l].astype(BF16)
    g1 = norm1[l].reshape(1, D_MODEL)
    n2 = norm2[l].reshape(1, D_MODEL)
    qg = jnp.tile(a_q_norm[l], LANES // A_QK).reshape(1, LANES)
    kg = jnp.tile(a_k_norm[l], LANES // A_QK).reshape(1, LANES)
    asub = a_subln[l].reshape(1, A_ROW)
    rnorm = r_norm[l].reshape(1, R_V)
    ones = _segment_ones()
    w_r = jnp.concatenate([w_group_router[l], w_expert_router[l]], axis=1)
    w_r = jnp.pad(w_r, ((0, 0), (0, LANES - w_r.shape[1])))
    wr_hi = w_r.astype(BF16)
    wr_lo = (w_r - wr_hi.astype(F32)).astype(BF16)
    b_r = jnp.concatenate([b_group_router[l], b_expert_router[l]])
    b_r = jnp.pad(b_r, (0, LANES - b_r.shape[0])).reshape(1, LANES)

    pos_s = jnp.full((nsamp,), past, dtype=jnp.int32)
    xs = x_sample.reshape(nsamp, D_MODEL)
    (sq_bf, k_s, _, v_s, _, srq, srk, srv, srg) = _proj(
        xs, g1, w_in_bf, qg, kg, _rope_tables(pos_s), _ret_tables(pos_s), ones, PLAN_TILE)
    a_o_s = _paged_attn(page_table, sq_bf.astype(F32), k_s, v_s, cache_k[l], cache_v[l], lams, lam_init)
    r_o_s, r_state_s = _ret_step(srq, srk, srv, state_ret[l].reshape(nsamp, R_QW, R_V))
    sample_rows = _mix_out(xs, a_o_s, r_o_s, srg, asub, rnorm, w_o_bf, n2, wr_hi, wr_lo, b_r,
                           lam_init, PLAN_TILE)

    pos_p = jnp.tile(jnp.arange(seq), batch)
    xp = x_prompt.reshape(n_prompt, D_MODEL)
    (q_bf, k_p, k_bf, v_p, v_bf, rq, rk, rv, rg) = _proj(
        xp, g1, w_in_bf, qg, kg, _rope_tables(pos_p), _ret_tables(pos_p), ones, TOKEN_TILE)
    a_o = _attn(q_bf, k_bf, v_bf, lams, batch, seq, lam_init)
    r_o, r_state_p = _retention(rq, rk, rv, batch, seq)
    h_pool, xn_pool, route = _mix_out(xp, a_o, r_o, rg, asub, rnorm, w_o_bf, n2, wr_hi, wr_lo, b_r,
                                      lam_init, TOKEN_TILE, tail=sample_rows)
    n_pool = h_pool.shape[0]

    n_assign = 2 * n_total
    n_tiles = -(-(n_assign + N_EXPERTS * (SLOT_TILE - 1)) // SLOT_TILE)
    n_slots = n_tiles * SLOT_TILE
    dest128, te128 = _plan(route, n_total, n_tiles)
    dest_flat = dest128[:, :2].reshape(-1)
    te = te128[:n_tiles, 0]
    src = _invert(dest_flat, n_slots)
    ys = _experts(te, src, xn_pool.reshape(n_pool, CHUNKS, LANES),
                  w_gate[l].reshape(N_EXPERTS, D_MODEL, D_EXPERT),
                  w_up[l].reshape(N_EXPERTS, D_MODEL, D_EXPERT),
                  w_down[l].reshape(N_EXPERTS, D_EXPERT, D_MODEL), n_tiles)
    ys3d = ys.reshape(n_slots, CHUNKS, LANES)
    y_p = _combine(dest_flat, ys3d, h_pool, route, TOKEN_TILE, 0, n_prompt)
    y_s = _combine(dest_flat, ys3d, h_pool, route, PLAN_TILE, n_prompt, nsamp)

    return (y_p.reshape(batch, seq, D_MODEL),
            y_s.reshape(nsamp, 1, D_MODEL),
            k_p.reshape(1, batch, seq, A_HEADS, A_ROW),
            v_p.reshape(1, batch, seq, A_HEADS, A_ROW),
            r_state_p.reshape(1, batch, R_HEADS, R_QK, R_V),
            k_s.reshape(1, nsamp, 1, A_HEADS, A_ROW),
            v_s.reshape(1, nsamp, 1, A_HEADS, A_ROW),
            r_state_s.reshape(1, nsamp, R_HEADS, R_QK, R_V))
```

```python
import functools
import math

import numpy as np
import jax
import jax.numpy as jnp
from jax import lax
from jax.experimental import pallas as pl
from jax.experimental.pallas import tpu as pltpu

F32 = jnp.float32
BF16 = jnp.bfloat16
I32 = jnp.int32

LANES = 128
SUBLANES = 8
CHUNKS = 8

D_MODEL = 1024
A_HEADS = 4
A_QK = 64
A_ROW = 2 * A_QK
A_WIDTH = A_HEADS * A_ROW
ROPE_THETA = 500000.0
ROPE_DIM = A_QK // 4
R_HEADS = 4
R_QK = 64
R_V = 128
R_QW = R_HEADS * R_QK
R_WIDTH = R_HEADS * R_V
R_THETA = 10000.0
N_GROUPS = 4
EXPERTS_PER_GROUP = 8
N_EXPERTS = N_GROUPS * EXPERTS_PER_GROUP
D_EXPERT = 512
EPS = 1e-6
NEG = -1e30

TOKEN_TILE = 256
ATTN_TILE = 512
ONES_ROWS = 16
LOG2E = 1.4426950408889634
RET_CHUNK = 128
PLAN_TILE = 128
SLOT_TILE = 256
SAMPLE_BLOCK = 8
VMEM_LIMIT = 56 * 1024 * 1024


def _cparams(sem, vmem=VMEM_LIMIT):
    return pltpu.CompilerParams(dimension_semantics=sem, vmem_limit_bytes=vmem)


def _rope_tables(pos):
    half = ROPE_DIM // 2
    inv = ROPE_THETA ** (-jnp.arange(half, dtype=F32) / half)
    ang = pos.astype(F32)[:, None] * inv[None, :]
    cos, sin = jnp.cos(ang), jnp.sin(ang)
    n = pos.shape[0]
    ones = jnp.ones((n, A_QK - ROPE_DIM), F32)
    zeros = jnp.zeros((n, A_QK - ROPE_DIM), F32)
    zh = jnp.zeros((n, half), F32)
    c = jnp.concatenate([cos, cos, ones], axis=1)
    s_next = jnp.concatenate([-sin, zh, zeros], axis=1)
    s_prev = jnp.concatenate([zh, sin, zeros], axis=1)
    rep = LANES // A_QK
    return tuple(jnp.tile(t, (1, rep)) for t in (c, s_next, s_prev))


def _ret_tables(pos):
    half = R_QK // 2
    inv = 1.0 / (R_THETA ** jnp.linspace(0.0, 1.0, half, dtype=F32))
    ang = pos.astype(F32)[:, None] * inv[None, :]
    cos, sin = jnp.cos(ang), jnp.sin(ang)
    z = jnp.zeros_like(sin)
    c = jnp.stack([cos, cos], axis=-1).reshape(-1, R_QK)
    s_next = jnp.stack([-sin, z], axis=-1).reshape(-1, R_QK)
    s_prev = jnp.stack([z, sin], axis=-1).reshape(-1, R_QK)
    rep = LANES // R_QK
    return tuple(jnp.tile(t, (1, rep)) for t in (c, s_next, s_prev))


def _segment_ones():
    seg = np.arange(LANES) // A_QK
    return jnp.asarray((seg[:, None] == seg[None, :]).astype(np.float32), dtype=BF16)


def _proj_kernel(x_ref, g1_ref, w_ref, qg_ref, kg_ref, rc_ref, rn_ref, rp_ref,
                 tc_ref, tn_ref, tp_ref, ones_ref,
                 qbf_ref, k_ref, kbf_ref, v_ref, vbf_ref, rq_ref, rk_ref, rv_ref, rg_ref):
    x = x_ref[...]
    ms = jnp.mean(x * x, axis=-1, keepdims=True)
    n = (x * lax.rsqrt(ms + EPS) * g1_ref[...]).astype(BF16)
    h = jnp.dot(n, w_ref[...], preferred_element_type=F32)
    ones = ones_ref[...]
    rc, rn, rp = rc_ref[...], rn_ref[...], rp_ref[...]

    def head_norm_rope(xh, gain):
        sq = xh * xh
        hi = sq.astype(BF16)
        lo = (sq - hi.astype(F32)).astype(BF16)
        ssq = (jnp.dot(hi, ones, preferred_element_type=F32)
               + jnp.dot(lo, ones, preferred_element_type=F32))
        y = xh * lax.rsqrt(ssq * (1.0 / A_QK) + EPS) * gain
        half = ROPE_DIM // 2
        return (y * rc + pltpu.roll(y, LANES - half, 1) * rn + pltpu.roll(y, half, 1) * rp)

    for hd in range(A_HEADS):
        sl = slice(hd * A_ROW, (hd + 1) * A_ROW)
        q = head_norm_rope(h[:, sl], qg_ref[...])
        qbf_ref[:, sl] = (q * (A_QK ** -0.5 * LOG2E)).astype(BF16)
        k = head_norm_rope(h[:, A_WIDTH + hd * A_ROW:A_WIDTH + (hd + 1) * A_ROW], kg_ref[...])
        k_ref[:, sl] = k
        kbf_ref[:, sl] = k.astype(BF16)
    o = 2 * A_WIDTH
    v = h[:, o:o + A_WIDTH]
    v_ref[...] = v
    vbf_ref[...] = v.astype(BF16)
    o += A_WIDTH
    tc, tn, tp = tc_ref[...], tn_ref[...], tp_ref[...]

    def pair_rotate(xs):
        return xs * tc + pltpu.roll(xs, LANES - 1, 1) * tn + pltpu.roll(xs, 1, 1) * tp

    for j in range(R_QW // LANES):
        sl = slice(j * LANES, (j + 1) * LANES)
        rq_ref[:, sl] = pair_rotate(h[:, o + j * LANES:o + (j + 1) * LANES])
        rk_ref[:, sl] = pair_rotate(h[:, o + R_QW + j * LANES:o + R_QW + (j + 1) * LANES]) * (R_QK ** -0.5)
    o += 2 * R_QW
    rv_ref[...] = h[:, o:o + R_WIDTH]
    rg_ref[...] = h[:, o + R_WIDTH:o + 2 * R_WIDTH]


def _proj(x2d, g1, w_bf, qg, kg, rope_t, ret_t, ones, tm):
    t = x2d.shape[0]
    cols = w_bf.shape[1]
    row = lambda w: pl.BlockSpec((tm, w), lambda i: (i, 0))
    full = lambda a: pl.BlockSpec(a.shape, lambda i: (0,) * a.ndim)
    out_shape = (
        jax.ShapeDtypeStruct((t, A_WIDTH), BF16),
        jax.ShapeDtypeStruct((t, A_WIDTH), F32),
        jax.ShapeDtypeStruct((t, A_WIDTH), BF16),
        jax.ShapeDtypeStruct((t, A_WIDTH), F32),
        jax.ShapeDtypeStruct((t, A_WIDTH), BF16),
        jax.ShapeDtypeStruct((t, R_QW), F32),
        jax.ShapeDtypeStruct((t, R_QW), F32),
        jax.ShapeDtypeStruct((t, R_WIDTH), F32),
        jax.ShapeDtypeStruct((t, R_WIDTH), F32),
    )
    return pl.pallas_call(
        _proj_kernel,
        grid=(t // tm,),
        in_specs=[row(D_MODEL), full(g1), full(w_bf), full(qg), full(kg)]
                 + [row(LANES)] * 6 + [full(ones)],
        out_specs=tuple(row(s.shape[1]) for s in out_shape),
        out_shape=out_shape,
        compiler_params=_cparams(("parallel",)),
        name="proj",
    )(x2d, g1, w_bf, qg, kg, *rope_t, *ret_t, ones)


def _lambda(lq1_ref, lk1_ref, lq2_ref, lk2_ref, lam_init):
    s1 = jnp.sum(lq1_ref[...] * lk1_ref[...], axis=-1, keepdims=True)
    s2 = jnp.sum(lq2_ref[...] * lk2_ref[...], axis=-1, keepdims=True)
    return jnp.exp(s1) - jnp.exp(s2) + lam_init


def _attn_kernel(q_ref, k_ref, v_ref, lq1_ref, lk1_ref, lq2_ref, lk2_ref, o_ref,
                 m_ref, l_ref, acc_ref, qq_ref, sa_ref, sb_ref, *, tile, lam_init):
    i = pl.program_id(2)
    q = q_ref[...]
    lane = lax.broadcasted_iota(I32, q.shape, 1)
    zero = jnp.zeros_like(q)
    qq = jnp.concatenate([jnp.where(lane < A_QK, q, zero), jnp.where(lane >= A_QK, q, zero)], axis=0)
    qq_ref[...] = qq
    m_ref[...] = jnp.full(m_ref.shape, NEG, F32)
    l_ref[...] = jnp.zeros(l_ref.shape, F32)
    acc_ref[...] = jnp.zeros(acc_ref.shape, F32)

    ones_rows = jnp.ones((ONES_ROWS, tile), BF16)

    def scores(j, s_ref):
        k = k_ref[pl.ds(pl.multiple_of(j * tile, tile), tile), :]
        s_ref[...] = lax.dot_general(k, qq_ref[...], (((1,), (1,)), ((), ())), preferred_element_type=F32)

    def accumulate(j, s_ref, masked):
        v = v_ref[pl.ds(pl.multiple_of(j * tile, tile), tile), :]
        vt = jnp.concatenate([v.T, ones_rows], axis=0)
        s = s_ref[...]
        if masked:
            key = lax.broadcasted_iota(I32, s.shape, 0)
            qry = lax.broadcasted_iota(I32, s.shape, 1)
            qry = jnp.where(qry >= tile, qry - tile, qry)
            s = jnp.where(key <= qry, s, NEG)
        m_prev = m_ref[...]
        m_new = jnp.maximum(m_prev, jnp.max(s, axis=0, keepdims=True))
        alpha = jnp.exp2(m_prev - m_new)
        p = jnp.exp2(s - m_new).astype(BF16)
        pv = jnp.dot(vt, p, preferred_element_type=F32)
        acc_ref[...] = alpha * acc_ref[...] + pv[:A_ROW, :]
        l_ref[...] = alpha * l_ref[...] + pv[A_ROW:A_ROW + 1, :]
        m_ref[...] = m_new

    scores(0, sa_ref)

    def pair(t, carry):
        j = 2 * t
        scores(j + 1, sb_ref)
        accumulate(j, sa_ref, False)
        scores(j + 2, sa_ref)
        accumulate(j + 1, sb_ref, False)
        return carry

    lax.fori_loop(0, i // 2, pair, 0)

    @pl.when(i % 2 == 0)
    def _():
        accumulate(i, sa_ref, True)

    @pl.when(i % 2 == 1)
    def _():
        scores(i, sb_ref)
        accumulate(i - 1, sa_ref, False)
        accumulate(i, sb_ref, True)

    lam = _lambda(lq1_ref, lk1_ref, lq2_ref, lk2_ref, lam_init)
    o1 = acc_ref[:, :tile] / l_ref[:, :tile]
    o2 = acc_ref[:, tile:] / l_ref[:, tile:]
    o_ref[...] = (o1 - lam * o2).T


def _attn(q_bf, k_bf, v_bf, lams, batch, seq, lam_init):
    tile = ATTN_TILE
    nq = seq // tile
    lam_spec = pl.BlockSpec((1, A_QK), lambda b, h, i: (0, 0))
    kernel = functools.partial(_attn_kernel, tile=tile, lam_init=lam_init)
    return pl.pallas_call(
        kernel,
        grid=(batch, A_HEADS, nq),
        in_specs=[pl.BlockSpec((tile, A_ROW), lambda b, h, i: (b * nq + i, h)),
                  pl.BlockSpec((seq, A_ROW), lambda b, h, i: (b, h)),
                  pl.BlockSpec((seq, A_ROW), lambda b, h, i: (b, h))] + [lam_spec] * 4,
        out_specs=pl.BlockSpec((tile, A_ROW), lambda b, h, i: (b * nq + i, h)),
        out_shape=jax.ShapeDtypeStruct((batch * seq, A_WIDTH), F32),
        scratch_shapes=[pltpu.VMEM((1, 2 * tile), F32), pltpu.VMEM((1, 2 * tile), F32),
                        pltpu.VMEM((A_ROW, 2 * tile), F32), pltpu.VMEM((2 * tile, A_ROW), BF16),
                        pltpu.VMEM((tile, 2 * tile), F32), pltpu.VMEM((tile, 2 * tile), F32)],
        compiler_params=_cparams(("parallel", "parallel", "arbitrary")),
        name="attn",
    )(q_bf, k_bf, v_bf, *lams)


def _ret_decay():
    return [math.log(1.0 - 2.0 ** (-5.0 - h)) for h in range(R_HEADS)]


def _ret_tables_chunk(chunk):
    log_g = jnp.log(1.0 - 2.0 ** (-5.0 - jnp.arange(R_HEADS, dtype=F32)))
    idx = jnp.arange(chunk, dtype=F32)
    diff = idx[:, None] - idx[None, :]
    dmask = jnp.where(diff >= 0, jnp.exp(jnp.maximum(diff, 0.0)[None] * log_g[:, None, None]), 0.0)
    q_dec = jnp.exp((idx + 1.0)[:, None] * log_g[None, :])
    k_dec = jnp.exp((chunk - 1.0 - idx)[:, None] * log_g[None, :])
    q_dec = jnp.repeat(q_dec, R_QK, axis=1)
    k_dec = jnp.repeat(k_dec, R_QK, axis=1)
    g_chunk = jnp.exp(chunk * log_g)
    g_rows = jnp.broadcast_to(jnp.repeat(g_chunk, R_QK)[:, None], (R_QW, R_V))
    return dmask, q_dec, k_dec, g_rows


def _ret_kernel(q_ref, k_ref, v_ref, dmask_ref, qdec_ref, kdec_ref, grow_ref,
                o_ref, st_ref, state_ref):
    c = pl.program_id(1)

    @pl.when(c == 0)
    def _():
        state_ref[...] = jnp.zeros(state_ref.shape, F32)

    q = q_ref[...]
    k = k_ref[...]
    qd = (q * qdec_ref[...]).astype(BF16)
    kd = (k * kdec_ref[...]).astype(BF16)
    qb = q.astype(BF16)
    kb = k.astype(BF16)
    vb = v_ref[...].astype(BF16)
    for h in range(R_HEADS):
        ks = slice(h * R_QK, (h + 1) * R_QK)
        vs = slice(h * R_V, (h + 1) * R_V)
        state = state_ref[ks, :]
        inner = lax.dot_general(qb[:, ks], kb[:, ks], (((1,), (1,)), ((), ())),
                                preferred_element_type=F32) * dmask_ref[h]
        o = (jnp.dot(inner.astype(BF16), vb[:, vs], preferred_element_type=F32)
             + jnp.dot(qd[:, ks], state.astype(BF16), preferred_element_type=F32))
        o_ref[:, vs] = o
        upd = lax.dot_general(kd[:, ks], vb[:, vs], (((0,), (0,)), ((), ())),
                              preferred_element_type=F32)
        state_ref[ks, :] = grow_ref[ks, :] * state + upd
    st_ref[...] = state_ref[...]


def _retention(rq, rk, rv, batch, seq):
    chunk = RET_CHUNK
    nc = seq // chunk
    dmask, q_dec, k_dec, g_rows = _ret_tables_chunk(chunk)
    row = lambda w: pl.BlockSpec((chunk, w), lambda b, c: (b * nc + c, 0))
    full = lambda a: pl.BlockSpec(a.shape, lambda b, c: (0,) * a.ndim)
    return pl.pallas_call(
        _ret_kernel,
        grid=(batch, nc),
        in_specs=[row(R_QW), row(R_QW), row(R_WIDTH), full(dmask), full(q_dec), full(k_dec), full(g_rows)],
        out_specs=(row(R_WIDTH), pl.BlockSpec((None, R_QW, R_V), lambda b, c: (b, 0, 0))),
        out_shape=(jax.ShapeDtypeStruct((batch * seq, R_WIDTH), F32),
                   jax.ShapeDtypeStruct((batch, R_QW, R_V), F32)),
        scratch_shapes=[pltpu.VMEM((R_QW, R_V), F32)],
        compiler_params=_cparams(("parallel", "arbitrary")),
        name="retention",
    )(rq, rk, rv, dmask, q_dec, k_dec, g_rows)


def _ret_step_kernel(q_ref, k_ref, v_ref, state_ref, grow_ref, o_ref, ns_ref):
    g = grow_ref[...]
    for t in range(SAMPLE_BLOCK):
        kcol = jnp.broadcast_to(k_ref[t:t + 1, :], (LANES, R_QW)).T
        qcol = jnp.broadcast_to(q_ref[t:t + 1, :], (LANES, R_QW)).T
        vrows = jnp.concatenate(
            [jnp.broadcast_to(v_ref[t:t + 1, h * R_V:(h + 1) * R_V], (R_QK, R_V)) for h in range(R_HEADS)],
            axis=0)
        new = g * state_ref[t] + kcol * vrows
        ns_ref[t] = new
        qn = qcol * new
        for h in range(R_HEADS):
            o_ref[t:t + 1, h * R_V:(h + 1) * R_V] = jnp.sum(qn[h * R_QK:(h + 1) * R_QK, :], axis=0, keepdims=True)


def _ret_step(rq, rk, rv, state):
    n = rq.shape[0]
    bb = SAMPLE_BLOCK
    log_g = jnp.log(1.0 - 2.0 ** (-5.0 - jnp.arange(R_HEADS, dtype=F32)))
    g_rows = jnp.broadcast_to(jnp.repeat(jnp.exp(log_g), R_QK)[:, None], (R_QW, R_V))
    row = lambda w: pl.BlockSpec((bb, w), lambda i: (i, 0))
    st = pl.BlockSpec((bb, R_QW, R_V), lambda i: (i, 0, 0))
    return pl.pallas_call(
        _ret_step_kernel,
        grid=(n // bb,),
        in_specs=[row(R_QW), row(R_QW), row(R_WIDTH), st, pl.BlockSpec((R_QW, R_V), lambda i: (0, 0))],
        out_specs=(row(R_WIDTH), st),
        out_shape=(jax.ShapeDtypeStruct((n, R_WIDTH), F32), jax.ShapeDtypeStruct((n, R_QW, R_V), F32)),
        compiler_params=_cparams(("parallel",)),
        name="ret_step",
    )(rq, rk, rv, state, g_rows)


def _paged_kernel(pt_ref, q_ref, kn_ref, vn_ref, bias_ref, nbias_ref,
                  lq1_ref, lk1_ref, lq2_ref, lk2_ref, *rest, npages, lam_init):
    k_refs = rest[:npages]
    v_refs = rest[npages:2 * npages]
    o_ref = rest[2 * npages]
    q = q_ref[...]
    nrow = 2 * A_HEADS

    def head_rows(x, n):
        row = lax.broadcasted_iota(I32, (n, A_ROW), 0)
        out = jnp.zeros((n, A_ROW), F32)
        for h in range(A_HEADS):
            out = jnp.where(row == h, jnp.broadcast_to(x[:, h * A_ROW:(h + 1) * A_ROW], (n, A_ROW)), out)
        return out

    row8 = lax.broadcasted_iota(I32, (nrow, A_ROW), 0)
    lane8 = lax.broadcasted_iota(I32, (nrow, A_ROW), 1)
    q4 = head_rows(q, nrow)
    q8 = q4 + pltpu.roll(q4, A_HEADS, 0)
    qm = jnp.where((row8 < A_HEADS) == (lane8 < A_QK), q8, 0.0).astype(BF16)

    nt = (((1,), (1,)), ((), ()))
    bias = bias_ref[...]
    s = [lax.dot_general(qm, k_refs[j][...].astype(BF16), nt, preferred_element_type=F32) + bias
         for j in range(npages)]
    kn = head_rows(kn_ref[...], LANES).astype(BF16)
    s.append(lax.dot_general(qm, kn, nt, preferred_element_type=F32) + nbias_ref[...])
    m = functools.reduce(jnp.maximum, [jnp.max(x, axis=-1, keepdims=True) for x in s])
    p = [jnp.exp2(x - m) for x in s]
    l = functools.reduce(lambda a, b: a + b, [jnp.sum(x, axis=-1, keepdims=True) for x in p])
    inv = 1.0 / l
    lam = _lambda(lq1_ref, lk1_ref, lq2_ref, lk2_ref, lam_init)
    vs = [v_refs[j][...].astype(BF16) for j in range(npages)] + [head_rows(vn_ref[...], LANES).astype(BF16)]
    out = jnp.zeros((nrow, A_ROW), F32)
    for pj, vj in zip(p, vs):
        pn = pj * inv
        first = lax.broadcasted_iota(I32, pn.shape, 0) < A_HEADS
        w8 = jnp.where(first, pn - lam * pltpu.roll(pn, A_HEADS, 0), 0.0).astype(BF16)
        out = out + jnp.dot(w8, vj, preferred_element_type=F32)
    for h in range(A_HEADS):
        o_ref[:, h * A_ROW:(h + 1) * A_ROW] = out[h:h + 1, :]


def _paged_attn(page_table, q, k_new, v_new, cache_k, cache_v, lams, lam_init):
    nseq, npages = page_table.shape
    n_phys, page = cache_k.shape[0], cache_k.shape[1]
    prow = page * A_HEADS
    ck = cache_k.reshape(n_phys, prow, A_ROW)
    cv = cache_v.reshape(n_phys, prow, A_ROW)
    r = np.arange(2 * A_HEADS)[:, None] % A_HEADS
    j = np.arange(prow)[None, :]
    bias = jnp.asarray(np.where(j % A_HEADS == r, 0.0, NEG).astype(np.float32))
    jn = np.arange(LANES)[None, :]
    nbias = jnp.asarray(np.where(jn == r, 0.0, NEG).astype(np.float32))
    tok = lambda w: pl.BlockSpec((None, 1, w), lambda b, pt: (b, 0, 0))
    full = lambda a: pl.BlockSpec(a.shape, lambda b, pt: (0,) * a.ndim)

    def page_spec(jj):
        return pl.BlockSpec((None, prow, A_ROW), lambda b, pt: (pt[b * npages + jj], 0, 0))

    kernel = functools.partial(_paged_kernel, npages=npages, lam_init=lam_init)
    grid_spec = pltpu.PrefetchScalarGridSpec(
        num_scalar_prefetch=1,
        grid=(nseq,),
        in_specs=[tok(A_WIDTH), tok(A_WIDTH), tok(A_WIDTH), full(bias), full(nbias)]
                 + [pl.BlockSpec((1, A_QK), lambda b, pt: (0, 0))] * 4
                 + [page_spec(jj) for jj in range(npages)] * 2,
        out_specs=tok(A_WIDTH),
    )
    out = pl.pallas_call(
        kernel,
        grid_spec=grid_spec,
        out_shape=jax.ShapeDtypeStruct((nseq, 1, A_WIDTH), F32),
        compiler_params=_cparams(("arbitrary",)),
        name="paged_attn",
    )(page_table.reshape(-1), q.reshape(nseq, 1, A_WIDTH), k_new.reshape(nseq, 1, A_WIDTH),
      v_new.reshape(nseq, 1, A_WIDTH), bias, nbias, *lams, *([ck] * npages), *([cv] * npages))
    return out.reshape(nseq, A_WIDTH)


def _mix_tile(x_ref, a_ref, r_ref, g_ref, asub_ref, rnorm_ref, wo_ref, n2_ref,
              wrh_ref, wrl_ref, br_ref, h_ref, xn_ref, route_ref, *, lam_init):
    tm = x_ref.shape[0]
    parts = []
    for hd in range(A_HEADS):
        a = a_ref[:, hd * A_ROW:(hd + 1) * A_ROW]
        ms = jnp.mean(a * a, axis=-1, keepdims=True)
        parts.append((a * lax.rsqrt(ms + EPS) * asub_ref[...] * (1.0 - lam_init)).astype(BF16))
    for hd in range(R_HEADS):
        sl = slice(hd * R_V, (hd + 1) * R_V)
        r = r_ref[:, sl]
        ms = jnp.mean(r * r, axis=-1, keepdims=True)
        gate = g_ref[:, sl]
        gate = gate * (1.0 / (1.0 + jnp.exp(-gate)))
        parts.append((r * lax.rsqrt(ms + EPS) * rnorm_ref[...] * gate).astype(BF16))
    merged = jnp.concatenate(parts, axis=1)
    h = x_ref[...] + jnp.dot(merged, wo_ref[...], preferred_element_type=F32)
    h_ref[...] = h
    ms = jnp.mean(h * h, axis=-1, keepdims=True)
    xn = h * lax.rsqrt(ms + EPS) * n2_ref[...]
    for c in range(CHUNKS):
        xn_ref[pl.ds(c, tm, stride=CHUNKS), :] = xn[:, c * LANES:(c + 1) * LANES]
    xh = xn.astype(BF16)
    xl = (xn - xh.astype(F32)).astype(BF16)
    logits = (jnp.dot(xh, wrh_ref[...], preferred_element_type=F32)
              + jnp.dot(xl, wrh_ref[...], preferred_element_type=F32)
              + jnp.dot(xh, wrl_ref[...], preferred_element_type=F32)) + br_ref[...]
    lane = lax.broadcasted_iota(I32, logits.shape, 1)
    big = jnp.int32(LANES)
    gl = jnp.where(lane < N_GROUPS, logits, NEG)
    gmax = jnp.max(gl, axis=-1, keepdims=True)
    gidx = jnp.min(jnp.where(gl == gmax, lane, big), axis=-1, keepdims=True)
    gsum = jnp.sum(jnp.where(lane < N_GROUPS, jnp.exp(gl - gmax), 0.0), axis=-1, keepdims=True)
    gprob = 1.0 / gsum
    lo = N_GROUPS + EXPERTS_PER_GROUP * gidx
    el = jnp.where((lane >= lo) & (lane < lo + EXPERTS_PER_GROUP), logits, NEG)
    v1 = jnp.max(el, axis=-1, keepdims=True)
    i1 = jnp.min(jnp.where(el == v1, lane, big), axis=-1, keepdims=True)
    el2 = jnp.where(lane == i1, NEG, el)
    v2 = jnp.max(el2, axis=-1, keepdims=True)
    i2 = jnp.min(jnp.where(el2 == v2, lane, big), axis=-1, keepdims=True)
    e = jnp.exp(v2 - v1)
    w1 = gprob / (1.0 + e)
    w2 = gprob * e / (1.0 + e)
    e1 = (i1 - N_GROUPS).astype(F32)
    e2 = (i2 - N_GROUPS).astype(F32)
    route_ref[...] = jnp.where(lane == 0, e1, jnp.where(lane == 1, e2, jnp.where(
        lane == 2, w1, jnp.where(lane == 3, w2, 0.0))))


def _mix_kernel(*refs, lam_init, n_tiles, has_tail):
    if not has_tail:
        _mix_tile(*refs, lam_init=lam_init)
        return
    ins, (th_ref, txn_ref, troute_ref), outs = refs[:11], refs[11:14], refs[14:]
    h_ref, xn_ref, route_ref = outs
    i = pl.program_id(0)

    @pl.when(i < n_tiles)
    def _():
        _mix_tile(*ins, *outs, lam_init=lam_init)

    @pl.when(i == n_tiles)
    def _():
        tm = h_ref.shape[0]
        nt = th_ref.shape[0]
        h_ref[:nt, :] = th_ref[...]
        h_ref[nt:, :] = jnp.zeros((tm - nt, D_MODEL), F32)
        xn_ref[:nt * CHUNKS, :] = txn_ref[...]
        xn_ref[nt * CHUNKS:, :] = jnp.zeros(((tm - nt) * CHUNKS, LANES), F32)
        route_ref[:nt, :] = troute_ref[...]
        route_ref[nt:, :] = jnp.zeros((tm - nt, LANES), F32)


def _mix_out(x2d, a_o, r_o, rg, asub, rnorm, wo_bf, n2, wr_hi, wr_lo, br, lam_init, tm, tail=None):
    t = x2d.shape[0]
    n_tiles = t // tm
    has_tail = tail is not None
    n_out = n_tiles + (1 if has_tail else 0)
    row = lambda w: pl.BlockSpec((tm, w), lambda i: (jnp.minimum(i, n_tiles - 1), 0))
    full = lambda a: pl.BlockSpec(a.shape, lambda i: (0,) * a.ndim)
    out_shape = (jax.ShapeDtypeStruct((n_out * tm, D_MODEL), F32),
                 jax.ShapeDtypeStruct((n_out * tm * CHUNKS, LANES), F32),
                 jax.ShapeDtypeStruct((n_out * tm, LANES), F32))
    out_specs = (pl.BlockSpec((tm, D_MODEL), lambda i: (i, 0)),
                 pl.BlockSpec((tm * CHUNKS, LANES), lambda i: (i, 0)),
                 pl.BlockSpec((tm, LANES), lambda i: (i, 0)))
    in_specs = [row(D_MODEL), row(A_WIDTH), row(R_WIDTH), row(R_WIDTH), full(asub), full(rnorm),
                full(wo_bf), full(n2), full(wr_hi), full(wr_lo), full(br)]
    args = [x2d, a_o, r_o, rg, asub, rnorm, wo_bf, n2, wr_hi, wr_lo, br]
    if has_tail:
        assert tail[0].shape[0] <= tm
        in_specs += [full(a) for a in tail]
        args += list(tail)
    kernel = functools.partial(_mix_kernel, lam_init=lam_init, n_tiles=n_tiles, has_tail=has_tail)
    return pl.pallas_call(
        kernel,
        grid=(n_out,),
        in_specs=in_specs,
        out_specs=out_specs,
        out_shape=out_shape,
        compiler_params=_cparams(("arbitrary",)),
        name="mix_out",
    )(*args)


def _plan_kernel(route_ref, ltri_ref, utri_ref, dest_ref, te_ref, cnt_ref, base_ref, *, slot_tile):
    ph = pl.program_id(0)
    j = pl.program_id(1)
    route = route_ref[...]
    lane = lax.broadcasted_iota(I32, route.shape, 1)
    e1 = route[:, 0:1].astype(I32)
    e2 = route[:, 1:2].astype(I32)
    onehot = jnp.where((lane == e1) | (lane == e2), 1.0, 0.0)
    colsum = jnp.sum(onehot, axis=0, keepdims=True)

    @pl.when((ph == 0) & (j == 0))
    def _():
        cnt_ref[...] = jnp.zeros(cnt_ref.shape, F32)

    @pl.when(ph == 0)
    def _():
        cnt_ref[...] = cnt_ref[...] + colsum

    @pl.when((ph == 1) & (j == 0))
    def _():
        ntile = jnp.floor((cnt_ref[...] + (slot_tile - 1)) * (1.0 / slot_tile))
        nt8 = jnp.broadcast_to(ntile, (SUBLANES, LANES)).astype(BF16)
        base_t = jnp.dot(nt8, utri_ref[...], preferred_element_type=F32)[0:1, :]
        base_ref[...] = base_t * slot_tile
        cnt_ref[...] = jnp.zeros(cnt_ref.shape, F32)
        ends = base_t + ntile
        tl = lax.broadcasted_iota(I32, te_ref.shape, 0).astype(F32)
        el = lax.broadcasted_iota(I32, te_ref.shape, 1)
        hit = jnp.where((el < N_EXPERTS) & (ends <= tl), 1.0, 0.0)
        te = jnp.sum(hit, axis=-1, keepdims=True)
        te_ref[...] = jnp.broadcast_to(te, te_ref.shape).astype(I32)

    @pl.when(ph == 1)
    def _():
        rank = jnp.dot(ltri_ref[...], onehot.astype(BF16), preferred_element_type=F32)
        pos = base_ref[...] + cnt_ref[...] + rank
        d1 = jnp.sum(jnp.where(lane == e1, pos, 0.0), axis=-1, keepdims=True)
        d2 = jnp.sum(jnp.where(lane == e2, pos, 0.0), axis=-1, keepdims=True)
        dest_ref[...] = jnp.where(lane == 0, d1, jnp.where(lane == 1, d2, 0.0)).astype(I32)
        cnt_ref[...] = cnt_ref[...] + colsum


def _plan(route, n, n_slot_tiles):
    tile = PLAN_TILE
    te_rows = -(-n_slot_tiles // SUBLANES) * SUBLANES
    ii = np.arange(tile)
    ltri = jnp.asarray((ii[None, :] < ii[:, None]).astype(np.float32), dtype=BF16)
    ee = np.arange(LANES)
    utri = jnp.asarray((ee[:, None] < ee[None, :]).astype(np.float32), dtype=BF16)
    kernel = functools.partial(_plan_kernel, slot_tile=SLOT_TILE)
    return pl.pallas_call(
        kernel,
        grid=(2, n // tile),
        in_specs=[pl.BlockSpec((tile, LANES), lambda p, j: (j, 0)),
                  pl.BlockSpec((tile, tile), lambda p, j: (0, 0)),
                  pl.BlockSpec((LANES, LANES), lambda p, j: (0, 0))],
        out_specs=(pl.BlockSpec((tile, LANES), lambda p, j: (j * p, 0)),
                   pl.BlockSpec((te_rows, LANES), lambda p, j: (0, 0))),
        out_shape=(jax.ShapeDtypeStruct((n, LANES), I32),
                   jax.ShapeDtypeStruct((te_rows, LANES), I32)),
        scratch_shapes=[pltpu.VMEM((1, LANES), F32), pltpu.VMEM((1, LANES), F32)],
        compiler_params=_cparams(("arbitrary", "arbitrary")),
        name="plan",
    )(route, ltri, utri)


def _invert_kernel(dest_ref, src_ref, *, n_assign, n_slots):
    def init(s, c):
        src_ref[s] = 0
        return c

    lax.fori_loop(0, n_slots, init, 0, unroll=8)

    def scatter(a, c):
        src_ref[dest_ref[a]] = a
        return c

    lax.fori_loop(0, n_assign, scatter, 0, unroll=8)


def _invert(dest_flat, n_slots):
    n_assign = dest_flat.shape[0]
    kernel = functools.partial(_invert_kernel, n_assign=n_assign, n_slots=n_slots)
    return pl.pallas_call(
        kernel,
        in_specs=[pl.BlockSpec(memory_space=pltpu.SMEM)],
        out_specs=pl.BlockSpec(memory_space=pltpu.SMEM),
        out_shape=jax.ShapeDtypeStruct((n_slots,), I32),
        name="invert",
    )(dest_flat)


def _row_gather(src_hbm, idx_of_row, buf, sem, n_rows):
    def body(r, c):
        pltpu.make_async_copy(src_hbm.at[idx_of_row(r)],
                              buf.at[pl.ds(pl.multiple_of(r * CHUNKS, CHUNKS), CHUNKS), :], sem).start()
        return c

    lax.fori_loop(0, n_rows, body, 0, unroll=8)


def _row_wait(src_hbm, buf, sem, n_rows):
    def body(r, c):
        pltpu.make_async_copy(src_hbm.at[0], buf.at[pl.ds(0, CHUNKS), :], sem).wait()
        return c

    lax.fori_loop(0, n_rows, body, 0, unroll=8)


def _gathered_rows(buf, n_rows):
    return jnp.concatenate([buf[pl.ds(c, n_rows, stride=CHUNKS), :] for c in range(CHUNKS)], axis=1)


def _expert_kernel(te_ref, src_ref, xn_hbm, wg_ref, wu_ref, wd_ref, ys_ref,
                   xbuf, sem, wg_bf, wu_bf, wd_bf, *, n_tiles):
    i = pl.program_id(0)
    ts = SLOT_TILE
    slot = i % 2

    def start(tile, sl):
        base = tile * ts
        _row_gather(xn_hbm, lambda r: src_ref[base + r] >> 1, xbuf.at[sl], sem.at[sl], ts)

    @pl.when(i == 0)
    def _():
        start(0, 0)

    @pl.when(i + 1 < n_tiles)
    def _():
        start(i + 1, 1 - slot)

    e = te_ref[i]
    prev = te_ref[jnp.maximum(i - 1, 0)]

    @pl.when((i == 0) | (e != prev))
    def _():
        wg_bf[...] = wg_ref[...].astype(BF16)
        wu_bf[...] = wu_ref[...].astype(BF16)
        wd_bf[...] = wd_ref[...].astype(BF16)

    _row_wait(xn_hbm, xbuf.at[slot], sem.at[slot], ts)

    @pl.when(e < N_EXPERTS)
    def _():
        x = _gathered_rows(xbuf.at[slot], ts).astype(BF16)
        a = jnp.dot(x, wg_bf[...], preferred_element_type=F32)
        u = jnp.dot(x, wu_bf[...], preferred_element_type=F32)
        hmid = (a * (1.0 / (1.0 + jnp.exp(-a))) * u).astype(BF16)
        y = jnp.dot(hmid, wd_bf[...], preferred_element_type=F32)
        for c in range(CHUNKS):
            ys_ref[pl.ds(c, ts, stride=CHUNKS), :] = y[:, c * LANES:(c + 1) * LANES]

    @pl.when(e >= N_EXPERTS)
    def _():
        ys_ref[...] = jnp.zeros(ys_ref.shape, F32)


def _experts(te, src, xn3d, w_gate, w_up, w_down, n_tiles):
    ts = SLOT_TILE
    wsel = lambda i, te, src: (jnp.minimum(te[i], N_EXPERTS - 1), 0, 0)
    kernel = functools.partial(_expert_kernel, n_tiles=n_tiles)
    grid_spec = pltpu.PrefetchScalarGridSpec(
        num_scalar_prefetch=2,
        grid=(n_tiles,),
        in_specs=[pl.BlockSpec(memory_space=pl.ANY),
                  pl.BlockSpec((None, D_MODEL, D_EXPERT), wsel),
                  pl.BlockSpec((None, D_MODEL, D_EXPERT), wsel),
                  pl.BlockSpec((None, D_EXPERT, D_MODEL), wsel)],
        out_specs=pl.BlockSpec((ts * CHUNKS, LANES), lambda i, te, src: (i, 0)),
        scratch_shapes=[pltpu.VMEM((2, ts * CHUNKS, LANES), F32),
                        pltpu.SemaphoreType.DMA((2,)),
                        pltpu.VMEM((D_MODEL, D_EXPERT), BF16),
                        pltpu.VMEM((D_MODEL, D_EXPERT), BF16),
                        pltpu.VMEM((D_EXPERT, D_MODEL), BF16)],
    )
    return pl.pallas_call(
        kernel,
        grid_spec=grid_spec,
        out_shape=jax.ShapeDtypeStruct((n_tiles * ts * CHUNKS, LANES), F32),
        compiler_params=_cparams(("arbitrary",)),
        name="experts",
    )(te, src, xn3d, w_gate, w_up, w_down)


def _combine_kernel(dest_ref, ys_hbm, h_ref, route_ref, y_ref, gbuf, sem, *, tm, n_steps, tok_off):
    i = pl.program_id(0)
    slot = i % 2
    nrow = 2 * tm

    def start(step, sl):
        base = (tok_off + step * tm) * 2
        _row_gather(ys_hbm, lambda r: dest_ref[base + r], gbuf.at[sl], sem.at[sl], nrow)

    @pl.when(i == 0)
    def _():
        start(0, 0)

    @pl.when(i + 1 < n_steps)
    def _():
        start(i + 1, 1 - slot)

    _row_wait(ys_hbm, gbuf.at[slot], sem.at[slot], nrow)
    buf = gbuf.at[slot]
    route = route_ref[...]
    w1 = route[:, 2:3]
    w2 = route[:, 3:4]
    for c in range(CHUNKS):
        g1 = buf[pl.ds(c, tm, stride=2 * CHUNKS), :]
        g2 = buf[pl.ds(CHUNKS + c, tm, stride=2 * CHUNKS), :]
        sl = slice(c * LANES, (c + 1) * LANES)
        y_ref[:, sl] = h_ref[:, sl] + w1 * g1 + w2 * g2


def _combine(dest_flat, ys3d, h_pool, route, tm, tok_off, n_tok):
    n_steps = n_tok // tm
    boff = tok_off // tm
    kernel = functools.partial(_combine_kernel, tm=tm, n_steps=n_steps, tok_off=tok_off)
    grid_spec = pltpu.PrefetchScalarGridSpec(
        num_scalar_prefetch=1,
        grid=(n_steps,),
        in_specs=[pl.BlockSpec(memory_space=pl.ANY),
                  pl.BlockSpec((tm, D_MODEL), lambda i, d: (i + boff, 0)),
                  pl.BlockSpec((tm, LANES), lambda i, d: (i + boff, 0))],
        out_specs=pl.BlockSpec((tm, D_MODEL), lambda i, d: (i, 0)),
        scratch_shapes=[pltpu.VMEM((2, 2 * tm * CHUNKS, LANES), F32), pltpu.SemaphoreType.DMA((2,))],
    )
    return pl.pallas_call(
        kernel,
        grid_spec=grid_spec,
        out_shape=jax.ShapeDtypeStruct((n_tok, D_MODEL), F32),
        compiler_params=_cparams(("arbitrary",)),
        name="combine",
    )(dest_flat, ys3d, h_pool, route)


def kernel(x_prompt, x_sample, cache_k, cache_v, state_ret, page_table, norm1, w_in, a_q_norm, a_k_norm,
           a_lambda_q1, a_lambda_k1, a_lambda_q2, a_lambda_k2, a_subln, r_norm, w_o, norm2,
           w_group_router, b_group_router, w_expert_router, b_expert_router, w_gate, w_up, w_down):
    depth = norm1.shape[0]
    assert depth == 1, "single-layer step"
    batch, seq, d = x_prompt.shape
    nsamp, tdec, _ = x_sample.shape
    assert d == D_MODEL and tdec == 1
    assert seq % TOKEN_TILE == 0 and seq % ATTN_TILE == 0 and seq % RET_CHUNK == 0
    assert nsamp % PLAN_TILE == 0 and nsamp % SAMPLE_BLOCK == 0
    past = page_table.shape[1] * cache_k.shape[2]
    n_prompt = batch * seq
    n_total = n_prompt + nsamp
    lam_init = 0.8 - 0.6 * math.exp(-0.3 * 0)
    l = 0

    lams = tuple(a[l].reshape(1, A_QK) for a in (a_lambda_q1, a_lambda_k1, a_lambda_q2, a_lambda_k2))
    w_in_bf = w_in[l].astype(BF16)
    w_o_bf = w_o[l].astype(BF16)
    g1 = norm1[l].reshape(1, D_MODEL)
    n2 = norm2[l].reshape(1, D_MODEL)
    qg = jnp.tile(a_q_norm[l], LANES // A_QK).reshape(1, LANES)
    kg = jnp.tile(a_k_norm[l], LANES // A_QK).reshape(1, LANES)
    asub = a_subln[l].reshape(1, A_ROW)
    rnorm = r_norm[l].reshape(1, R_V)
    ones = _segment_ones()
    w_r = jnp.concatenate([w_group_router[l], w_expert_router[l]], axis=1)
    w_r = jnp.pad(w_r, ((0, 0), (0, LANES - w_r.shape[1])))
    wr_hi = w_r.astype(BF16)
    wr_lo = (w_r - wr_hi.astype(F32)).astype(BF16)
    b_r = jnp.concatenate([b_group_router[l], b_expert_router[l]])
    b_r = jnp.pad(b_r, (0, LANES - b_r.shape[0])).reshape(1, LANES)

    pos_s = jnp.full((nsamp,), past, dtype=jnp.int32)
    xs = x_sample.reshape(nsamp, D_MODEL)
    (sq_bf, k_s, _, v_s, _, srq, srk, srv, srg) = _proj(
        xs, g1, w_in_bf, qg, kg, _rope_tables(pos_s), _ret_tables(pos_s), ones, PLAN_TILE)
    a_o_s = _paged_attn(page_table, sq_bf.astype(F32), k_s, v_s, cache_k[l], cache_v[l], lams, lam_init)
    r_o_s, r_state_s = _ret_step(srq, srk, srv, state_ret[l].reshape(nsamp, R_QW, R_V))
    sample_rows = _mix_out(xs, a_o_s, r_o_s, srg, asub, rnorm, w_o_bf, n2, wr_hi, wr_lo, b_r,
                           lam_init, PLAN_TILE)

    pos_p = jnp.tile(jnp.arange(seq), batch)
    xp = x_prompt.reshape(n_prompt, D_MODEL)
    (q_bf, k_p, k_bf, v_p, v_bf, rq, rk, rv, rg) = _proj(
        xp, g1, w_in_bf, qg, kg, _rope_tables(pos_p), _ret_tables(pos_p), ones, TOKEN_TILE)
    a_o = _attn(q_bf, k_bf, v_bf, lams, batch, seq, lam_init)
    r_o, r_state_p = _retention(rq, rk, rv, batch, seq)
    h_pool, xn_pool, route = _mix_out(xp, a_o, r_o, rg, asub, rnorm, w_o_bf, n2, wr_hi, wr_lo, b_r,
                                      lam_init, TOKEN_TILE, tail=sample_rows)
    n_pool = h_pool.shape[0]

    n_assign = 2 * n_total
    n_tiles = -(-(n_assign + N_EXPERTS * (SLOT_TILE - 1)) // SLOT_TILE)
    n_slots = n_tiles * SLOT_TILE
    dest128, te128 = _plan(route, n_total, n_tiles)
    dest_flat = dest128[:, :2].reshape(-1)
    te = te128[:n_tiles, 0]
    src = _invert(dest_flat, n_slots)
    ys = _experts(te, src, xn_pool.reshape(n_pool, CHUNKS, LANES),
                  w_gate[l].reshape(N_EXPERTS, D_MODEL, D_EXPERT),
                  w_up[l].reshape(N_EXPERTS, D_MODEL, D_EXPERT),
                  w_down[l].reshape(N_EXPERTS, D_EXPERT, D_MODEL), n_tiles)
    ys3d = ys.reshape(n_slots, CHUNKS, LANES)
    y_p = _combine(dest_flat, ys3d, h_pool, route, TOKEN_TILE, 0, n_prompt)
    y_s = _combine(dest_flat, ys3d, h_pool, route, PLAN_TILE, n_prompt, nsamp)

    return (y_p.reshape(batch, seq, D_MODEL),
            y_s.reshape(nsamp, 1, D_MODEL),
            k_p.reshape(1, batch, seq, A_HEADS, A_ROW),
            v_p.reshape(1, batch, seq, A_HEADS, A_ROW),
            r_state_p.reshape(1, batch, R_HEADS, R_QK, R_V),
            k_s.reshape(1, nsamp, 1, A_HEADS, A_ROW),
            v_s.reshape(1, nsamp, 1, A_HEADS, A_ROW),
            r_state_s.reshape(1, nsamp, R_HEADS, R_QK, R_V))
```

```python
import functools
import math

import numpy as np
import jax
import jax.numpy as jnp
from jax import lax
from jax.experimental import pallas as pl
from jax.experimental.pallas import tpu as pltpu

F32 = jnp.float32
BF16 = jnp.bfloat16
I32 = jnp.int32

LANES = 128
SUBLANES = 8
CHUNKS = 8

D_MODEL = 1024
A_HEADS = 4
A_QK = 64
A_ROW = 2 * A_QK
A_WIDTH = A_HEADS * A_ROW
ROPE_THETA = 500000.0
ROPE_DIM = A_QK // 4
R_HEADS = 4
R_QK = 64
R_V = 128
R_QW = R_HEADS * R_QK
R_WIDTH = R_HEADS * R_V
R_THETA = 10000.0
N_GROUPS = 4
EXPERTS_PER_GROUP = 8
N_EXPERTS = N_GROUPS * EXPERTS_PER_GROUP
D_EXPERT = 512
EPS = 1e-6
NEG = -1e30

TOKEN_TILE = 256
ATTN_TILE = 512
ONES_ROWS = 16
LOG2E = 1.4426950408889634
RET_CHUNK = 128
PLAN_TILE = 128
PLAN_UNROLL = 3
SLOT_TILE = 256
GATHER_SLOTS = 3
EXPERT_SPLIT = 2
DOWN_SPLIT = 4
SAMPLE_BLOCK = 8
VMEM_LIMIT = 56 * 1024 * 1024


def _cparams(sem, vmem=VMEM_LIMIT):
    return pltpu.CompilerParams(dimension_semantics=sem, vmem_limit_bytes=vmem)


def _rope_tables(pos):
    half = ROPE_DIM // 2
    inv = ROPE_THETA ** (-jnp.arange(half, dtype=F32) / half)
    ang = pos.astype(F32)[:, None] * inv[None, :]
    cos, sin = jnp.cos(ang), jnp.sin(ang)
    n = pos.shape[0]
    ones = jnp.ones((n, A_QK - ROPE_DIM), F32)
    zeros = jnp.zeros((n, A_QK - ROPE_DIM), F32)
    zh = jnp.zeros((n, half), F32)
    c = jnp.concatenate([cos, cos, ones], axis=1)
    s_next = jnp.concatenate([-sin, zh, zeros], axis=1)
    s_prev = jnp.concatenate([zh, sin, zeros], axis=1)
    rep = LANES // A_QK
    return tuple(jnp.tile(t, (1, rep)) for t in (c, s_next, s_prev))


def _ret_tables(pos):
    half = R_QK // 2
    inv = 1.0 / (R_THETA ** jnp.linspace(0.0, 1.0, half, dtype=F32))
    ang = pos.astype(F32)[:, None] * inv[None, :]
    cos, sin = jnp.cos(ang), jnp.sin(ang)
    z = jnp.zeros_like(sin)
    c = jnp.stack([cos, cos], axis=-1).reshape(-1, R_QK)
    s_next = jnp.stack([-sin, z], axis=-1).reshape(-1, R_QK)
    s_prev = jnp.stack([z, sin], axis=-1).reshape(-1, R_QK)
    rep = LANES // R_QK
    return tuple(jnp.tile(t, (1, rep)) for t in (c, s_next, s_prev))


def _segment_ones():
    seg = np.arange(LANES) // A_QK
    return jnp.asarray((seg[:, None] == seg[None, :]).astype(np.float32), dtype=BF16)


def _proj_kernel(x_ref, g1_ref, w_ref, qg_ref, kg_ref, rc_ref, rn_ref, rp_ref,
                 tc_ref, tn_ref, tp_ref, ones_ref,
                 qbf_ref, k_ref, kbf_ref, v_ref, vbf_ref, rq_ref, rk_ref, rv_ref, rg_ref):
    x = x_ref[...]
    ms = jnp.mean(x * x, axis=-1, keepdims=True)
    n = (x * lax.rsqrt(ms + EPS) * g1_ref[...]).astype(BF16)
    h = jnp.dot(n, w_ref[...], preferred_element_type=F32)
    ones = ones_ref[...]
    rc, rn, rp = rc_ref[...], rn_ref[...], rp_ref[...]

    def head_norm_rope(xh, gain):
        sq = xh * xh
        hi = sq.astype(BF16)
        lo = (sq - hi.astype(F32)).astype(BF16)
        ssq = (jnp.dot(hi, ones, preferred_element_type=F32)
               + jnp.dot(lo, ones, preferred_element_type=F32))
        y = xh * lax.rsqrt(ssq * (1.0 / A_QK) + EPS) * gain
        half = ROPE_DIM // 2
        return (y * rc + pltpu.roll(y, LANES - half, 1) * rn + pltpu.roll(y, half, 1) * rp)

    for hd in range(A_HEADS):
        sl = slice(hd * A_ROW, (hd + 1) * A_ROW)
        q = head_norm_rope(h[:, sl], qg_ref[...])
        qbf_ref[:, sl] = (q * (A_QK ** -0.5 * LOG2E)).astype(BF16)
        k = head_norm_rope(h[:, A_WIDTH + hd * A_ROW:A_WIDTH + (hd + 1) * A_ROW], kg_ref[...])
        k_ref[:, sl] = k
        kbf_ref[:, sl] = k.astype(BF16)
    o = 2 * A_WIDTH
    v = h[:, o:o + A_WIDTH]
    v_ref[...] = v
    vbf_ref[...] = v.astype(BF16)
    o += A_WIDTH
    tc, tn, tp = tc_ref[...], tn_ref[...], tp_ref[...]

    def pair_rotate(xs):
        return xs * tc + pltpu.roll(xs, LANES - 1, 1) * tn + pltpu.roll(xs, 1, 1) * tp

    for j in range(R_QW // LANES):
        sl = slice(j * LANES, (j + 1) * LANES)
        rq_ref[:, sl] = pair_rotate(h[:, o + j * LANES:o + (j + 1) * LANES])
        rk_ref[:, sl] = pair_rotate(h[:, o + R_QW + j * LANES:o + R_QW + (j + 1) * LANES]) * (R_QK ** -0.5)
    o += 2 * R_QW
    rv_ref[...] = h[:, o:o + R_WIDTH]
    rg_ref[...] = h[:, o + R_WIDTH:o + 2 * R_WIDTH]


def _proj(x2d, g1, w_bf, qg, kg, rope_t, ret_t, ones, tm):
    t = x2d.shape[0]
    table_tiles = rope_t[0].shape[0] // tm
    row = lambda w: pl.BlockSpec((tm, w), lambda i: (i, 0))
    table = pl.BlockSpec((tm, LANES), lambda i: (i % table_tiles, 0))
    full = lambda a: pl.BlockSpec(a.shape, lambda i: (0,) * a.ndim)
    out_shape = (
        jax.ShapeDtypeStruct((t, A_WIDTH), BF16),
        jax.ShapeDtypeStruct((t, A_WIDTH), F32),
        jax.ShapeDtypeStruct((t, A_WIDTH), BF16),
        jax.ShapeDtypeStruct((t, A_WIDTH), F32),
        jax.ShapeDtypeStruct((t, A_WIDTH), BF16),
        jax.ShapeDtypeStruct((t, R_QW), F32),
        jax.ShapeDtypeStruct((t, R_QW), F32),
        jax.ShapeDtypeStruct((t, R_WIDTH), F32),
        jax.ShapeDtypeStruct((t, R_WIDTH), F32),
    )
    return pl.pallas_call(
        _proj_kernel,
        grid=(t // tm,),
        in_specs=[row(D_MODEL), full(g1), full(w_bf), full(qg), full(kg)]
                 + [table] * 6 + [full(ones)],
        out_specs=tuple(row(s.shape[1]) for s in out_shape),
        out_shape=out_shape,
        compiler_params=_cparams(("parallel",)),
        name="proj",
    )(x2d, g1, w_bf, qg, kg, *rope_t, *ret_t, ones)


def _lambda(lq1_ref, lk1_ref, lq2_ref, lk2_ref, lam_init):
    s1 = jnp.sum(lq1_ref[...] * lk1_ref[...], axis=-1, keepdims=True)
    s2 = jnp.sum(lq2_ref[...] * lk2_ref[...], axis=-1, keepdims=True)
    return jnp.exp(s1) - jnp.exp(s2) + lam_init


def _attn_kernel(q_ref, k_ref, v_ref, lq1_ref, lk1_ref, lq2_ref, lk2_ref, o_ref,
                 m_ref, l_ref, acc_ref, qq_ref, sa_ref, sb_ref, *, tile, lam_init):
    i = pl.program_id(2)
    q = q_ref[...]
    lane = lax.broadcasted_iota(I32, q.shape, 1)
    zero = jnp.zeros_like(q)
    qq = jnp.concatenate([jnp.where(lane < A_QK, q, zero), jnp.where(lane >= A_QK, q, zero)], axis=0)
    qq_ref[...] = qq
    m_ref[...] = jnp.full(m_ref.shape, NEG, F32)
    l_ref[...] = jnp.zeros(l_ref.shape, F32)
    acc_ref[...] = jnp.zeros(acc_ref.shape, F32)

    ones_rows = jnp.ones((ONES_ROWS, tile), BF16)

    def scores(j, s_ref):
        k = k_ref[pl.ds(pl.multiple_of(j * tile, tile), tile), :]
        s_ref[...] = lax.dot_general(k, qq_ref[...], (((1,), (1,)), ((), ())), preferred_element_type=F32)

    def accumulate(j, s_ref, masked):
        v = v_ref[pl.ds(pl.multiple_of(j * tile, tile), tile), :]
        vt = jnp.concatenate([v.T, ones_rows], axis=0)
        s = s_ref[...]
        if masked:
            key = lax.broadcasted_iota(I32, s.shape, 0)
            qry = lax.broadcasted_iota(I32, s.shape, 1)
            qry = jnp.where(qry >= tile, qry - tile, qry)
            s = jnp.where(key <= qry, s, NEG)
        m_prev = m_ref[...]
        m_new = jnp.maximum(m_prev, jnp.max(s, axis=0, keepdims=True))
        alpha = jnp.exp2(m_prev - m_new)
        p = jnp.exp2(s - m_new).astype(BF16)
        pv = jnp.dot(vt, p, preferred_element_type=F32)
        acc_ref[...] = alpha * acc_ref[...] + pv[:A_ROW, :]
        l_ref[...] = alpha * l_ref[...] + pv[A_ROW:A_ROW + 1, :]
        m_ref[...] = m_new

    scores(0, sa_ref)

    def pair(t, carry):
        j = 2 * t
        scores(j + 1, sb_ref)
        accumulate(j, sa_ref, False)
        scores(j + 2, sa_ref)
        accumulate(j + 1, sb_ref, False)
        return carry

    lax.fori_loop(0, i // 2, pair, 0)

    @pl.when(i % 2 == 0)
    def _():
        accumulate(i, sa_ref, True)

    @pl.when(i % 2 == 1)
    def _():
        scores(i, sb_ref)
        accumulate(i - 1, sa_ref, False)
        accumulate(i, sb_ref, True)

    lam = _lambda(lq1_ref, lk1_ref, lq2_ref, lk2_ref, lam_init)
    o1 = acc_ref[:, :tile] / l_ref[:, :tile]
    o2 = acc_ref[:, tile:] / l_ref[:, tile:]
    o_ref[...] = (o1 - lam * o2).T


def _attn(q_bf, k_bf, v_bf, lams, batch, seq, lam_init):
    tile = ATTN_TILE
    nq = seq // tile
    lam_spec = pl.BlockSpec((1, A_QK), lambda b, h, i: (0, 0))
    kernel = functools.partial(_attn_kernel, tile=tile, lam_init=lam_init)
    return pl.pallas_call(
        kernel,
        grid=(batch, A_HEADS, nq),
        in_specs=[pl.BlockSpec((tile, A_ROW), lambda b, h, i: (b * nq + i, h)),
                  pl.BlockSpec((seq, A_ROW), lambda b, h, i: (b, h)),
                  pl.BlockSpec((seq, A_ROW), lambda b, h, i: (b, h))] + [lam_spec] * 4,
        out_specs=pl.BlockSpec((tile, A_ROW), lambda b, h, i: (b * nq + i, h)),
        out_shape=jax.ShapeDtypeStruct((batch * seq, A_WIDTH), F32),
        scratch_shapes=[pltpu.VMEM((1, 2 * tile), F32), pltpu.VMEM((1, 2 * tile), F32),
                        pltpu.VMEM((A_ROW, 2 * tile), F32), pltpu.VMEM((2 * tile, A_ROW), BF16),
                        pltpu.VMEM((tile, 2 * tile), F32), pltpu.VMEM((tile, 2 * tile), F32)],
        compiler_params=_cparams(("parallel", "parallel", "arbitrary")),
        name="attn",
    )(q_bf, k_bf, v_bf, *lams)


def _ret_decay():
    return [math.log(1.0 - 2.0 ** (-5.0 - h)) for h in range(R_HEADS)]


def _ret_tables_chunk(chunk):
    log_g = jnp.log(1.0 - 2.0 ** (-5.0 - jnp.arange(R_HEADS, dtype=F32)))
    idx = jnp.arange(chunk, dtype=F32)
    diff = idx[:, None] - idx[None, :]
    dmask = jnp.where(diff >= 0, jnp.exp(jnp.maximum(diff, 0.0)[None] * log_g[:, None, None]), 0.0)
    q_dec = jnp.exp((idx + 1.0)[:, None] * log_g[None, :])
    k_dec = jnp.exp((chunk - 1.0 - idx)[:, None] * log_g[None, :])
    q_dec = jnp.repeat(q_dec, R_QK, axis=1)
    k_dec = jnp.repeat(k_dec, R_QK, axis=1)
    g_chunk = jnp.exp(chunk * log_g)
    g_rows = jnp.broadcast_to(jnp.repeat(g_chunk, R_QK)[:, None], (R_QW, R_V))
    return dmask, q_dec, k_dec, g_rows


def _ret_kernel(q_ref, k_ref, v_ref, dmask_ref, qdec_ref, kdec_ref, grow_ref,
                o_ref, st_ref, state_ref):
    c = pl.program_id(1)

    @pl.when(c == 0)
    def _():
        state_ref[...] = jnp.zeros(state_ref.shape, F32)

    q = q_ref[...]
    k = k_ref[...]
    qd = (q * qdec_ref[...]).astype(BF16)
    kd = (k * kdec_ref[...]).astype(BF16)
    qb = q.astype(BF16)
    kb = k.astype(BF16)
    vb = v_ref[...].astype(BF16)
    for h in range(R_HEADS):
        ks = slice(h * R_QK, (h + 1) * R_QK)
        vs = slice(h * R_V, (h + 1) * R_V)
        state = state_ref[ks, :]
        inner = lax.dot_general(qb[:, ks], kb[:, ks], (((1,), (1,)), ((), ())),
                                preferred_element_type=F32) * dmask_ref[h]
        o = (jnp.dot(inner.astype(BF16), vb[:, vs], preferred_element_type=F32)
             + jnp.dot(qd[:, ks], state.astype(BF16), preferred_element_type=F32))
        o_ref[:, vs] = o
        upd = lax.dot_general(kd[:, ks], vb[:, vs], (((0,), (0,)), ((), ())),
                              preferred_element_type=F32)
        state_ref[ks, :] = grow_ref[ks, :] * state + upd
    st_ref[...] = state_ref[...]


def _retention(rq, rk, rv, batch, seq):
    chunk = RET_CHUNK
    nc = seq // chunk
    dmask, q_dec, k_dec, g_rows = _ret_tables_chunk(chunk)
    row = lambda w: pl.BlockSpec((chunk, w), lambda b, c: (b * nc + c, 0))
    full = lambda a: pl.BlockSpec(a.shape, lambda b, c: (0,) * a.ndim)
    return pl.pallas_call(
        _ret_kernel,
        grid=(batch, nc),
        in_specs=[row(R_QW), row(R_QW), row(R_WIDTH), full(dmask), full(q_dec), full(k_dec), full(g_rows)],
        out_specs=(row(R_WIDTH), pl.BlockSpec((None, R_QW, R_V), lambda b, c: (b, 0, 0))),
        out_shape=(jax.ShapeDtypeStruct((batch * seq, R_WIDTH), F32),
                   jax.ShapeDtypeStruct((batch, R_QW, R_V), F32)),
        scratch_shapes=[pltpu.VMEM((R_QW, R_V), F32)],
        compiler_params=_cparams(("parallel", "arbitrary")),
        name="retention",
    )(rq, rk, rv, dmask, q_dec, k_dec, g_rows)


def _ret_step_kernel(q_ref, k_ref, v_ref, state_ref, grow_ref, o_ref, ns_ref):
    g = grow_ref[...]
    for t in range(SAMPLE_BLOCK):
        kcol = jnp.broadcast_to(k_ref[t:t + 1, :], (LANES, R_QW)).T
        qcol = jnp.broadcast_to(q_ref[t:t + 1, :], (LANES, R_QW)).T
        vrows = jnp.concatenate(
            [jnp.broadcast_to(v_ref[t:t + 1, h * R_V:(h + 1) * R_V], (R_QK, R_V)) for h in range(R_HEADS)],
            axis=0)
        new = g * state_ref[t] + kcol * vrows
        ns_ref[t] = new
        qn = qcol * new
        for h in range(R_HEADS):
            o_ref[t:t + 1, h * R_V:(h + 1) * R_V] = jnp.sum(qn[h * R_QK:(h + 1) * R_QK, :], axis=0, keepdims=True)


def _ret_step(rq, rk, rv, state):
    n = rq.shape[0]
    bb = SAMPLE_BLOCK
    log_g = jnp.log(1.0 - 2.0 ** (-5.0 - jnp.arange(R_HEADS, dtype=F32)))
    g_rows = jnp.broadcast_to(jnp.repeat(jnp.exp(log_g), R_QK)[:, None], (R_QW, R_V))
    row = lambda w: pl.BlockSpec((bb, w), lambda i: (i, 0))
    st = pl.BlockSpec((bb, R_QW, R_V), lambda i: (i, 0, 0))
    return pl.pallas_call(
        _ret_step_kernel,
        grid=(n // bb,),
        in_specs=[row(R_QW), row(R_QW), row(R_WIDTH), st, pl.BlockSpec((R_QW, R_V), lambda i: (0, 0))],
        out_specs=(row(R_WIDTH), st),
        out_shape=(jax.ShapeDtypeStruct((n, R_WIDTH), F32), jax.ShapeDtypeStruct((n, R_QW, R_V), F32)),
        compiler_params=_cparams(("parallel",)),
        name="ret_step",
    )(rq, rk, rv, state, g_rows)


def _paged_kernel(pt_ref, q_ref, kn_ref, vn_ref, bias_ref, nbias_ref,
                  lq1_ref, lk1_ref, lq2_ref, lk2_ref, *rest, npages, lam_init):
    k_refs = rest[:npages]
    v_refs = rest[npages:2 * npages]
    o_ref = rest[2 * npages]
    q = q_ref[...]
    nrow = 2 * A_HEADS

    def head_rows(x, n):
        row = lax.broadcasted_iota(I32, (n, A_ROW), 0)
        out = jnp.zeros((n, A_ROW), F32)
        for h in range(A_HEADS):
            out = jnp.where(row == h, jnp.broadcast_to(x[:, h * A_ROW:(h + 1) * A_ROW], (n, A_ROW)), out)
        return out

    row8 = lax.broadcasted_iota(I32, (nrow, A_ROW), 0)
    lane8 = lax.broadcasted_iota(I32, (nrow, A_ROW), 1)
    q4 = head_rows(q, nrow)
    q8 = q4 + pltpu.roll(q4, A_HEADS, 0)
    qm = jnp.where((row8 < A_HEADS) == (lane8 < A_QK), q8, 0.0).astype(BF16)

    nt = (((1,), (1,)), ((), ()))
    bias = bias_ref[...]
    s = [lax.dot_general(qm, k_refs[j][...].astype(BF16), nt, preferred_element_type=F32) + bias
         for j in range(npages)]
    kn = head_rows(kn_ref[...], LANES).astype(BF16)
    s.append(lax.dot_general(qm, kn, nt, preferred_element_type=F32) + nbias_ref[...])
    m = functools.reduce(jnp.maximum, [jnp.max(x, axis=-1, keepdims=True) for x in s])
    p = [jnp.exp2(x - m) for x in s]
    l = functools.reduce(lambda a, b: a + b, [jnp.sum(x, axis=-1, keepdims=True) for x in p])
    inv = 1.0 / l
    lam = _lambda(lq1_ref, lk1_ref, lq2_ref, lk2_ref, lam_init)
    vs = [v_refs[j][...].astype(BF16) for j in range(npages)] + [head_rows(vn_ref[...], LANES).astype(BF16)]
    out = jnp.zeros((nrow, A_ROW), F32)
    for pj, vj in zip(p, vs):
        pn = pj * inv
        first = lax.broadcasted_iota(I32, pn.shape, 0) < A_HEADS
        w8 = jnp.where(first, pn - lam * pltpu.roll(pn, A_HEADS, 0), 0.0).astype(BF16)
        out = out + jnp.dot(w8, vj, preferred_element_type=F32)
    for h in range(A_HEADS):
        o_ref[:, h * A_ROW:(h + 1) * A_ROW] = out[h:h + 1, :]


def _paged_attn(page_table, q, k_new, v_new, cache_k, cache_v, lams, lam_init):
    nseq, npages = page_table.shape
    n_phys, page = cache_k.shape[0], cache_k.shape[1]
    prow = page * A_HEADS
    ck = cache_k.reshape(n_phys, prow, A_ROW)
    cv = cache_v.reshape(n_phys, prow, A_ROW)
    r = np.arange(2 * A_HEADS)[:, None] % A_HEADS
    j = np.arange(prow)[None, :]
    bias = jnp.asarray(np.where(j % A_HEADS == r, 0.0, NEG).astype(np.float32))
    jn = np.arange(LANES)[None, :]
    nbias = jnp.asarray(np.where(jn == r, 0.0, NEG).astype(np.float32))
    tok = lambda w: pl.BlockSpec((None, 1, w), lambda b, pt: (b, 0, 0))
    full = lambda a: pl.BlockSpec(a.shape, lambda b, pt: (0,) * a.ndim)

    def page_spec(jj):
        return pl.BlockSpec((None, prow, A_ROW), lambda b, pt: (pt[b * npages + jj], 0, 0))

    kernel = functools.partial(_paged_kernel, npages=npages, lam_init=lam_init)
    grid_spec = pltpu.PrefetchScalarGridSpec(
        num_scalar_prefetch=1,
        grid=(nseq,),
        in_specs=[tok(A_WIDTH), tok(A_WIDTH), tok(A_WIDTH), full(bias), full(nbias)]
                 + [pl.BlockSpec((1, A_QK), lambda b, pt: (0, 0))] * 4
                 + [page_spec(jj) for jj in range(npages)] * 2,
        out_specs=tok(A_WIDTH),
    )
    out = pl.pallas_call(
        kernel,
        grid_spec=grid_spec,
        out_shape=jax.ShapeDtypeStruct((nseq, 1, A_WIDTH), F32),
        compiler_params=_cparams(("arbitrary",)),
        name="paged_attn",
    )(page_table.reshape(-1), q.reshape(nseq, 1, A_WIDTH), k_new.reshape(nseq, 1, A_WIDTH),
      v_new.reshape(nseq, 1, A_WIDTH), bias, nbias, *lams, *([ck] * npages), *([cv] * npages))
    return out.reshape(nseq, A_WIDTH)


def _mix_tile(x_ref, a_ref, r_ref, g_ref, asub_ref, rnorm_ref, wo_ref, n2_ref,
              wrh_ref, wrl_ref, br_ref, h_ref, xn_ref, route_ref, *, lam_init):
    tm = x_ref.shape[0]
    parts = []
    for hd in range(A_HEADS):
        a = a_ref[:, hd * A_ROW:(hd + 1) * A_ROW]
        ms = jnp.mean(a * a, axis=-1, keepdims=True)
        parts.append((a * lax.rsqrt(ms + EPS) * asub_ref[...] * (1.0 - lam_init)).astype(BF16))
    for hd in range(R_HEADS):
        sl = slice(hd * R_V, (hd + 1) * R_V)
        r = r_ref[:, sl]
        ms = jnp.mean(r * r, axis=-1, keepdims=True)
        gate = g_ref[:, sl]
        gate = gate * (1.0 / (1.0 + jnp.exp(-gate)))
        parts.append((r * lax.rsqrt(ms + EPS) * rnorm_ref[...] * gate).astype(BF16))
    merged = jnp.concatenate(parts, axis=1)
    h = x_ref[...] + jnp.dot(merged, wo_ref[...], preferred_element_type=F32)
    h_ref[...] = h
    ms = jnp.mean(h * h, axis=-1, keepdims=True)
    xn = h * lax.rsqrt(ms + EPS) * n2_ref[...]
    for c in range(CHUNKS):
        xn_ref[pl.ds(c, tm, stride=CHUNKS), :] = xn[:, c * LANES:(c + 1) * LANES]
    xh = xn.astype(BF16)
    xl = (xn - xh.astype(F32)).astype(BF16)
    logits = (jnp.dot(xh, wrh_ref[...], preferred_element_type=F32)
              + jnp.dot(xl, wrh_ref[...], preferred_element_type=F32)
              + jnp.dot(xh, wrl_ref[...], preferred_element_type=F32)) + br_ref[...]
    lane = lax.broadcasted_iota(I32, logits.shape, 1)
    big = jnp.int32(LANES)
    gl = jnp.where(lane < N_GROUPS, logits, NEG)
    gmax = jnp.max(gl, axis=-1, keepdims=True)
    gidx = jnp.min(jnp.where(gl == gmax, lane, big), axis=-1, keepdims=True)
    gsum = jnp.sum(jnp.where(lane < N_GROUPS, jnp.exp(gl - gmax), 0.0), axis=-1, keepdims=True)
    gprob = 1.0 / gsum
    lo = N_GROUPS + EXPERTS_PER_GROUP * gidx
    el = jnp.where((lane >= lo) & (lane < lo + EXPERTS_PER_GROUP), logits, NEG)
    v1 = jnp.max(el, axis=-1, keepdims=True)
    i1 = jnp.min(jnp.where(el == v1, lane, big), axis=-1, keepdims=True)
    el2 = jnp.where(lane == i1, NEG, el)
    v2 = jnp.max(el2, axis=-1, keepdims=True)
    i2 = jnp.min(jnp.where(el2 == v2, lane, big), axis=-1, keepdims=True)
    e = jnp.exp(v2 - v1)
    w1 = gprob / (1.0 + e)
    w2 = gprob * e / (1.0 + e)
    e1 = (i1 - N_GROUPS).astype(F32)
    e2 = (i2 - N_GROUPS).astype(F32)
    route_ref[...] = jnp.where(lane == 0, e1, jnp.where(lane == 1, e2, jnp.where(
        lane == 2, w1, jnp.where(lane == 3, w2, 0.0))))


def _mix_kernel(*refs, lam_init, n_tiles, has_tail):
    if not has_tail:
        _mix_tile(*refs, lam_init=lam_init)
        return
    ins, (th_ref, txn_ref, troute_ref), outs = refs[:11], refs[11:14], refs[14:]
    h_ref, xn_ref, route_ref = outs
    i = pl.program_id(0)

    @pl.when(i < n_tiles)
    def _():
        _mix_tile(*ins, *outs, lam_init=lam_init)

    @pl.when(i == n_tiles)
    def _():
        tm = h_ref.shape[0]
        nt = th_ref.shape[0]
        h_ref[:nt, :] = th_ref[...]
        h_ref[nt:, :] = jnp.zeros((tm - nt, D_MODEL), F32)
        xn_ref[:nt * CHUNKS, :] = txn_ref[...]
        xn_ref[nt * CHUNKS:, :] = jnp.zeros(((tm - nt) * CHUNKS, LANES), F32)
        route_ref[:nt, :] = troute_ref[...]
        route_ref[nt:, :] = jnp.zeros((tm - nt, LANES), F32)


def _mix_out(x2d, a_o, r_o, rg, asub, rnorm, wo_bf, n2, wr_hi, wr_lo, br, lam_init, tm, tail=None):
    t = x2d.shape[0]
    n_tiles = t // tm
    has_tail = tail is not None
    n_out = n_tiles + (1 if has_tail else 0)
    row = lambda w: pl.BlockSpec((tm, w), lambda i: (jnp.minimum(i, n_tiles - 1), 0))
    full = lambda a: pl.BlockSpec(a.shape, lambda i: (0,) * a.ndim)
    out_shape = (jax.ShapeDtypeStruct((n_out * tm, D_MODEL), F32),
                 jax.ShapeDtypeStruct((n_out * tm * CHUNKS, LANES), F32),
                 jax.ShapeDtypeStruct((n_out * tm, LANES), F32))
    out_specs = (pl.BlockSpec((tm, D_MODEL), lambda i: (i, 0)),
                 pl.BlockSpec((tm * CHUNKS, LANES), lambda i: (i, 0)),
                 pl.BlockSpec((tm, LANES), lambda i: (i, 0)))
    in_specs = [row(D_MODEL), row(A_WIDTH), row(R_WIDTH), row(R_WIDTH), full(asub), full(rnorm),
                full(wo_bf), full(n2), full(wr_hi), full(wr_lo), full(br)]
    args = [x2d, a_o, r_o, rg, asub, rnorm, wo_bf, n2, wr_hi, wr_lo, br]
    if has_tail:
        assert tail[0].shape[0] <= tm
        in_specs += [full(a) for a in tail]
        args += list(tail)
    kernel = functools.partial(_mix_kernel, lam_init=lam_init, n_tiles=n_tiles, has_tail=has_tail)
    return pl.pallas_call(
        kernel,
        grid=(n_out,),
        in_specs=in_specs,
        out_specs=out_specs,
        out_shape=out_shape,
        compiler_params=_cparams(("arbitrary",)),
        name="mix_out",
    )(*args)


def _plan_kernel(route_ref, ltri_ref, utri_ref, dest_ref, te_ref, meta_ref, *, n, slot_tile):
    tile = PLAN_TILE
    lane = lax.broadcasted_iota(I32, (tile, LANES), 1)

    def block(b):
        r = route_ref[pl.ds(pl.multiple_of(b * tile, tile), tile), :]
        e1 = r[:, 0:1].astype(I32)
        e2 = r[:, 1:2].astype(I32)
        return jnp.where((lane == e1) | (lane == e2), 1.0, 0.0), e1, e2

    def count(b, c):
        return c + jnp.sum(block(b)[0], axis=0, keepdims=True)

    cnt = lax.fori_loop(0, n // tile, count, jnp.zeros((1, LANES), F32), unroll=PLAN_UNROLL)
    ntile = jnp.floor((cnt + (slot_tile - 1)) * (1.0 / slot_tile))
    nt8 = jnp.broadcast_to(ntile, (SUBLANES, LANES)).astype(BF16)
    base_t = jnp.dot(nt8, utri_ref[...], preferred_element_type=F32)[0:1, :]
    base = base_t * slot_tile
    ends = base_t + ntile
    tl = lax.broadcasted_iota(I32, te_ref.shape, 0).astype(F32)
    el = lax.broadcasted_iota(I32, te_ref.shape, 1)
    hit = jnp.where((el < N_EXPERTS) & (ends <= tl), 1.0, 0.0)
    te_ref[...] = jnp.broadcast_to(jnp.sum(hit, axis=-1, keepdims=True), te_ref.shape).astype(I32)
    mrow = lax.broadcasted_iota(I32, meta_ref.shape, 0)
    meta_ref[...] = jnp.where(mrow == 0, cnt, jnp.where(mrow == 1, base, jnp.where(
        mrow == 2, ntile * slot_tile, 0.0))).astype(I32)

    def place(b, run):
        onehot, e1, e2 = block(b)
        rank = jnp.dot(ltri_ref[...], onehot.astype(BF16), preferred_element_type=F32)
        pos = base + run + rank
        d1 = jnp.sum(jnp.where(lane == e1, pos, 0.0), axis=-1, keepdims=True)
        d2 = jnp.sum(jnp.where(lane == e2, pos, 0.0), axis=-1, keepdims=True)
        dest_ref[pl.ds(pl.multiple_of(b * tile, tile), tile), :] = jnp.where(
            lane == 0, d1, jnp.where(lane == 1, d2, 0.0)).astype(I32)
        return run + jnp.sum(onehot, axis=0, keepdims=True)

    lax.fori_loop(0, n // tile, place, jnp.zeros((1, LANES), F32), unroll=PLAN_UNROLL)


def _plan(route, n, n_slot_tiles):
    tile = PLAN_TILE
    te_rows = -(-n_slot_tiles // SUBLANES) * SUBLANES
    ii = np.arange(tile)
    ltri = jnp.asarray((ii[None, :] < ii[:, None]).astype(np.float32), dtype=BF16)
    ee = np.arange(LANES)
    utri = jnp.asarray((ee[:, None] < ee[None, :]).astype(np.float32), dtype=BF16)
    kernel = functools.partial(_plan_kernel, n=n, slot_tile=SLOT_TILE)
    return pl.pallas_call(
        kernel,
        out_shape=(jax.ShapeDtypeStruct((n, LANES), I32),
                   jax.ShapeDtypeStruct((te_rows, LANES), I32),
                   jax.ShapeDtypeStruct((SUBLANES, LANES), I32)),
        compiler_params=pltpu.CompilerParams(vmem_limit_bytes=VMEM_LIMIT),
        name="plan",
    )(route, ltri, utri)


def _invert_kernel(dest_ref, meta_ref, src_ref, *, n_assign, n_slots):
    def fill(start, stop):
        def body(s, c):
            src_ref[s] = 0
            return c

        lax.fori_loop(start, stop, body, 0)

    def per_expert(e, c):
        first = meta_ref[LANES + e]
        fill(first + meta_ref[e], first + meta_ref[2 * LANES + e])
        return c

    lax.fori_loop(0, N_EXPERTS, per_expert, 0)
    last = N_EXPERTS - 1
    fill(meta_ref[LANES + last] + meta_ref[2 * LANES + last], n_slots)

    def scatter(a, c):
        src_ref[dest_ref[a]] = a
        return c

    lax.fori_loop(0, n_assign, scatter, 0, unroll=8)


def _invert(dest_flat, meta_flat, n_slots):
    n_assign = dest_flat.shape[0]
    kernel = functools.partial(_invert_kernel, n_assign=n_assign, n_slots=n_slots)
    return pl.pallas_call(
        kernel,
        in_specs=[pl.BlockSpec(memory_space=pltpu.SMEM), pl.BlockSpec(memory_space=pltpu.SMEM)],
        out_specs=pl.BlockSpec(memory_space=pltpu.SMEM),
        out_shape=jax.ShapeDtypeStruct((n_slots,), I32),
        name="invert",
    )(dest_flat, meta_flat)


def _row_gather(src_hbm, idx_of_row, buf, sem, n_rows):
    def body(r, c):
        pltpu.make_async_copy(src_hbm.at[idx_of_row(r)],
                              buf.at[pl.ds(pl.multiple_of(r * CHUNKS, CHUNKS), CHUNKS), :], sem).start()
        return c

    lax.fori_loop(0, n_rows, body, 0, unroll=8)


def _row_wait(src_hbm, buf, sem, n_rows):
    def body(r, c):
        pltpu.make_async_copy(src_hbm.at[0], buf.at[pl.ds(0, CHUNKS), :], sem).wait()
        return c

    lax.fori_loop(0, n_rows, body, 0, unroll=8)


def _gathered_rows(buf, n_rows):
    return jnp.concatenate([buf[pl.ds(c, n_rows, stride=CHUNKS), :] for c in range(CHUNKS)], axis=1)


def _expert_kernel(te_ref, src_ref, xn_hbm, wg_ref, wu_ref, wd_ref, ys_ref,
                   xbuf, sem, wg_bf, wu_bf, wd_bf, *, n_tiles):
    i = pl.program_id(0)
    ts = SLOT_TILE
    ahead = GATHER_SLOTS - 1

    def is_active(t):
        return (t < n_tiles) & (te_ref[jnp.minimum(t, n_tiles - 1)] < N_EXPERTS)

    def token_of(tile, r):
        return src_ref[tile * ts + r] >> 1

    def start_tile(tile):
        sl = tile % GATHER_SLOTS
        _row_gather(xn_hbm, lambda r: token_of(tile, r), xbuf.at[sl], sem.at[sl], ts)

    def start_rows(tile, r0, n):
        sl = tile % GATHER_SLOTS
        for r in range(r0, r0 + n):
            pltpu.make_async_copy(xn_hbm.at[token_of(tile, r)],
                                  xbuf.at[sl, pl.ds(r * CHUNKS, CHUNKS), :], sem.at[sl]).start()

    @pl.when(i == 0)
    def _():
        for t in range(ahead):
            pl.when(is_active(t))(functools.partial(start_tile, t))

    e = te_ref[i]
    active = e < N_EXPERTS
    prev = te_ref[jnp.maximum(i - 1, 0)]

    @pl.when(active & ((i == 0) | (e != prev)))
    def _():
        wg_bf[...] = wg_ref[...].astype(BF16)
        wu_bf[...] = wu_ref[...].astype(BF16)
        wd_bf[...] = wd_ref[...].astype(BF16)

    slot = i % GATHER_SLOTS

    @pl.when(active)
    def _():
        _row_wait(xn_hbm, xbuf.at[slot], sem.at[slot], ts)

    def compute(prefetch):
        groups = 2 * EXPERT_SPLIT + DOWN_SPLIT
        per = ts // groups
        state = {"g": 0}

        def issue():
            if prefetch:
                start_rows(i + ahead, state["g"] * per, per)
            state["g"] += 1

        x = _gathered_rows(xbuf.at[slot], ts).astype(BF16)
        wcol = D_EXPERT // EXPERT_SPLIT
        a_parts, u_parts = [], []
        for c in range(EXPERT_SPLIT):
            issue()
            a_parts.append(jnp.dot(x, wg_bf[:, c * wcol:(c + 1) * wcol], preferred_element_type=F32))
            issue()
            u_parts.append(jnp.dot(x, wu_bf[:, c * wcol:(c + 1) * wcol], preferred_element_type=F32))
        a = jnp.concatenate(a_parts, axis=1)
        u = jnp.concatenate(u_parts, axis=1)
        hmid = (a * (1.0 / (1.0 + jnp.exp(-a))) * u).astype(BF16)
        ycol = D_MODEL // DOWN_SPLIT
        for c in range(DOWN_SPLIT):
            issue()
            y = jnp.dot(hmid, wd_bf[:, c * ycol:(c + 1) * ycol], preferred_element_type=F32)
            for cc in range(ycol // LANES):
                ys_ref[pl.ds(c * (ycol // LANES) + cc, ts, stride=CHUNKS), :] = y[:, cc * LANES:(cc + 1) * LANES]

    ahead_active = is_active(i + ahead)
    pl.when(active & ahead_active)(functools.partial(compute, True))
    pl.when(active & jnp.logical_not(ahead_active))(functools.partial(compute, False))

    @pl.when(jnp.logical_not(active))
    def _():
        ys_ref[...] = jnp.zeros(ys_ref.shape, F32)


def _experts(te, src, xn3d, w_gate, w_up, w_down, n_tiles):
    ts = SLOT_TILE
    wsel = lambda i, te, src: (jnp.minimum(te[i], N_EXPERTS - 1), 0, 0)
    kernel = functools.partial(_expert_kernel, n_tiles=n_tiles)
    grid_spec = pltpu.PrefetchScalarGridSpec(
        num_scalar_prefetch=2,
        grid=(n_tiles,),
        in_specs=[pl.BlockSpec(memory_space=pl.ANY),
                  pl.BlockSpec((None, D_MODEL, D_EXPERT), wsel),
                  pl.BlockSpec((None, D_MODEL, D_EXPERT), wsel),
                  pl.BlockSpec((None, D_EXPERT, D_MODEL), wsel)],
        out_specs=pl.BlockSpec((ts * CHUNKS, LANES), lambda i, te, src: (i, 0)),
        scratch_shapes=[pltpu.VMEM((GATHER_SLOTS, ts * CHUNKS, LANES), F32),
                        pltpu.SemaphoreType.DMA((GATHER_SLOTS,)),
                        pltpu.VMEM((D_MODEL, D_EXPERT), BF16),
                        pltpu.VMEM((D_MODEL, D_EXPERT), BF16),
                        pltpu.VMEM((D_EXPERT, D_MODEL), BF16)],
    )
    return pl.pallas_call(
        kernel,
        grid_spec=grid_spec,
        out_shape=jax.ShapeDtypeStruct((n_tiles * ts * CHUNKS, LANES), F32),
        compiler_params=_cparams(("arbitrary",)),
        name="experts",
    )(te, src, xn3d, w_gate, w_up, w_down)


def _combine_kernel(dest_ref, ys_hbm, h_ref, route_ref, y_ref, gbuf, sem, *, tm, n_steps, tok_off):
    i = pl.program_id(0)
    slot = i % 2
    nrow = 2 * tm

    def start(step, sl):
        base = (tok_off + step * tm) * 2
        _row_gather(ys_hbm, lambda r: dest_ref[base + r], gbuf.at[sl], sem.at[sl], nrow)

    @pl.when(i == 0)
    def _():
        start(0, 0)

    @pl.when(i + 1 < n_steps)
    def _():
        start(i + 1, 1 - slot)

    _row_wait(ys_hbm, gbuf.at[slot], sem.at[slot], nrow)
    buf = gbuf.at[slot]
    route = route_ref[...]
    w1 = route[:, 2:3]
    w2 = route[:, 3:4]
    for c in range(CHUNKS):
        g1 = buf[pl.ds(c, tm, stride=2 * CHUNKS), :]
        g2 = buf[pl.ds(CHUNKS + c, tm, stride=2 * CHUNKS), :]
        sl = slice(c * LANES, (c + 1) * LANES)
        y_ref[:, sl] = h_ref[:, sl] + w1 * g1 + w2 * g2


def _combine(dest_flat, ys3d, h_pool, route, tm, tok_off, n_tok):
    n_steps = n_tok // tm
    boff = tok_off // tm
    kernel = functools.partial(_combine_kernel, tm=tm, n_steps=n_steps, tok_off=tok_off)
    grid_spec = pltpu.PrefetchScalarGridSpec(
        num_scalar_prefetch=1,
        grid=(n_steps,),
        in_specs=[pl.BlockSpec(memory_space=pl.ANY),
                  pl.BlockSpec((tm, D_MODEL), lambda i, d: (i + boff, 0)),
                  pl.BlockSpec((tm, LANES), lambda i, d: (i + boff, 0))],
        out_specs=pl.BlockSpec((tm, D_MODEL), lambda i, d: (i, 0)),
        scratch_shapes=[pltpu.VMEM((2, 2 * tm * CHUNKS, LANES), F32), pltpu.SemaphoreType.DMA((2,))],
    )
    return pl.pallas_call(
        kernel,
        grid_spec=grid_spec,
        out_shape=jax.ShapeDtypeStruct((n_tok, D_MODEL), F32),
        compiler_params=_cparams(("arbitrary",)),
        name="combine",
    )(dest_flat, ys3d, h_pool, route)


def kernel(x_prompt, x_sample, cache_k, cache_v, state_ret, page_table, norm1, w_in, a_q_norm, a_k_norm,
           a_lambda_q1, a_lambda_k1, a_lambda_q2, a_lambda_k2, a_subln, r_norm, w_o, norm2,
           w_group_router, b_group_router, w_expert_router, b_expert_router, w_gate, w_up, w_down):
    depth = norm1.shape[0]
    assert depth == 1, "single-layer step"
    batch, seq, d = x_prompt.shape
    nsamp, tdec, _ = x_sample.shape
    assert d == D_MODEL and tdec == 1
    assert seq % TOKEN_TILE == 0 and seq % ATTN_TILE == 0 and seq % RET_CHUNK == 0
    assert nsamp % PLAN_TILE == 0 and nsamp % SAMPLE_BLOCK == 0
    past = page_table.shape[1] * cache_k.shape[2]
    n_prompt = batch * seq
    n_total = n_prompt + nsamp
    lam_init = 0.8 - 0.6 * math.exp(-0.3 * 0)
    l = 0

    lams = tuple(a[l].reshape(1, A_QK) for a in (a_lambda_q1, a_lambda_k1, a_lambda_q2, a_lambda_k2))
    w_in_bf = w_in[l].astype(BF16)
    w_o_bf = w_o[l].astype(BF16)
    g1 = norm1[l].reshape(1, D_MODEL)
    n2 = norm2[l].reshape(1, D_MODEL)
    qg = jnp.tile(a_q_norm[l], LANES // A_QK).reshape(1, LANES)
    kg = jnp.tile(a_k_norm[l], LANES // A_QK).reshape(1, LANES)
    asub = a_subln[l].reshape(1, A_ROW)
    rnorm = r_norm[l].reshape(1, R_V)
    ones = _segment_ones()
    w_r = jnp.concatenate([w_group_router[l], w_expert_router[l]], axis=1)
    w_r = jnp.pad(w_r, ((0, 0), (0, LANES - w_r.shape[1])))
    wr_hi = w_r.astype(BF16)
    wr_lo = (w_r - wr_hi.astype(F32)).astype(BF16)
    b_r = jnp.concatenate([b_group_router[l], b_expert_router[l]])
    b_r = jnp.pad(b_r, (0, LANES - b_r.shape[0])).reshape(1, LANES)

    pos_s = jnp.full((nsamp,), past, dtype=jnp.int32)
    xs = x_sample.reshape(nsamp, D_MODEL)
    (sq_bf, k_s, _, v_s, _, srq, srk, srv, srg) = _proj(
        xs, g1, w_in_bf, qg, kg, _rope_tables(pos_s), _ret_tables(pos_s), ones, PLAN_TILE)
    a_o_s = _paged_attn(page_table, sq_bf.astype(F32), k_s, v_s, cache_k[l], cache_v[l], lams, lam_init)
    r_o_s, r_state_s = _ret_step(srq, srk, srv, state_ret[l].reshape(nsamp, R_QW, R_V))
    sample_rows = _mix_out(xs, a_o_s, r_o_s, srg, asub, rnorm, w_o_bf, n2, wr_hi, wr_lo, b_r,
                           lam_init, PLAN_TILE)

    pos_p = jnp.arange(seq)
    xp = x_prompt.reshape(n_prompt, D_MODEL)
    (q_bf, k_p, k_bf, v_p, v_bf, rq, rk, rv, rg) = _proj(
        xp, g1, w_in_bf, qg, kg, _rope_tables(pos_p), _ret_tables(pos_p), ones, TOKEN_TILE)
    a_o = _attn(q_bf, k_bf, v_bf, lams, batch, seq, lam_init)
    r_o, r_state_p = _retention(rq, rk, rv, batch, seq)
    h_pool, xn_pool, route = _mix_out(xp, a_o, r_o, rg, asub, rnorm, w_o_bf, n2, wr_hi, wr_lo, b_r,
                                      lam_init, TOKEN_TILE, tail=sample_rows)
    n_pool = h_pool.shape[0]

    n_assign = 2 * n_total
    n_tiles = -(-(n_assign + N_EXPERTS * (SLOT_TILE - 1)) // SLOT_TILE)
    n_slots = n_tiles * SLOT_TILE
    dest128, te128, meta = _plan(route, n_total, n_tiles)
    dest_flat = dest128[:, :2].reshape(-1)
    te = te128[:n_tiles, 0]
    src = _invert(dest_flat, meta.reshape(-1), n_slots)
    ys = _experts(te, src, xn_pool.reshape(n_pool, CHUNKS, LANES),
                  w_gate[l].reshape(N_EXPERTS, D_MODEL, D_EXPERT),
                  w_up[l].reshape(N_EXPERTS, D_MODEL, D_EXPERT),
                  w_down[l].reshape(N_EXPERTS, D_EXPERT, D_MODEL), n_tiles)
    ys3d = ys.reshape(n_slots, CHUNKS, LANES)
    y_p = _combine(dest_flat, ys3d, h_pool, route, TOKEN_TILE, 0, n_prompt)
    y_s = _combine(dest_flat, ys3d, h_pool, route, PLAN_TILE, n_prompt, nsamp)

    return (y_p.reshape(batch, seq, D_MODEL),
            y_s.reshape(nsamp, 1, D_MODEL),
            k_p.reshape(1, batch, seq, A_HEADS, A_ROW),
            v_p.reshape(1, batch, seq, A_HEADS, A_ROW),
            r_state_p.reshape(1, batch, R_HEADS, R_QK, R_V),
            k_s.reshape(1, nsamp, 1, A_HEADS, A_ROW),
            v_s.reshape(1, nsamp, 1, A_HEADS, A_ROW),
            r_state_s.reshape(1, nsamp, R_HEADS, R_QK, R_V))
```

```python
import functools
import math

import numpy as np
import jax
import jax.numpy as jnp
from jax import lax
from jax.experimental import pallas as pl
from jax.experimental.pallas import tpu as pltpu

F32 = jnp.float32
BF16 = jnp.bfloat16
I32 = jnp.int32

LANES = 128
SUBLANES = 8
CHUNKS = 8

D_MODEL = 1024
A_HEADS = 4
A_QK = 64
A_ROW = 2 * A_QK
A_WIDTH = A_HEADS * A_ROW
ROPE_THETA = 500000.0
ROPE_DIM = A_QK // 4
R_HEADS = 4
R_QK = 64
R_V = 128
R_QW = R_HEADS * R_QK
R_WIDTH = R_HEADS * R_V
R_THETA = 10000.0
N_GROUPS = 4
EXPERTS_PER_GROUP = 8
N_EXPERTS = N_GROUPS * EXPERTS_PER_GROUP
D_EXPERT = 512
EPS = 1e-6
NEG = -1e30

TOKEN_TILE = 256
ATTN_TILE = 512
ONES_ROWS = 16
LOG2E = 1.4426950408889634
RET_CHUNK = 256
PLAN_TILE = 128
PLAN_UNROLL = 3
SLOT_TILE = 256
GATHER_SLOTS = 3
EXPERT_SPLIT = 2
DOWN_SPLIT = 4
SAMPLE_BLOCK = 8
VMEM_LIMIT = 56 * 1024 * 1024


def _cparams(sem, vmem=VMEM_LIMIT):
    return pltpu.CompilerParams(dimension_semantics=sem, vmem_limit_bytes=vmem)


def _rope_tables(pos):
    half = ROPE_DIM // 2
    inv = ROPE_THETA ** (-jnp.arange(half, dtype=F32) / half)
    ang = pos.astype(F32)[:, None] * inv[None, :]
    cos, sin = jnp.cos(ang), jnp.sin(ang)
    n = pos.shape[0]
    ones = jnp.ones((n, A_QK - ROPE_DIM), F32)
    zeros = jnp.zeros((n, A_QK - ROPE_DIM), F32)
    zh = jnp.zeros((n, half), F32)
    c = jnp.concatenate([cos, cos, ones], axis=1)
    s_next = jnp.concatenate([-sin, zh, zeros], axis=1)
    s_prev = jnp.concatenate([zh, sin, zeros], axis=1)
    rep = LANES // A_QK
    return tuple(jnp.tile(t, (1, rep)) for t in (c, s_next, s_prev))


def _ret_tables(pos):
    half = R_QK // 2
    inv = 1.0 / (R_THETA ** jnp.linspace(0.0, 1.0, half, dtype=F32))
    ang = pos.astype(F32)[:, None] * inv[None, :]
    cos, sin = jnp.cos(ang), jnp.sin(ang)
    z = jnp.zeros_like(sin)
    c = jnp.stack([cos, cos], axis=-1).reshape(-1, R_QK)
    s_next = jnp.stack([-sin, z], axis=-1).reshape(-1, R_QK)
    s_prev = jnp.stack([z, sin], axis=-1).reshape(-1, R_QK)
    rep = LANES // R_QK
    return tuple(jnp.tile(t, (1, rep)) for t in (c, s_next, s_prev))


def _segment_ones():
    seg = np.arange(LANES) // A_QK
    return jnp.asarray((seg[:, None] == seg[None, :]).astype(np.float32), dtype=BF16)


def _proj_kernel(x_ref, g1_ref, w_ref, qg_ref, kg_ref, rc_ref, rn_ref, rp_ref,
                 tc_ref, tn_ref, tp_ref, ones_ref,
                 qbf_ref, k_ref, kbf_ref, v_ref, vbf_ref, rq_ref, rk_ref, rv_ref, rg_ref):
    tm = x_ref.shape[0]
    x = x_ref[...]
    ms = jnp.mean(x * x, axis=-1, keepdims=True)
    n = (x * lax.rsqrt(ms + EPS) * g1_ref[...]).astype(BF16)
    h = jnp.dot(n, w_ref[...], preferred_element_type=F32)
    ones = ones_ref[...]
    rc, rn, rp = rc_ref[...], rn_ref[...], rp_ref[...]

    def head_norm_rope(xh, gain):
        sq = xh * xh
        hi = sq.astype(BF16)
        lo = (sq - hi.astype(F32)).astype(BF16)
        ssq = (jnp.dot(hi, ones, preferred_element_type=F32)
               + jnp.dot(lo, ones, preferred_element_type=F32))
        y = xh * lax.rsqrt(ssq * (1.0 / A_QK) + EPS) * gain
        half = ROPE_DIM // 2
        return (y * rc + pltpu.roll(y, LANES - half, 1) * rn + pltpu.roll(y, half, 1) * rp)

    for hd in range(A_HEADS):
        sl = slice(hd * A_ROW, (hd + 1) * A_ROW)
        q = head_norm_rope(h[:, sl], qg_ref[...])
        qbf_ref[:, sl] = (q * (A_QK ** -0.5 * LOG2E)).astype(BF16)
        k = head_norm_rope(h[:, A_WIDTH + hd * A_ROW:A_WIDTH + (hd + 1) * A_ROW], kg_ref[...])
        k_ref[pl.ds(hd, tm, stride=A_HEADS), :] = k
        kbf_ref[:, sl] = k.astype(BF16)
    o = 2 * A_WIDTH
    v = h[:, o:o + A_WIDTH]
    for hd in range(A_HEADS):
        v_ref[pl.ds(hd, tm, stride=A_HEADS), :] = v[:, hd * A_ROW:(hd + 1) * A_ROW]
    vbf_ref[...] = v.astype(BF16)
    o += A_WIDTH
    tc, tn, tp = tc_ref[...], tn_ref[...], tp_ref[...]

    def pair_rotate(xs):
        return xs * tc + pltpu.roll(xs, LANES - 1, 1) * tn + pltpu.roll(xs, 1, 1) * tp

    for j in range(R_QW // LANES):
        sl = slice(j * LANES, (j + 1) * LANES)
        rq_ref[:, sl] = pair_rotate(h[:, o + j * LANES:o + (j + 1) * LANES])
        rk_ref[:, sl] = pair_rotate(h[:, o + R_QW + j * LANES:o + R_QW + (j + 1) * LANES]) * (R_QK ** -0.5)
    o += 2 * R_QW
    rv_ref[...] = h[:, o:o + R_WIDTH]
    rg_ref[...] = h[:, o + R_WIDTH:o + 2 * R_WIDTH]


def _proj(x2d, g1, w_bf, qg, kg, rope_t, ret_t, ones, tm):
    t = x2d.shape[0]
    table_tiles = rope_t[0].shape[0] // tm
    row = lambda w: pl.BlockSpec((tm, w), lambda i: (i, 0))
    table = pl.BlockSpec((tm, LANES), lambda i: (i % table_tiles, 0))
    full = lambda a: pl.BlockSpec(a.shape, lambda i: (0,) * a.ndim)
    out_shape = (
        jax.ShapeDtypeStruct((t, A_WIDTH), BF16),
        jax.ShapeDtypeStruct((t * A_HEADS, A_ROW), F32),
        jax.ShapeDtypeStruct((t, A_WIDTH), BF16),
        jax.ShapeDtypeStruct((t * A_HEADS, A_ROW), F32),
        jax.ShapeDtypeStruct((t, A_WIDTH), BF16),
        jax.ShapeDtypeStruct((t, R_QW), F32),
        jax.ShapeDtypeStruct((t, R_QW), F32),
        jax.ShapeDtypeStruct((t, R_WIDTH), F32),
        jax.ShapeDtypeStruct((t, R_WIDTH), F32),
    )
    return pl.pallas_call(
        _proj_kernel,
        grid=(t // tm,),
        in_specs=[row(D_MODEL), full(g1), full(w_bf), full(qg), full(kg)]
                 + [table] * 6 + [full(ones)],
        out_specs=tuple(pl.BlockSpec((tm * s.shape[0] // t, s.shape[1]), lambda i: (i, 0)) for s in out_shape),
        out_shape=out_shape,
        compiler_params=_cparams(("parallel",)),
        name="proj",
    )(x2d, g1, w_bf, qg, kg, *rope_t, *ret_t, ones)


def _lambda(lq1_ref, lk1_ref, lq2_ref, lk2_ref, lam_init):
    s1 = jnp.sum(lq1_ref[...] * lk1_ref[...], axis=-1, keepdims=True)
    s2 = jnp.sum(lq2_ref[...] * lk2_ref[...], axis=-1, keepdims=True)
    return jnp.exp(s1) - jnp.exp(s2) + lam_init


def _attn_kernel(q_ref, k_ref, v_ref, lq1_ref, lk1_ref, lq2_ref, lk2_ref, o_ref,
                 m_ref, l_ref, acc_ref, qq_ref, sa_ref, sb_ref, *, tile, lam_init):
    i = pl.program_id(2)
    q = q_ref[...]
    lane = lax.broadcasted_iota(I32, q.shape, 1)
    zero = jnp.zeros_like(q)
    qq = jnp.concatenate([jnp.where(lane < A_QK, q, zero), jnp.where(lane >= A_QK, q, zero)], axis=0)
    qq_ref[...] = qq
    m_ref[...] = jnp.full(m_ref.shape, NEG, F32)
    l_ref[...] = jnp.zeros(l_ref.shape, F32)
    acc_ref[...] = jnp.zeros(acc_ref.shape, F32)

    ones_rows = jnp.ones((ONES_ROWS, tile), BF16)

    def scores(j, s_ref):
        k = k_ref[pl.ds(pl.multiple_of(j * tile, tile), tile), :]
        s_ref[...] = lax.dot_general(k, qq_ref[...], (((1,), (1,)), ((), ())), preferred_element_type=F32)

    def accumulate(j, s_ref, masked):
        v = v_ref[pl.ds(pl.multiple_of(j * tile, tile), tile), :]
        vt = jnp.concatenate([v.T, ones_rows], axis=0)
        s = s_ref[...]
        if masked:
            key = lax.broadcasted_iota(I32, s.shape, 0)
            qry = lax.broadcasted_iota(I32, s.shape, 1)
            qry = jnp.where(qry >= tile, qry - tile, qry)
            s = jnp.where(key <= qry, s, NEG)
        m_prev = m_ref[...]
        m_new = jnp.maximum(m_prev, jnp.max(s, axis=0, keepdims=True))
        alpha = jnp.exp2(m_prev - m_new)
        p = jnp.exp2(s - m_new).astype(BF16)
        pv = jnp.dot(vt, p, preferred_element_type=F32)
        acc_ref[...] = alpha * acc_ref[...] + pv[:A_ROW, :]
        l_ref[...] = alpha * l_ref[...] + pv[A_ROW:A_ROW + 1, :]
        m_ref[...] = m_new

    scores(0, sa_ref)

    def pair(t, carry):
        j = 2 * t
        scores(j + 1, sb_ref)
        accumulate(j, sa_ref, False)
        scores(j + 2, sa_ref)
        accumulate(j + 1, sb_ref, False)
        return carry

    lax.fori_loop(0, i // 2, pair, 0)

    @pl.when(i % 2 == 0)
    def _():
        accumulate(i, sa_ref, True)

    @pl.when(i % 2 == 1)
    def _():
        scores(i, sb_ref)
        accumulate(i - 1, sa_ref, False)
        accumulate(i, sb_ref, True)

    lam = _lambda(lq1_ref, lk1_ref, lq2_ref, lk2_ref, lam_init)
    o1 = acc_ref[:, :tile] / l_ref[:, :tile]
    o2 = acc_ref[:, tile:] / l_ref[:, tile:]
    o_ref[...] = (o1 - lam * o2).T


def _attn(q_bf, k_bf, v_bf, lams, batch, seq, lam_init):
    tile = ATTN_TILE
    nq = seq // tile
    lam_spec = pl.BlockSpec((1, A_QK), lambda b, h, i: (0, 0))
    kernel = functools.partial(_attn_kernel, tile=tile, lam_init=lam_init)
    return pl.pallas_call(
        kernel,
        grid=(batch, A_HEADS, nq),
        in_specs=[pl.BlockSpec((tile, A_ROW), lambda b, h, i: (b * nq + i, h)),
                  pl.BlockSpec((seq, A_ROW), lambda b, h, i: (b, h)),
                  pl.BlockSpec((seq, A_ROW), lambda b, h, i: (b, h))] + [lam_spec] * 4,
        out_specs=pl.BlockSpec((tile, A_ROW), lambda b, h, i: (b * nq + i, h)),
        out_shape=jax.ShapeDtypeStruct((batch * seq, A_WIDTH), F32),
        scratch_shapes=[pltpu.VMEM((1, 2 * tile), F32), pltpu.VMEM((1, 2 * tile), F32),
                        pltpu.VMEM((A_ROW, 2 * tile), F32), pltpu.VMEM((2 * tile, A_ROW), BF16),
                        pltpu.VMEM((tile, 2 * tile), F32), pltpu.VMEM((tile, 2 * tile), F32)],
        compiler_params=_cparams(("parallel", "parallel", "arbitrary")),
        name="attn",
    )(q_bf, k_bf, v_bf, *lams)


def _ret_decay():
    return [math.log(1.0 - 2.0 ** (-5.0 - h)) for h in range(R_HEADS)]


def _ret_tables_chunk(chunk):
    log_g = jnp.log(1.0 - 2.0 ** (-5.0 - jnp.arange(R_HEADS, dtype=F32)))
    idx = jnp.arange(chunk, dtype=F32)
    diff = idx[:, None] - idx[None, :]
    dmask = jnp.where(diff >= 0, jnp.exp(jnp.maximum(diff, 0.0)[None] * log_g[:, None, None]), 0.0)
    q_dec = jnp.exp((idx + 1.0)[:, None] * log_g[None, :])
    k_dec = jnp.exp((chunk - 1.0 - idx)[:, None] * log_g[None, :])
    q_dec = jnp.repeat(q_dec, R_QK, axis=1)
    k_dec = jnp.repeat(k_dec, R_QK, axis=1)
    g_chunk = jnp.exp(chunk * log_g)
    g_rows = jnp.broadcast_to(jnp.repeat(g_chunk, R_QK)[:, None], (R_QW, R_V))
    return dmask, q_dec, k_dec, g_rows


def _ret_kernel(q_ref, k_ref, v_ref, dmask_ref, qdec_ref, kdec_ref, grow_ref,
                o_ref, st_ref, state_ref):
    c = pl.program_id(1)

    @pl.when(c == 0)
    def _():
        state_ref[...] = jnp.zeros(state_ref.shape, F32)

    q = q_ref[...]
    k = k_ref[...]
    qd = (q * qdec_ref[...]).astype(BF16)
    kd = (k * kdec_ref[...]).astype(BF16)
    qb = q.astype(BF16)
    kb = k.astype(BF16)
    vb = v_ref[...].astype(BF16)
    for h in range(R_HEADS):
        ks = slice(h * R_QK, (h + 1) * R_QK)
        vs = slice(h * R_V, (h + 1) * R_V)
        state = state_ref[ks, :]
        inner = lax.dot_general(qb[:, ks], kb[:, ks], (((1,), (1,)), ((), ())),
                                preferred_element_type=F32) * dmask_ref[h]
        o = (jnp.dot(inner.astype(BF16), vb[:, vs], preferred_element_type=F32)
             + jnp.dot(qd[:, ks], state.astype(BF16), preferred_element_type=F32))
        o_ref[:, vs] = o
        upd = lax.dot_general(kd[:, ks], vb[:, vs], (((0,), (0,)), ((), ())),
                              preferred_element_type=F32)
        state_ref[ks, :] = grow_ref[ks, :] * state + upd
    st_ref[...] = state_ref[...]


def _retention(rq, rk, rv, batch, seq):
    chunk = RET_CHUNK
    nc = seq // chunk
    dmask, q_dec, k_dec, g_rows = _ret_tables_chunk(chunk)
    row = lambda w: pl.BlockSpec((chunk, w), lambda b, c: (b * nc + c, 0))
    full = lambda a: pl.BlockSpec(a.shape, lambda b, c: (0,) * a.ndim)
    return pl.pallas_call(
        _ret_kernel,
        grid=(batch, nc),
        in_specs=[row(R_QW), row(R_QW), row(R_WIDTH), full(dmask), full(q_dec), full(k_dec), full(g_rows)],
        out_specs=(row(R_WIDTH), pl.BlockSpec((None, R_QW, R_V), lambda b, c: (b, 0, 0))),
        out_shape=(jax.ShapeDtypeStruct((batch * seq, R_WIDTH), F32),
                   jax.ShapeDtypeStruct((batch, R_QW, R_V), F32)),
        scratch_shapes=[pltpu.VMEM((R_QW, R_V), F32)],
        compiler_params=_cparams(("parallel", "arbitrary")),
        name="retention",
    )(rq, rk, rv, dmask, q_dec, k_dec, g_rows)


def _ret_step_kernel(q_ref, k_ref, v_ref, state_ref, grow_ref, o_ref, ns_ref):
    g = grow_ref[...]
    for t in range(SAMPLE_BLOCK):
        kcol = jnp.broadcast_to(k_ref[t:t + 1, :], (LANES, R_QW)).T
        qcol = jnp.broadcast_to(q_ref[t:t + 1, :], (LANES, R_QW)).T
        vrows = jnp.concatenate(
            [jnp.broadcast_to(v_ref[t:t + 1, h * R_V:(h + 1) * R_V], (R_QK, R_V)) for h in range(R_HEADS)],
            axis=0)
        new = g * state_ref[t] + kcol * vrows
        ns_ref[t] = new
        qn = qcol * new
        for h in range(R_HEADS):
            o_ref[t:t + 1, h * R_V:(h + 1) * R_V] = jnp.sum(qn[h * R_QK:(h + 1) * R_QK, :], axis=0, keepdims=True)


def _ret_step(rq, rk, rv, state):
    n = rq.shape[0]
    bb = SAMPLE_BLOCK
    log_g = jnp.log(1.0 - 2.0 ** (-5.0 - jnp.arange(R_HEADS, dtype=F32)))
    g_rows = jnp.broadcast_to(jnp.repeat(jnp.exp(log_g), R_QK)[:, None], (R_QW, R_V))
    row = lambda w: pl.BlockSpec((bb, w), lambda i: (i, 0))
    st = pl.BlockSpec((bb, R_QW, R_V), lambda i: (i, 0, 0))
    return pl.pallas_call(
        _ret_step_kernel,
        grid=(n // bb,),
        in_specs=[row(R_QW), row(R_QW), row(R_WIDTH), st, pl.BlockSpec((R_QW, R_V), lambda i: (0, 0))],
        out_specs=(row(R_WIDTH), st),
        out_shape=(jax.ShapeDtypeStruct((n, R_WIDTH), F32), jax.ShapeDtypeStruct((n, R_QW, R_V), F32)),
        compiler_params=_cparams(("parallel",)),
        name="ret_step",
    )(rq, rk, rv, state, g_rows)


def _paged_kernel(pt_ref, q_ref, kn_ref, vn_ref, bias_ref, nbias_ref,
                  lq1_ref, lk1_ref, lq2_ref, lk2_ref, *rest, npages, lam_init):
    k_refs = rest[:npages]
    v_refs = rest[npages:2 * npages]
    o_ref = rest[2 * npages]
    q = q_ref[...]
    nrow = 2 * A_HEADS

    def head_rows(x, n):
        row = lax.broadcasted_iota(I32, (n, A_ROW), 0)
        out = jnp.zeros((n, A_ROW), F32)
        for h in range(A_HEADS):
            out = jnp.where(row == h, jnp.broadcast_to(x[:, h * A_ROW:(h + 1) * A_ROW], (n, A_ROW)), out)
        return out

    row8 = lax.broadcasted_iota(I32, (nrow, A_ROW), 0)
    lane8 = lax.broadcasted_iota(I32, (nrow, A_ROW), 1)
    q4 = head_rows(q, nrow)
    q8 = q4 + pltpu.roll(q4, A_HEADS, 0)
    qm = jnp.where((row8 < A_HEADS) == (lane8 < A_QK), q8, 0.0).astype(BF16)

    nt = (((1,), (1,)), ((), ()))
    bias = bias_ref[...]
    s = [lax.dot_general(qm, k_refs[j][...].astype(BF16), nt, preferred_element_type=F32) + bias
         for j in range(npages)]
    kn = head_rows(kn_ref[...], LANES).astype(BF16)
    s.append(lax.dot_general(qm, kn, nt, preferred_element_type=F32) + nbias_ref[...])
    m = functools.reduce(jnp.maximum, [jnp.max(x, axis=-1, keepdims=True) for x in s])
    p = [jnp.exp2(x - m) for x in s]
    l = functools.reduce(lambda a, b: a + b, [jnp.sum(x, axis=-1, keepdims=True) for x in p])
    inv = 1.0 / l
    lam = _lambda(lq1_ref, lk1_ref, lq2_ref, lk2_ref, lam_init)
    vs = [v_refs[j][...].astype(BF16) for j in range(npages)] + [head_rows(vn_ref[...], LANES).astype(BF16)]
    out = jnp.zeros((nrow, A_ROW), F32)
    for pj, vj in zip(p, vs):
        pn = pj * inv
        first = lax.broadcasted_iota(I32, pn.shape, 0) < A_HEADS
        w8 = jnp.where(first, pn - lam * pltpu.roll(pn, A_HEADS, 0), 0.0).astype(BF16)
        out = out + jnp.dot(w8, vj, preferred_element_type=F32)
    for h in range(A_HEADS):
        o_ref[:, h * A_ROW:(h + 1) * A_ROW] = out[h:h + 1, :]


def _paged_attn(page_table, q, k_new, v_new, cache_k, cache_v, lams, lam_init):
    nseq, npages = page_table.shape
    n_phys, page = cache_k.shape[0], cache_k.shape[1]
    prow = page * A_HEADS
    ck = cache_k.reshape(n_phys, prow, A_ROW)
    cv = cache_v.reshape(n_phys, prow, A_ROW)
    r = np.arange(2 * A_HEADS)[:, None] % A_HEADS
    j = np.arange(prow)[None, :]
    bias = jnp.asarray(np.where(j % A_HEADS == r, 0.0, NEG).astype(np.float32))
    jn = np.arange(LANES)[None, :]
    nbias = jnp.asarray(np.where(jn == r, 0.0, NEG).astype(np.float32))
    tok = lambda w: pl.BlockSpec((None, 1, w), lambda b, pt: (b, 0, 0))
    full = lambda a: pl.BlockSpec(a.shape, lambda b, pt: (0,) * a.ndim)

    def page_spec(jj):
        return pl.BlockSpec((None, prow, A_ROW), lambda b, pt: (pt[b * npages + jj], 0, 0))

    kernel = functools.partial(_paged_kernel, npages=npages, lam_init=lam_init)
    grid_spec = pltpu.PrefetchScalarGridSpec(
        num_scalar_prefetch=1,
        grid=(nseq,),
        in_specs=[tok(A_WIDTH), tok(A_WIDTH), tok(A_WIDTH), full(bias), full(nbias)]
                 + [pl.BlockSpec((1, A_QK), lambda b, pt: (0, 0))] * 4
                 + [page_spec(jj) for jj in range(npages)] * 2,
        out_specs=tok(A_WIDTH),
    )
    out = pl.pallas_call(
        kernel,
        grid_spec=grid_spec,
        out_shape=jax.ShapeDtypeStruct((nseq, 1, A_WIDTH), F32),
        compiler_params=_cparams(("arbitrary",)),
        name="paged_attn",
    )(page_table.reshape(-1), q.reshape(nseq, 1, A_WIDTH), k_new.reshape(nseq, 1, A_WIDTH),
      v_new.reshape(nseq, 1, A_WIDTH), bias, nbias, *lams, *([ck] * npages), *([cv] * npages))
    return out.reshape(nseq, A_WIDTH)


def _mix_tile(x_ref, a_ref, r_ref, g_ref, asub_ref, rnorm_ref, wo_ref, n2_ref,
              wrh_ref, wrl_ref, br_ref, h_ref, xn_ref, route_ref, *, lam_init):
    tm = x_ref.shape[0]
    parts = []
    for hd in range(A_HEADS):
        a = a_ref[:, hd * A_ROW:(hd + 1) * A_ROW]
        ms = jnp.mean(a * a, axis=-1, keepdims=True)
        parts.append((a * lax.rsqrt(ms + EPS) * asub_ref[...] * (1.0 - lam_init)).astype(BF16))
    for hd in range(R_HEADS):
        sl = slice(hd * R_V, (hd + 1) * R_V)
        r = r_ref[:, sl]
        ms = jnp.mean(r * r, axis=-1, keepdims=True)
        gate = g_ref[:, sl]
        gate = gate * (1.0 / (1.0 + jnp.exp(-gate)))
        parts.append((r * lax.rsqrt(ms + EPS) * rnorm_ref[...] * gate).astype(BF16))
    merged = jnp.concatenate(parts, axis=1)
    h = x_ref[...] + jnp.dot(merged, wo_ref[...], preferred_element_type=F32)
    h_ref[...] = h
    ms = jnp.mean(h * h, axis=-1, keepdims=True)
    xn = h * lax.rsqrt(ms + EPS) * n2_ref[...]
    for c in range(CHUNKS):
        xn_ref[pl.ds(c, tm, stride=CHUNKS), :] = xn[:, c * LANES:(c + 1) * LANES]
    xh = xn.astype(BF16)
    xl = (xn - xh.astype(F32)).astype(BF16)
    logits = (jnp.dot(xh, wrh_ref[...], preferred_element_type=F32)
              + jnp.dot(xl, wrh_ref[...], preferred_element_type=F32)
              + jnp.dot(xh, wrl_ref[...], preferred_element_type=F32)) + br_ref[...]
    lane = lax.broadcasted_iota(I32, logits.shape, 1).astype(F32)
    big = float(LANES)
    gl = jnp.where(lane < N_GROUPS, logits, NEG)
    gmax = jnp.max(gl, axis=-1, keepdims=True)
    gidx = jnp.min(jnp.where(gl == gmax, lane, big), axis=-1, keepdims=True)
    gsum = jnp.sum(jnp.where(lane < N_GROUPS, jnp.exp(gl - gmax), 0.0), axis=-1, keepdims=True)
    gprob = 1.0 / gsum
    lo = N_GROUPS + EXPERTS_PER_GROUP * gidx
    el = jnp.where((lane >= lo) & (lane < lo + EXPERTS_PER_GROUP), logits, NEG)
    v1 = jnp.max(el, axis=-1, keepdims=True)
    i1 = jnp.min(jnp.where(el == v1, lane, big), axis=-1, keepdims=True)
    el2 = jnp.where(lane == i1, NEG, el)
    v2 = jnp.max(el2, axis=-1, keepdims=True)
    i2 = jnp.min(jnp.where(el2 == v2, lane, big), axis=-1, keepdims=True)
    e = jnp.exp(v2 - v1)
    w1 = gprob / (1.0 + e)
    w2 = gprob * e / (1.0 + e)
    e1 = i1 - N_GROUPS
    e2 = i2 - N_GROUPS
    route_ref[...] = jnp.where(lane == 0, e1, jnp.where(lane == 1, e2, jnp.where(
        lane == 2, w1, jnp.where(lane == 3, w2, 0.0))))


def _mix_kernel(*refs, lam_init, n_tiles, has_tail):
    if not has_tail:
        _mix_tile(*refs, lam_init=lam_init)
        return
    ins, (th_ref, txn_ref, troute_ref), outs = refs[:11], refs[11:14], refs[14:]
    h_ref, xn_ref, route_ref = outs
    i = pl.program_id(0)

    @pl.when(i < n_tiles)
    def _():
        _mix_tile(*ins, *outs, lam_init=lam_init)

    @pl.when(i == n_tiles)
    def _():
        tm = h_ref.shape[0]
        nt = th_ref.shape[0]
        h_ref[:nt, :] = th_ref[...]
        h_ref[nt:, :] = jnp.zeros((tm - nt, D_MODEL), F32)
        xn_ref[:nt * CHUNKS, :] = txn_ref[...]
        xn_ref[nt * CHUNKS:, :] = jnp.zeros(((tm - nt) * CHUNKS, LANES), F32)
        route_ref[:nt, :] = troute_ref[...]
        route_ref[nt:, :] = jnp.zeros((tm - nt, LANES), F32)


def _mix_out(x2d, a_o, r_o, rg, asub, rnorm, wo_bf, n2, wr_hi, wr_lo, br, lam_init, tm, tail=None):
    t = x2d.shape[0]
    n_tiles = t // tm
    has_tail = tail is not None
    n_out = n_tiles + (1 if has_tail else 0)
    row = lambda w: pl.BlockSpec((tm, w), lambda i: (jnp.minimum(i, n_tiles - 1), 0))
    full = lambda a: pl.BlockSpec(a.shape, lambda i: (0,) * a.ndim)
    out_shape = (jax.ShapeDtypeStruct((n_out * tm, D_MODEL), F32),
                 jax.ShapeDtypeStruct((n_out * tm * CHUNKS, LANES), F32),
                 jax.ShapeDtypeStruct((n_out * tm, LANES), F32))
    out_specs = (pl.BlockSpec((tm, D_MODEL), lambda i: (i, 0)),
                 pl.BlockSpec((tm * CHUNKS, LANES), lambda i: (i, 0)),
                 pl.BlockSpec((tm, LANES), lambda i: (i, 0)))
    in_specs = [row(D_MODEL), row(A_WIDTH), row(R_WIDTH), row(R_WIDTH), full(asub), full(rnorm),
                full(wo_bf), full(n2), full(wr_hi), full(wr_lo), full(br)]
    args = [x2d, a_o, r_o, rg, asub, rnorm, wo_bf, n2, wr_hi, wr_lo, br]
    if has_tail:
        assert tail[0].shape[0] <= tm
        in_specs += [full(a) for a in tail]
        args += list(tail)
    kernel = functools.partial(_mix_kernel, lam_init=lam_init, n_tiles=n_tiles, has_tail=has_tail)
    return pl.pallas_call(
        kernel,
        grid=(n_out,),
        in_specs=in_specs,
        out_specs=out_specs,
        out_shape=out_shape,
        compiler_params=_cparams(("arbitrary",)),
        name="mix_out",
    )(*args)


def _plan_kernel(route_ref, ltri_ref, utri_ref, dest_ref, te_ref, meta_ref, *, n, slot_tile):
    tile = PLAN_TILE
    lane = lax.broadcasted_iota(I32, (tile, LANES), 1)

    def block(b):
        r = route_ref[pl.ds(pl.multiple_of(b * tile, tile), tile), :]
        e1 = r[:, 0:1].astype(I32)
        e2 = r[:, 1:2].astype(I32)
        return jnp.where((lane == e1) | (lane == e2), 1.0, 0.0), e1, e2

    def count(b, c):
        return c + jnp.sum(block(b)[0], axis=0, keepdims=True)

    cnt = lax.fori_loop(0, n // tile, count, jnp.zeros((1, LANES), F32), unroll=PLAN_UNROLL)
    ntile = jnp.floor((cnt + (slot_tile - 1)) * (1.0 / slot_tile))
    nt8 = jnp.broadcast_to(ntile, (SUBLANES, LANES)).astype(BF16)
    base_t = jnp.dot(nt8, utri_ref[...], preferred_element_type=F32)[0:1, :]
    base = base_t * slot_tile
    ends = base_t + ntile
    tl = lax.broadcasted_iota(I32, te_ref.shape, 0).astype(F32)
    el = lax.broadcasted_iota(I32, te_ref.shape, 1)
    hit = jnp.where((el < N_EXPERTS) & (ends <= tl), 1.0, 0.0)
    te_ref[...] = jnp.broadcast_to(jnp.sum(hit, axis=-1, keepdims=True), te_ref.shape).astype(I32)
    mrow = lax.broadcasted_iota(I32, meta_ref.shape, 0)
    meta_ref[...] = jnp.where(mrow == 0, cnt, jnp.where(mrow == 1, base, jnp.where(
        mrow == 2, ntile * slot_tile, 0.0))).astype(I32)

    def place(b, run):
        onehot, e1, e2 = block(b)
        rank = jnp.dot(ltri_ref[...], onehot.astype(BF16), preferred_element_type=F32)
        pos = base + run + rank
        d1 = jnp.sum(jnp.where(lane == e1, pos, 0.0), axis=-1, keepdims=True)
        d2 = jnp.sum(jnp.where(lane == e2, pos, 0.0), axis=-1, keepdims=True)
        dest_ref[pl.ds(pl.multiple_of(b * tile, tile), tile), :] = jnp.where(
            lane == 0, d1, jnp.where(lane == 1, d2, 0.0)).astype(I32)
        return run + jnp.sum(onehot, axis=0, keepdims=True)

    lax.fori_loop(0, n // tile, place, jnp.zeros((1, LANES), F32), unroll=PLAN_UNROLL)


def _plan(route, n, n_slot_tiles):
    tile = PLAN_TILE
    te_rows = -(-n_slot_tiles // SUBLANES) * SUBLANES
    ii = np.arange(tile)
    ltri = jnp.asarray((ii[None, :] < ii[:, None]).astype(np.float32), dtype=BF16)
    ee = np.arange(LANES)
    utri = jnp.asarray((ee[:, None] < ee[None, :]).astype(np.float32), dtype=BF16)
    kernel = functools.partial(_plan_kernel, n=n, slot_tile=SLOT_TILE)
    return pl.pallas_call(
        kernel,
        out_shape=(jax.ShapeDtypeStruct((n, LANES), I32),
                   jax.ShapeDtypeStruct((te_rows, LANES), I32),
                   jax.ShapeDtypeStruct((SUBLANES, LANES), I32)),
        compiler_params=pltpu.CompilerParams(vmem_limit_bytes=VMEM_LIMIT),
        name="plan",
    )(route, ltri, utri)


def _invert_slots(dest_ref, meta_ref, src_ref, n_assign, n_slots):
    def fill(start, stop):
        def body(s, c):
            src_ref[s] = 0
            return c

        lax.fori_loop(start, stop, body, 0)

    def per_expert(e, c):
        first = meta_ref[LANES + e]
        fill(first + meta_ref[e], first + meta_ref[2 * LANES + e])
        return c

    lax.fori_loop(0, N_EXPERTS, per_expert, 0)
    last = N_EXPERTS - 1
    fill(meta_ref[LANES + last] + meta_ref[2 * LANES + last], n_slots)

    def scatter(a, c):
        src_ref[dest_ref[a]] = a
        return c

    lax.fori_loop(0, n_assign, scatter, 0, unroll=8)


def _row_gather(src_hbm, idx_of_row, buf, sem, n_rows):
    def body(r, c):
        pltpu.make_async_copy(src_hbm.at[idx_of_row(r)],
                              buf.at[pl.ds(pl.multiple_of(r * CHUNKS, CHUNKS), CHUNKS), :], sem).start()
        return c

    lax.fori_loop(0, n_rows, body, 0, unroll=8)


def _row_wait(src_hbm, buf, sem, n_rows):
    def body(r, c):
        pltpu.make_async_copy(src_hbm.at[0], buf.at[pl.ds(0, CHUNKS), :], sem).wait()
        return c

    lax.fori_loop(0, n_rows, body, 0, unroll=8)


def _gathered_rows(buf, n_rows):
    return jnp.concatenate([buf[pl.ds(c, n_rows, stride=CHUNKS), :] for c in range(CHUNKS)], axis=1)


def _expert_kernel(te_ref, dest_ref, meta_ref, xn_hbm, wg_ref, wu_ref, wd_ref, ys_ref,
                   xbuf, sem, wg_bf, wu_bf, wd_bf, src_ref, *, n_tiles):
    i = pl.program_id(0)
    ts = SLOT_TILE
    ahead = GATHER_SLOTS - 1

    @pl.when(i == 0)
    def _():
        _invert_slots(dest_ref, meta_ref, src_ref, dest_ref.shape[0], n_tiles * ts)

    def is_active(t):
        return (t < n_tiles) & (te_ref[jnp.minimum(t, n_tiles - 1)] < N_EXPERTS)

    def token_of(tile, r):
        return src_ref[tile * ts + r] >> 1

    def start_tile(tile):
        sl = tile % GATHER_SLOTS
        _row_gather(xn_hbm, lambda r: token_of(tile, r), xbuf.at[sl], sem.at[sl], ts)

    def start_rows(tile, r0, n):
        sl = tile % GATHER_SLOTS
        for r in range(r0, r0 + n):
            pltpu.make_async_copy(xn_hbm.at[token_of(tile, r)],
                                  xbuf.at[sl, pl.ds(r * CHUNKS, CHUNKS), :], sem.at[sl]).start()

    @pl.when(i == 0)
    def _():
        for t in range(ahead):
            pl.when(is_active(t))(functools.partial(start_tile, t))

    e = te_ref[i]
    active = e < N_EXPERTS
    prev = te_ref[jnp.maximum(i - 1, 0)]

    @pl.when(active & ((i == 0) | (e != prev)))
    def _():
        wg_bf[...] = wg_ref[...].astype(BF16)
        wu_bf[...] = wu_ref[...].astype(BF16)
        wd_bf[...] = wd_ref[...].astype(BF16)

    slot = i % GATHER_SLOTS

    @pl.when(active)
    def _():
        _row_wait(xn_hbm, xbuf.at[slot], sem.at[slot], ts)

    def compute(prefetch):
        groups = 2 * EXPERT_SPLIT + DOWN_SPLIT
        per = ts // groups
        state = {"g": 0}

        def issue():
            if prefetch:
                start_rows(i + ahead, state["g"] * per, per)
            state["g"] += 1

        x = _gathered_rows(xbuf.at[slot], ts).astype(BF16)
        wcol = D_EXPERT // EXPERT_SPLIT
        a_parts, u_parts = [], []
        for c in range(EXPERT_SPLIT):
            issue()
            a_parts.append(jnp.dot(x, wg_bf[:, c * wcol:(c + 1) * wcol], preferred_element_type=F32))
            issue()
            u_parts.append(jnp.dot(x, wu_bf[:, c * wcol:(c + 1) * wcol], preferred_element_type=F32))
        a = jnp.concatenate(a_parts, axis=1)
        u = jnp.concatenate(u_parts, axis=1)
        hmid = (a * (1.0 / (1.0 + jnp.exp(-a))) * u).astype(BF16)
        ycol = D_MODEL // DOWN_SPLIT
        for c in range(DOWN_SPLIT):
            issue()
            y = jnp.dot(hmid, wd_bf[:, c * ycol:(c + 1) * ycol], preferred_element_type=F32)
            for cc in range(ycol // LANES):
                ys_ref[pl.ds(c * (ycol // LANES) + cc, ts, stride=CHUNKS), :] = y[:, cc * LANES:(cc + 1) * LANES]

    ahead_active = is_active(i + ahead)
    pl.when(active & ahead_active)(functools.partial(compute, True))
    pl.when(active & jnp.logical_not(ahead_active))(functools.partial(compute, False))

    @pl.when(jnp.logical_not(active))
    def _():
        ys_ref[...] = jnp.zeros(ys_ref.shape, F32)


def _experts(te, dest_flat, meta_flat, xn3d, w_gate, w_up, w_down, n_tiles):
    ts = SLOT_TILE
    wsel = lambda i, te, dest, meta: (jnp.minimum(te[i], N_EXPERTS - 1), 0, 0)
    kernel = functools.partial(_expert_kernel, n_tiles=n_tiles)
    grid_spec = pltpu.PrefetchScalarGridSpec(
        num_scalar_prefetch=3,
        grid=(n_tiles,),
        in_specs=[pl.BlockSpec(memory_space=pl.ANY),
                  pl.BlockSpec((None, D_MODEL, D_EXPERT), wsel),
                  pl.BlockSpec((None, D_MODEL, D_EXPERT), wsel),
                  pl.BlockSpec((None, D_EXPERT, D_MODEL), wsel)],
        out_specs=pl.BlockSpec((ts * CHUNKS, LANES), lambda i, te, dest, meta: (i, 0)),
        scratch_shapes=[pltpu.VMEM((GATHER_SLOTS, ts * CHUNKS, LANES), F32),
                        pltpu.SemaphoreType.DMA((GATHER_SLOTS,)),
                        pltpu.VMEM((D_MODEL, D_EXPERT), BF16),
                        pltpu.VMEM((D_MODEL, D_EXPERT), BF16),
                        pltpu.VMEM((D_EXPERT, D_MODEL), BF16),
                        pltpu.SMEM((n_tiles * ts,), I32)],
    )
    return pl.pallas_call(
        kernel,
        grid_spec=grid_spec,
        out_shape=jax.ShapeDtypeStruct((n_tiles * ts * CHUNKS, LANES), F32),
        compiler_params=_cparams(("arbitrary",)),
        name="experts",
    )(te, dest_flat, meta_flat, xn3d, w_gate, w_up, w_down)


def _combine_kernel(dest_ref, ys_hbm, h_ref, route_ref, y_ref, gbuf, sem, *, tm, n_steps, tok_off):
    i = pl.program_id(0)
    slot = i % 2
    nrow = 2 * tm

    def start(step, sl):
        base = (tok_off + step * tm) * 2
        _row_gather(ys_hbm, lambda r: dest_ref[base + r], gbuf.at[sl], sem.at[sl], nrow)

    @pl.when(i == 0)
    def _():
        start(0, 0)

    @pl.when(i + 1 < n_steps)
    def _():
        start(i + 1, 1 - slot)

    _row_wait(ys_hbm, gbuf.at[slot], sem.at[slot], nrow)
    buf = gbuf.at[slot]
    route = route_ref[...]
    w1 = route[:, 2:3]
    w2 = route[:, 3:4]
    for c in range(CHUNKS):
        g1 = buf[pl.ds(c, tm, stride=2 * CHUNKS), :]
        g2 = buf[pl.ds(CHUNKS + c, tm, stride=2 * CHUNKS), :]
        sl = slice(c * LANES, (c + 1) * LANES)
        y_ref[:, sl] = h_ref[:, sl] + w1 * g1 + w2 * g2


def _combine(dest_flat, ys3d, h_pool, route, tm, tok_off, n_tok):
    n_steps = n_tok // tm
    boff = tok_off // tm
    kernel = functools.partial(_combine_kernel, tm=tm, n_steps=n_steps, tok_off=tok_off)
    grid_spec = pltpu.PrefetchScalarGridSpec(
        num_scalar_prefetch=1,
        grid=(n_steps,),
        in_specs=[pl.BlockSpec(memory_space=pl.ANY),
                  pl.BlockSpec((tm, D_MODEL), lambda i, d: (i + boff, 0)),
                  pl.BlockSpec((tm, LANES), lambda i, d: (i + boff, 0))],
        out_specs=pl.BlockSpec((tm, D_MODEL), lambda i, d: (i, 0)),
        scratch_shapes=[pltpu.VMEM((2, 2 * tm * CHUNKS, LANES), F32), pltpu.SemaphoreType.DMA((2,))],
    )
    return pl.pallas_call(
        kernel,
        grid_spec=grid_spec,
        out_shape=jax.ShapeDtypeStruct((n_tok, D_MODEL), F32),
        compiler_params=_cparams(("arbitrary",)),
        name="combine",
    )(dest_flat, ys3d, h_pool, route)


def kernel(x_prompt, x_sample, cache_k, cache_v, state_ret, page_table, norm1, w_in, a_q_norm, a_k_norm,
           a_lambda_q1, a_lambda_k1, a_lambda_q2, a_lambda_k2, a_subln, r_norm, w_o, norm2,
           w_group_router, b_group_router, w_expert_router, b_expert_router, w_gate, w_up, w_down):
    depth = norm1.shape[0]
    assert depth == 1, "single-layer step"
    batch, seq, d = x_prompt.shape
    nsamp, tdec, _ = x_sample.shape
    assert d == D_MODEL and tdec == 1
    assert seq % TOKEN_TILE == 0 and seq % ATTN_TILE == 0 and seq % RET_CHUNK == 0
    assert nsamp % PLAN_TILE == 0 and nsamp % SAMPLE_BLOCK == 0
    past = page_table.shape[1] * cache_k.shape[2]
    n_prompt = batch * seq
    n_total = n_prompt + nsamp
    lam_init = 0.8 - 0.6 * math.exp(-0.3 * 0)
    l = 0

    lams = tuple(a[l].reshape(1, A_QK) for a in (a_lambda_q1, a_lambda_k1, a_lambda_q2, a_lambda_k2))
    w_in_bf = w_in[l].astype(BF16)
    w_o_bf = w_o[l].astype(BF16)
    g1 = norm1[l].reshape(1, D_MODEL)
    n2 = norm2[l].reshape(1, D_MODEL)
    qg = jnp.tile(a_q_norm[l], LANES // A_QK).reshape(1, LANES)
    kg = jnp.tile(a_k_norm[l], LANES // A_QK).reshape(1, LANES)
    asub = a_subln[l].reshape(1, A_ROW)
    rnorm = r_norm[l].reshape(1, R_V)
    ones = _segment_ones()
    w_r = jnp.concatenate([w_group_router[l], w_expert_router[l]], axis=1)
    w_r = jnp.pad(w_r, ((0, 0), (0, LANES - w_r.shape[1])))
    wr_hi = w_r.astype(BF16)
    wr_lo = (w_r - wr_hi.astype(F32)).astype(BF16)
    b_r = jnp.concatenate([b_group_router[l], b_expert_router[l]])
    b_r = jnp.pad(b_r, (0, LANES - b_r.shape[0])).reshape(1, LANES)

    pos_s = jnp.full((nsamp,), past, dtype=jnp.int32)
    xs = x_sample.reshape(nsamp, D_MODEL)
    (sq_bf, k_s, _, v_s, _, srq, srk, srv, srg) = _proj(
        xs, g1, w_in_bf, qg, kg, _rope_tables(pos_s), _ret_tables(pos_s), ones, PLAN_TILE)
    a_o_s = _paged_attn(page_table, sq_bf.astype(F32), k_s.reshape(nsamp, A_WIDTH), v_s.reshape(nsamp, A_WIDTH),
                        cache_k[l], cache_v[l], lams, lam_init)
    r_o_s, r_state_s = _ret_step(srq, srk, srv, state_ret[l].reshape(nsamp, R_QW, R_V))
    sample_rows = _mix_out(xs, a_o_s, r_o_s, srg, asub, rnorm, w_o_bf, n2, wr_hi, wr_lo, b_r,
                           lam_init, PLAN_TILE)

    pos_p = jnp.arange(seq)
    xp = x_prompt.reshape(n_prompt, D_MODEL)
    (q_bf, k_p, k_bf, v_p, v_bf, rq, rk, rv, rg) = _proj(
        xp, g1, w_in_bf, qg, kg, _rope_tables(pos_p), _ret_tables(pos_p), ones, TOKEN_TILE)
    a_o = _attn(q_bf, k_bf, v_bf, lams, batch, seq, lam_init)
    r_o, r_state_p = _retention(rq, rk, rv, batch, seq)
    h_pool, xn_pool, route = _mix_out(xp, a_o, r_o, rg, asub, rnorm, w_o_bf, n2, wr_hi, wr_lo, b_r,
                                      lam_init, TOKEN_TILE, tail=sample_rows)
    n_pool = h_pool.shape[0]

    n_assign = 2 * n_total
    n_tiles = -(-(n_assign + N_EXPERTS * (SLOT_TILE - 1)) // SLOT_TILE)
    n_slots = n_tiles * SLOT_TILE
    dest128, te128, meta = _plan(route, n_total, n_tiles)
    dest_flat = dest128[:, :2].reshape(-1)
    te = te128[:n_tiles, 0]
    ys = _experts(te, dest_flat, meta.reshape(-1), xn_pool.reshape(n_pool, CHUNKS, LANES),
                  w_gate[l].reshape(N_EXPERTS, D_MODEL, D_EXPERT),
                  w_up[l].reshape(N_EXPERTS, D_MODEL, D_EXPERT),
                  w_down[l].reshape(N_EXPERTS, D_EXPERT, D_MODEL), n_tiles)
    ys3d = ys.reshape(n_slots, CHUNKS, LANES)
    y_p = _combine(dest_flat, ys3d, h_pool, route, TOKEN_TILE, 0, n_prompt)
    y_s = _combine(dest_flat, ys3d, h_pool, route, PLAN_TILE, n_prompt, nsamp)

    return (y_p.reshape(batch, seq, D_MODEL),
            y_s.reshape(nsamp, 1, D_MODEL),
            k_p.reshape(1, batch, seq, A_HEADS, A_ROW),
            v_p.reshape(1, batch, seq, A_HEADS, A_ROW),
            r_state_p.reshape(1, batch, R_HEADS, R_QK, R_V),
            k_s.reshape(1, nsamp, 1, A_HEADS, A_ROW),
            v_s.reshape(1, nsamp, 1, A_HEADS, A_ROW),
            r_state_s.reshape(1, nsamp, R_HEADS, R_QK, R_V))
```

```python
import functools
import math

import numpy as np
import jax
import jax.numpy as jnp
from jax import lax
from jax.experimental import pallas as pl
from jax.experimental.pallas import tpu as pltpu

F32 = jnp.float32
BF16 = jnp.bfloat16
I32 = jnp.int32

LANES = 128
SUBLANES = 8
CHUNKS = 8

D_MODEL = 1024
A_HEADS = 4
A_QK = 64
A_ROW = 2 * A_QK
A_WIDTH = A_HEADS * A_ROW
ROPE_THETA = 500000.0
ROPE_DIM = A_QK // 4
R_HEADS = 4
R_QK = 64
R_V = 128
R_QW = R_HEADS * R_QK
R_WIDTH = R_HEADS * R_V
R_THETA = 10000.0
N_GROUPS = 4
EXPERTS_PER_GROUP = 8
N_EXPERTS = N_GROUPS * EXPERTS_PER_GROUP
D_EXPERT = 512
EPS = 1e-6
NEG = -1e30

TOKEN_TILE = 256
ATTN_TILE = 512
ONES_ROWS = 16
LOG2E = 1.4426950408889634
RET_CHUNK = 256
PLAN_TILE = 128
PLAN_UNROLL = 3
SLOT_TILE = 256
GATHER_SLOTS = 3
EXPERT_SPLIT = 2
DOWN_SPLIT = 4
SAMPLE_BLOCK = 8
VMEM_LIMIT = 56 * 1024 * 1024


def _cparams(sem, vmem=VMEM_LIMIT):
    return pltpu.CompilerParams(dimension_semantics=sem, vmem_limit_bytes=vmem)


def _rope_tables(pos):
    half = ROPE_DIM // 2
    inv = ROPE_THETA ** (-jnp.arange(half, dtype=F32) / half)
    ang = pos.astype(F32)[:, None] * inv[None, :]
    cos, sin = jnp.cos(ang), jnp.sin(ang)
    n = pos.shape[0]
    ones = jnp.ones((n, A_QK - ROPE_DIM), F32)
    zeros = jnp.zeros((n, A_QK - ROPE_DIM), F32)
    zh = jnp.zeros((n, half), F32)
    c = jnp.concatenate([cos, cos, ones], axis=1)
    s_next = jnp.concatenate([-sin, zh, zeros], axis=1)
    s_prev = jnp.concatenate([zh, sin, zeros], axis=1)
    rep = LANES // A_QK
    return tuple(jnp.tile(t, (1, rep)) for t in (c, s_next, s_prev))


def _ret_tables(pos):
    half = R_QK // 2
    inv = 1.0 / (R_THETA ** jnp.linspace(0.0, 1.0, half, dtype=F32))
    ang = pos.astype(F32)[:, None] * inv[None, :]
    cos, sin = jnp.cos(ang), jnp.sin(ang)
    z = jnp.zeros_like(sin)
    c = jnp.stack([cos, cos], axis=-1).reshape(-1, R_QK)
    s_next = jnp.stack([-sin, z], axis=-1).reshape(-1, R_QK)
    s_prev = jnp.stack([z, sin], axis=-1).reshape(-1, R_QK)
    rep = LANES // R_QK
    return tuple(jnp.tile(t, (1, rep)) for t in (c, s_next, s_prev))


def _segment_ones():
    seg = np.arange(LANES) // A_QK
    return jnp.asarray((seg[:, None] == seg[None, :]).astype(np.float32), dtype=BF16)


def _proj_kernel(x_ref, g1_ref, w_ref, qg_ref, kg_ref, rc_ref, rn_ref, rp_ref,
                 tc_ref, tn_ref, tp_ref, ones_ref,
                 qbf_ref, k_ref, kbf_ref, v_ref, vbf_ref, rq_ref, rk_ref, rv_ref, rg_ref):
    tm = x_ref.shape[0]
    x = x_ref[...]
    ms = jnp.mean(x * x, axis=-1, keepdims=True)
    n = (x * lax.rsqrt(ms + EPS) * g1_ref[...]).astype(BF16)
    h = jnp.dot(n, w_ref[...], preferred_element_type=F32)
    ones = ones_ref[...]
    rc, rn, rp = rc_ref[...], rn_ref[...], rp_ref[...]

    def head_norm_rope(xh, gain):
        sq = xh * xh
        hi = sq.astype(BF16)
        lo = (sq - hi.astype(F32)).astype(BF16)
        ssq = (jnp.dot(hi, ones, preferred_element_type=F32)
               + jnp.dot(lo, ones, preferred_element_type=F32))
        y = xh * lax.rsqrt(ssq * (1.0 / A_QK) + EPS) * gain
        half = ROPE_DIM // 2
        return (y * rc + pltpu.roll(y, LANES - half, 1) * rn + pltpu.roll(y, half, 1) * rp)

    for hd in range(A_HEADS):
        sl = slice(hd * A_ROW, (hd + 1) * A_ROW)
        q = head_norm_rope(h[:, sl], qg_ref[...])
        qbf_ref[:, sl] = (q * (A_QK ** -0.5 * LOG2E)).astype(BF16)
        k = head_norm_rope(h[:, A_WIDTH + hd * A_ROW:A_WIDTH + (hd + 1) * A_ROW], kg_ref[...])
        k_ref[pl.ds(hd, tm, stride=A_HEADS), :] = k
        kbf_ref[:, sl] = k.astype(BF16)
    o = 2 * A_WIDTH
    v = h[:, o:o + A_WIDTH]
    for hd in range(A_HEADS):
        v_ref[pl.ds(hd, tm, stride=A_HEADS), :] = v[:, hd * A_ROW:(hd + 1) * A_ROW]
    vbf_ref[...] = v.astype(BF16)
    o += A_WIDTH
    tc, tn, tp = tc_ref[...], tn_ref[...], tp_ref[...]

    def pair_rotate(xs):
        return xs * tc + pltpu.roll(xs, LANES - 1, 1) * tn + pltpu.roll(xs, 1, 1) * tp

    for j in range(R_QW // LANES):
        sl = slice(j * LANES, (j + 1) * LANES)
        rq_ref[:, sl] = pair_rotate(h[:, o + j * LANES:o + (j + 1) * LANES])
        rk_ref[:, sl] = pair_rotate(h[:, o + R_QW + j * LANES:o + R_QW + (j + 1) * LANES]) * (R_QK ** -0.5)
    o += 2 * R_QW
    rv_ref[...] = h[:, o:o + R_WIDTH]
    rg_ref[...] = h[:, o + R_WIDTH:o + 2 * R_WIDTH]


def _proj(x2d, g1, w_bf, qg, kg, rope_t, ret_t, ones, tm):
    t = x2d.shape[0]
    table_tiles = rope_t[0].shape[0] // tm
    row = lambda w: pl.BlockSpec((tm, w), lambda i: (i, 0))
    table = pl.BlockSpec((tm, LANES), lambda i: (i % table_tiles, 0))
    full = lambda a: pl.BlockSpec(a.shape, lambda i: (0,) * a.ndim)
    out_shape = (
        jax.ShapeDtypeStruct((t, A_WIDTH), BF16),
        jax.ShapeDtypeStruct((t * A_HEADS, A_ROW), F32),
        jax.ShapeDtypeStruct((t, A_WIDTH), BF16),
        jax.ShapeDtypeStruct((t * A_HEADS, A_ROW), F32),
        jax.ShapeDtypeStruct((t, A_WIDTH), BF16),
        jax.ShapeDtypeStruct((t, R_QW), F32),
        jax.ShapeDtypeStruct((t, R_QW), F32),
        jax.ShapeDtypeStruct((t, R_WIDTH), F32),
        jax.ShapeDtypeStruct((t, R_WIDTH), F32),
    )
    return pl.pallas_call(
        _proj_kernel,
        grid=(t // tm,),
        in_specs=[row(D_MODEL), full(g1), full(w_bf), full(qg), full(kg)]
                 + [table] * 6 + [full(ones)],
        out_specs=tuple(pl.BlockSpec((tm * s.shape[0] // t, s.shape[1]), lambda i: (i, 0)) for s in out_shape),
        out_shape=out_shape,
        compiler_params=_cparams(("parallel",)),
        name="proj",
    )(x2d, g1, w_bf, qg, kg, *rope_t, *ret_t, ones)


def _lambda(lq1_ref, lk1_ref, lq2_ref, lk2_ref, lam_init):
    s1 = jnp.sum(lq1_ref[...] * lk1_ref[...], axis=-1, keepdims=True)
    s2 = jnp.sum(lq2_ref[...] * lk2_ref[...], axis=-1, keepdims=True)
    return jnp.exp(s1) - jnp.exp(s2) + lam_init


def _attn_kernel(pt_ref, q_ref, k_ref, v_ref, lq1_ref, lk1_ref, lq2_ref, lk2_ref,
                 sq_ref, skn_ref, svn_ref, bias_ref, nbias_ref, *rest, tile, lam_init, npages):
    k_pages = rest[:npages]
    v_pages = rest[npages:2 * npages]
    o_ref, os_ref, m_ref, l_ref, acc_ref, qq_ref, sa_ref, sb_ref = rest[2 * npages:]
    del pt_ref
    lam = _lambda(lq1_ref, lk1_ref, lq2_ref, lk2_ref, lam_init)
    _paged_step(sq_ref, skn_ref, svn_ref, bias_ref, nbias_ref, lam, k_pages, v_pages, os_ref)

    i = pl.program_id(2)
    q = q_ref[...]
    lane = lax.broadcasted_iota(I32, q.shape, 1)
    zero = jnp.zeros_like(q)
    qq = jnp.concatenate([jnp.where(lane < A_QK, q, zero), jnp.where(lane >= A_QK, q, zero)], axis=0)
    qq_ref[...] = qq
    m_ref[...] = jnp.full(m_ref.shape, NEG, F32)
    l_ref[...] = jnp.zeros(l_ref.shape, F32)
    acc_ref[...] = jnp.zeros(acc_ref.shape, F32)

    ones_rows = jnp.ones((ONES_ROWS, tile), BF16)

    def scores(j, s_ref):
        k = k_ref[pl.ds(pl.multiple_of(j * tile, tile), tile), :]
        s_ref[...] = lax.dot_general(k, qq_ref[...], (((1,), (1,)), ((), ())), preferred_element_type=F32)

    def accumulate(j, s_ref, masked):
        v = v_ref[pl.ds(pl.multiple_of(j * tile, tile), tile), :]
        vt = jnp.concatenate([v.T, ones_rows], axis=0)
        s = s_ref[...]
        if masked:
            key = lax.broadcasted_iota(I32, s.shape, 0)
            qry = lax.broadcasted_iota(I32, s.shape, 1)
            qry = jnp.where(qry >= tile, qry - tile, qry)
            s = jnp.where(key <= qry, s, NEG)
        m_prev = m_ref[...]
        m_new = jnp.maximum(m_prev, jnp.max(s, axis=0, keepdims=True))
        alpha = jnp.exp2(m_prev - m_new)
        p = jnp.exp2(s - m_new).astype(BF16)
        pv = jnp.dot(vt, p, preferred_element_type=F32)
        acc_ref[...] = alpha * acc_ref[...] + pv[:A_ROW, :]
        l_ref[...] = alpha * l_ref[...] + pv[A_ROW:A_ROW + 1, :]
        m_ref[...] = m_new

    scores(0, sa_ref)

    def pair(t, carry):
        j = 2 * t
        scores(j + 1, sb_ref)
        accumulate(j, sa_ref, False)
        scores(j + 2, sa_ref)
        accumulate(j + 1, sb_ref, False)
        return carry

    lax.fori_loop(0, i // 2, pair, 0)

    @pl.when(i % 2 == 0)
    def _():
        accumulate(i, sa_ref, True)

    @pl.when(i % 2 == 1)
    def _():
        scores(i, sb_ref)
        accumulate(i - 1, sa_ref, False)
        accumulate(i, sb_ref, True)

    o1 = acc_ref[:, :tile] / l_ref[:, :tile]
    o2 = acc_ref[:, tile:] / l_ref[:, tile:]
    o_ref[...] = (o1 - lam * o2).T


def _attn(q_bf, k_bf, v_bf, lams, batch, seq, lam_init, page_table, sq, k_new, v_new, cache_k, cache_v):
    tile = ATTN_TILE
    nq = seq // tile
    nseq, npages = page_table.shape
    assert nseq == batch * A_HEADS * nq, "one decode sequence per prompt query block"
    n_phys, page = cache_k.shape[0], cache_k.shape[1]
    prow = page * A_HEADS
    ck = cache_k.reshape(n_phys, prow, A_ROW)
    cv = cache_v.reshape(n_phys, prow, A_ROW)
    r = np.arange(2 * A_HEADS)[:, None] % A_HEADS
    bias = jnp.asarray(np.where(np.arange(prow)[None, :] % A_HEADS == r, 0.0, NEG).astype(np.float32))
    nbias = jnp.asarray(np.where(np.arange(LANES)[None, :] == r, 0.0, NEG).astype(np.float32))
    step = lambda b, h, i: (b * A_HEADS + h) * nq + i
    lam_spec = pl.BlockSpec((1, A_QK), lambda b, h, i, pt: (0, 0))
    tok = pl.BlockSpec((None, 1, A_WIDTH), lambda b, h, i, pt: (step(b, h, i), 0, 0))
    full = lambda a: pl.BlockSpec(a.shape, lambda b, h, i, pt: (0,) * a.ndim)

    def page_spec(jj):
        return pl.BlockSpec((None, prow, A_ROW), lambda b, h, i, pt: (pt[step(b, h, i) * npages + jj], 0, 0))

    kernel = functools.partial(_attn_kernel, tile=tile, lam_init=lam_init, npages=npages)
    grid_spec = pltpu.PrefetchScalarGridSpec(
        num_scalar_prefetch=1,
        grid=(batch, A_HEADS, nq),
        in_specs=[pl.BlockSpec((tile, A_ROW), lambda b, h, i, pt: (b * nq + i, h)),
                  pl.BlockSpec((seq, A_ROW), lambda b, h, i, pt: (b, h)),
                  pl.BlockSpec((seq, A_ROW), lambda b, h, i, pt: (b, h))] + [lam_spec] * 4
                 + [tok, tok, tok, full(bias), full(nbias)]
                 + [page_spec(jj) for jj in range(npages)] * 2,
        out_specs=(pl.BlockSpec((tile, A_ROW), lambda b, h, i, pt: (b * nq + i, h)), tok),
        scratch_shapes=[pltpu.VMEM((1, 2 * tile), F32), pltpu.VMEM((1, 2 * tile), F32),
                        pltpu.VMEM((A_ROW, 2 * tile), F32), pltpu.VMEM((2 * tile, A_ROW), BF16),
                        pltpu.VMEM((tile, 2 * tile), F32), pltpu.VMEM((tile, 2 * tile), F32)],
    )
    a_o, a_o_s = pl.pallas_call(
        kernel,
        grid_spec=grid_spec,
        out_shape=(jax.ShapeDtypeStruct((batch * seq, A_WIDTH), F32),
                   jax.ShapeDtypeStruct((nseq, 1, A_WIDTH), F32)),
        compiler_params=_cparams(("parallel", "parallel", "arbitrary")),
        name="attn",
    )(page_table.reshape(-1), q_bf, k_bf, v_bf, *lams,
      sq.reshape(nseq, 1, A_WIDTH), k_new.reshape(nseq, 1, A_WIDTH), v_new.reshape(nseq, 1, A_WIDTH),
      bias, nbias, *([ck] * npages), *([cv] * npages))
    return a_o, a_o_s.reshape(nseq, A_WIDTH)


def _ret_decay():
    return [math.log(1.0 - 2.0 ** (-5.0 - h)) for h in range(R_HEADS)]


def _ret_tables_chunk(chunk):
    log_g = jnp.log(1.0 - 2.0 ** (-5.0 - jnp.arange(R_HEADS, dtype=F32)))
    idx = jnp.arange(chunk, dtype=F32)
    diff = idx[:, None] - idx[None, :]
    dmask = jnp.where(diff >= 0, jnp.exp(jnp.maximum(diff, 0.0)[None] * log_g[:, None, None]), 0.0)
    q_dec = jnp.exp((idx + 1.0)[:, None] * log_g[None, :])
    k_dec = jnp.exp((chunk - 1.0 - idx)[:, None] * log_g[None, :])
    q_dec = jnp.repeat(q_dec, R_QK, axis=1)
    k_dec = jnp.repeat(k_dec, R_QK, axis=1)
    g_chunk = jnp.exp(chunk * log_g)
    g_rows = jnp.broadcast_to(jnp.repeat(g_chunk, R_QK)[:, None], (R_QW, R_V))
    return dmask, q_dec, k_dec, g_rows


def _ret_kernel(q_ref, k_ref, v_ref, dmask_ref, qdec_ref, kdec_ref, grow_ref,
                o_ref, st_ref, state_ref):
    c = pl.program_id(1)

    @pl.when(c == 0)
    def _():
        state_ref[...] = jnp.zeros(state_ref.shape, F32)

    q = q_ref[...]
    k = k_ref[...]
    qd = (q * qdec_ref[...]).astype(BF16)
    kd = (k * kdec_ref[...]).astype(BF16)
    qb = q.astype(BF16)
    kb = k.astype(BF16)
    vb = v_ref[...].astype(BF16)
    for h in range(R_HEADS):
        ks = slice(h * R_QK, (h + 1) * R_QK)
        vs = slice(h * R_V, (h + 1) * R_V)
        state = state_ref[ks, :]
        inner = lax.dot_general(qb[:, ks], kb[:, ks], (((1,), (1,)), ((), ())),
                                preferred_element_type=F32) * dmask_ref[h]
        o = (jnp.dot(inner.astype(BF16), vb[:, vs], preferred_element_type=F32)
             + jnp.dot(qd[:, ks], state.astype(BF16), preferred_element_type=F32))
        o_ref[:, vs] = o
        upd = lax.dot_general(kd[:, ks], vb[:, vs], (((0,), (0,)), ((), ())),
                              preferred_element_type=F32)
        state_ref[ks, :] = grow_ref[ks, :] * state + upd
    st_ref[...] = state_ref[...]


def _retention(rq, rk, rv, batch, seq):
    chunk = RET_CHUNK
    nc = seq // chunk
    dmask, q_dec, k_dec, g_rows = _ret_tables_chunk(chunk)
    row = lambda w: pl.BlockSpec((chunk, w), lambda b, c: (b * nc + c, 0))
    full = lambda a: pl.BlockSpec(a.shape, lambda b, c: (0,) * a.ndim)
    return pl.pallas_call(
        _ret_kernel,
        grid=(batch, nc),
        in_specs=[row(R_QW), row(R_QW), row(R_WIDTH), full(dmask), full(q_dec), full(k_dec), full(g_rows)],
        out_specs=(row(R_WIDTH), pl.BlockSpec((None, R_QW, R_V), lambda b, c: (b, 0, 0))),
        out_shape=(jax.ShapeDtypeStruct((batch * seq, R_WIDTH), F32),
                   jax.ShapeDtypeStruct((batch, R_QW, R_V), F32)),
        scratch_shapes=[pltpu.VMEM((R_QW, R_V), F32)],
        compiler_params=_cparams(("parallel", "arbitrary")),
        name="retention",
    )(rq, rk, rv, dmask, q_dec, k_dec, g_rows)


def _ret_step_kernel(q_ref, k_ref, v_ref, state_ref, grow_ref, o_ref, ns_ref):
    g = grow_ref[...]
    for t in range(SAMPLE_BLOCK):
        kcol = jnp.broadcast_to(k_ref[t:t + 1, :], (LANES, R_QW)).T
        qcol = jnp.broadcast_to(q_ref[t:t + 1, :], (LANES, R_QW)).T
        vrows = jnp.concatenate(
            [jnp.broadcast_to(v_ref[t:t + 1, h * R_V:(h + 1) * R_V], (R_QK, R_V)) for h in range(R_HEADS)],
            axis=0)
        new = g * state_ref[t] + kcol * vrows
        ns_ref[t] = new
        qn = qcol * new
        for h in range(R_HEADS):
            o_ref[t:t + 1, h * R_V:(h + 1) * R_V] = jnp.sum(qn[h * R_QK:(h + 1) * R_QK, :], axis=0, keepdims=True)


def _ret_step(rq, rk, rv, state):
    n = rq.shape[0]
    bb = SAMPLE_BLOCK
    log_g = jnp.log(1.0 - 2.0 ** (-5.0 - jnp.arange(R_HEADS, dtype=F32)))
    g_rows = jnp.broadcast_to(jnp.repeat(jnp.exp(log_g), R_QK)[:, None], (R_QW, R_V))
    row = lambda w: pl.BlockSpec((bb, w), lambda i: (i, 0))
    st = pl.BlockSpec((bb, R_QW, R_V), lambda i: (i, 0, 0))
    return pl.pallas_call(
        _ret_step_kernel,
        grid=(n // bb,),
        in_specs=[row(R_QW), row(R_QW), row(R_WIDTH), st, pl.BlockSpec((R_QW, R_V), lambda i: (0, 0))],
        out_specs=(row(R_WIDTH), st),
        out_shape=(jax.ShapeDtypeStruct((n, R_WIDTH), F32), jax.ShapeDtypeStruct((n, R_QW, R_V), F32)),
        compiler_params=_cparams(("parallel",)),
        name="ret_step",
    )(rq, rk, rv, state, g_rows)


def _paged_step(q_ref, kn_ref, vn_ref, bias_ref, nbias_ref, lam, k_refs, v_refs, o_ref):
    npages = len(k_refs)
    q = q_ref[...]
    nrow = 2 * A_HEADS

    def head_rows(x, n):
        row = lax.broadcasted_iota(I32, (n, A_ROW), 0)
        out = jnp.zeros((n, A_ROW), F32)
        for h in range(A_HEADS):
            out = jnp.where(row == h, jnp.broadcast_to(x[:, h * A_ROW:(h + 1) * A_ROW], (n, A_ROW)), out)
        return out

    row8 = lax.broadcasted_iota(I32, (nrow, A_ROW), 0)
    lane8 = lax.broadcasted_iota(I32, (nrow, A_ROW), 1)
    q4 = head_rows(q, nrow)
    q8 = q4 + pltpu.roll(q4, A_HEADS, 0)
    qm = jnp.where((row8 < A_HEADS) == (lane8 < A_QK), q8, 0.0).astype(BF16)

    nt = (((1,), (1,)), ((), ()))
    bias = bias_ref[...]
    s = [lax.dot_general(qm, k_refs[j][...].astype(BF16), nt, preferred_element_type=F32) + bias
         for j in range(npages)]
    kn = head_rows(kn_ref[...], LANES).astype(BF16)
    s.append(lax.dot_general(qm, kn, nt, preferred_element_type=F32) + nbias_ref[...])
    m = functools.reduce(jnp.maximum, [jnp.max(x, axis=-1, keepdims=True) for x in s])
    p = [jnp.exp2(x - m) for x in s]
    l = functools.reduce(lambda a, b: a + b, [jnp.sum(x, axis=-1, keepdims=True) for x in p])
    inv = 1.0 / l
    vs = [v_refs[j][...].astype(BF16) for j in range(npages)] + [head_rows(vn_ref[...], LANES).astype(BF16)]
    out = jnp.zeros((nrow, A_ROW), F32)
    for pj, vj in zip(p, vs):
        pn = pj * inv
        first = lax.broadcasted_iota(I32, pn.shape, 0) < A_HEADS
        w8 = jnp.where(first, pn - lam * pltpu.roll(pn, A_HEADS, 0), 0.0).astype(BF16)
        out = out + jnp.dot(w8, vj, preferred_element_type=F32)
    for h in range(A_HEADS):
        o_ref[:, h * A_ROW:(h + 1) * A_ROW] = out[h:h + 1, :]


def _mix_tile(x_ref, a_ref, r_ref, g_ref, asub_ref, rnorm_ref, wo_ref, n2_ref,
              wrc_ref, wrh_ref, br_ref, h_ref, xn_ref, route_ref, *, lam_init):
    tm = x_ref.shape[0]
    parts = []
    for hd in range(A_HEADS):
        a = a_ref[:, hd * A_ROW:(hd + 1) * A_ROW]
        ms = jnp.mean(a * a, axis=-1, keepdims=True)
        parts.append((a * lax.rsqrt(ms + EPS) * asub_ref[...] * (1.0 - lam_init)).astype(BF16))
    for hd in range(R_HEADS):
        sl = slice(hd * R_V, (hd + 1) * R_V)
        r = r_ref[:, sl]
        ms = jnp.mean(r * r, axis=-1, keepdims=True)
        gate = g_ref[:, sl]
        gate = gate * (1.0 / (1.0 + jnp.exp(-gate)))
        parts.append((r * lax.rsqrt(ms + EPS) * rnorm_ref[...] * gate).astype(BF16))
    merged = jnp.concatenate(parts, axis=1)
    h = x_ref[...] + jnp.dot(merged, wo_ref[...], preferred_element_type=F32)
    h_ref[...] = h
    ms = jnp.mean(h * h, axis=-1, keepdims=True)
    xn = h * lax.rsqrt(ms + EPS) * n2_ref[...]
    for c in range(CHUNKS):
        xn_ref[pl.ds(c, tm, stride=CHUNKS), :] = xn[:, c * LANES:(c + 1) * LANES]
    xh = xn.astype(BF16)
    xl = (xn - xh.astype(F32)).astype(BF16)
    both = jnp.dot(xh, wrc_ref[...], preferred_element_type=F32)
    logits = (both[:, :LANES] + jnp.dot(xl, wrh_ref[...], preferred_element_type=F32)
              + both[:, LANES:]) + br_ref[...]
    lane = lax.broadcasted_iota(I32, logits.shape, 1).astype(F32)
    big = float(LANES)
    gl = jnp.where(lane < N_GROUPS, logits, NEG)
    gmax = jnp.max(gl, axis=-1, keepdims=True)
    gidx = jnp.min(jnp.where(gl == gmax, lane, big), axis=-1, keepdims=True)
    gsum = jnp.sum(jnp.where(lane < N_GROUPS, jnp.exp(gl - gmax), 0.0), axis=-1, keepdims=True)
    gprob = 1.0 / gsum
    lo = N_GROUPS + EXPERTS_PER_GROUP * gidx
    el = jnp.where((lane >= lo) & (lane < lo + EXPERTS_PER_GROUP), logits, NEG)
    v1 = jnp.max(el, axis=-1, keepdims=True)
    i1 = jnp.min(jnp.where(el == v1, lane, big), axis=-1, keepdims=True)
    el2 = jnp.where(lane == i1, NEG, el)
    v2 = jnp.max(el2, axis=-1, keepdims=True)
    i2 = jnp.min(jnp.where(el2 == v2, lane, big), axis=-1, keepdims=True)
    e = jnp.exp(v2 - v1)
    w1 = gprob / (1.0 + e)
    w2 = gprob * e / (1.0 + e)
    e1 = i1 - N_GROUPS
    e2 = i2 - N_GROUPS
    route_ref[...] = jnp.where(lane == 0, e1, jnp.where(lane == 1, e2, jnp.where(
        lane == 2, w1, jnp.where(lane == 3, w2, 0.0))))


def _mix_kernel(*refs, lam_init, n_tiles, has_tail):
    if not has_tail:
        _mix_tile(*refs, lam_init=lam_init)
        return
    ins, (th_ref, txn_ref, troute_ref), outs = refs[:11], refs[11:14], refs[14:]
    h_ref, xn_ref, route_ref = outs
    i = pl.program_id(0)

    @pl.when(i < n_tiles)
    def _():
        _mix_tile(*ins, *outs, lam_init=lam_init)

    @pl.when(i == n_tiles)
    def _():
        tm = h_ref.shape[0]
        nt = th_ref.shape[0]
        h_ref[:nt, :] = th_ref[...]
        h_ref[nt:, :] = jnp.zeros((tm - nt, D_MODEL), F32)
        xn_ref[:nt * CHUNKS, :] = txn_ref[...]
        xn_ref[nt * CHUNKS:, :] = jnp.zeros(((tm - nt) * CHUNKS, LANES), F32)
        route_ref[:nt, :] = troute_ref[...]
        route_ref[nt:, :] = jnp.zeros((tm - nt, LANES), F32)


def _mix_out(x2d, a_o, r_o, rg, asub, rnorm, wo_bf, n2, wr_cat, wr_hi, br, lam_init, tm, tail=None):
    t = x2d.shape[0]
    n_tiles = t // tm
    has_tail = tail is not None
    n_out = n_tiles + (1 if has_tail else 0)
    row = lambda w: pl.BlockSpec((tm, w), lambda i: (jnp.minimum(i, n_tiles - 1), 0))
    full = lambda a: pl.BlockSpec(a.shape, lambda i: (0,) * a.ndim)
    out_shape = (jax.ShapeDtypeStruct((n_out * tm, D_MODEL), F32),
                 jax.ShapeDtypeStruct((n_out * tm * CHUNKS, LANES), F32),
                 jax.ShapeDtypeStruct((n_out * tm, LANES), F32))
    out_specs = (pl.BlockSpec((tm, D_MODEL), lambda i: (i, 0)),
                 pl.BlockSpec((tm * CHUNKS, LANES), lambda i: (i, 0)),
                 pl.BlockSpec((tm, LANES), lambda i: (i, 0)))
    in_specs = [row(D_MODEL), row(A_WIDTH), row(R_WIDTH), row(R_WIDTH), full(asub), full(rnorm),
                full(wo_bf), full(n2), full(wr_cat), full(wr_hi), full(br)]
    args = [x2d, a_o, r_o, rg, asub, rnorm, wo_bf, n2, wr_cat, wr_hi, br]
    if has_tail:
        assert tail[0].shape[0] <= tm
        in_specs += [full(a) for a in tail]
        args += list(tail)
    kernel = functools.partial(_mix_kernel, lam_init=lam_init, n_tiles=n_tiles, has_tail=has_tail)
    return pl.pallas_call(
        kernel,
        grid=(n_out,),
        in_specs=in_specs,
        out_specs=out_specs,
        out_shape=out_shape,
        compiler_params=_cparams(("arbitrary",)),
        name="mix_out",
    )(*args)


def _plan_kernel(route_ref, ltri_ref, utri_ref, dest_ref, te_ref, meta_ref, *, n, slot_tile):
    tile = PLAN_TILE
    lane = lax.broadcasted_iota(I32, (tile, LANES), 1)

    def block(b):
        r = route_ref[pl.ds(pl.multiple_of(b * tile, tile), tile), :]
        e1 = r[:, 0:1].astype(I32)
        e2 = r[:, 1:2].astype(I32)
        return jnp.where((lane == e1) | (lane == e2), 1.0, 0.0), e1, e2

    def count(b, c):
        return c + jnp.sum(block(b)[0], axis=0, keepdims=True)

    cnt = lax.fori_loop(0, n // tile, count, jnp.zeros((1, LANES), F32), unroll=PLAN_UNROLL)
    ntile = jnp.floor((cnt + (slot_tile - 1)) * (1.0 / slot_tile))
    nt8 = jnp.broadcast_to(ntile, (SUBLANES, LANES)).astype(BF16)
    base_t = jnp.dot(nt8, utri_ref[...], preferred_element_type=F32)[0:1, :]
    base = base_t * slot_tile
    ends = base_t + ntile
    tl = lax.broadcasted_iota(I32, te_ref.shape, 0).astype(F32)
    el = lax.broadcasted_iota(I32, te_ref.shape, 1)
    hit = jnp.where((el < N_EXPERTS) & (ends <= tl), 1.0, 0.0)
    te_ref[...] = jnp.broadcast_to(jnp.sum(hit, axis=-1, keepdims=True), te_ref.shape).astype(I32)
    mrow = lax.broadcasted_iota(I32, meta_ref.shape, 0)
    meta_ref[...] = jnp.where(mrow == 0, cnt, jnp.where(mrow == 1, base, jnp.where(
        mrow == 2, ntile * slot_tile, 0.0))).astype(I32)

    def place(b, run):
        onehot, e1, e2 = block(b)
        rank = jnp.dot(ltri_ref[...], onehot.astype(BF16), preferred_element_type=F32)
        pos = base + run + rank
        d1 = jnp.sum(jnp.where(lane == e1, pos, 0.0), axis=-1, keepdims=True)
        d2 = jnp.sum(jnp.where(lane == e2, pos, 0.0), axis=-1, keepdims=True)
        dest_ref[pl.ds(pl.multiple_of(b * tile, tile), tile), :] = jnp.where(
            lane == 0, d1, jnp.where(lane == 1, d2, 0.0)).astype(I32)
        return run + jnp.sum(onehot, axis=0, keepdims=True)

    lax.fori_loop(0, n // tile, place, jnp.zeros((1, LANES), F32), unroll=PLAN_UNROLL)


def _plan(route, n, n_slot_tiles):
    tile = PLAN_TILE
    te_rows = -(-n_slot_tiles // SUBLANES) * SUBLANES
    ii = np.arange(tile)
    ltri = jnp.asarray((ii[None, :] < ii[:, None]).astype(np.float32), dtype=BF16)
    ee = np.arange(LANES)
    utri = jnp.asarray((ee[:, None] < ee[None, :]).astype(np.float32), dtype=BF16)
    kernel = functools.partial(_plan_kernel, n=n, slot_tile=SLOT_TILE)
    return pl.pallas_call(
        kernel,
        out_shape=(jax.ShapeDtypeStruct((n, LANES), I32),
                   jax.ShapeDtypeStruct((te_rows, LANES), I32),
                   jax.ShapeDtypeStruct((SUBLANES, LANES), I32)),
        compiler_params=pltpu.CompilerParams(vmem_limit_bytes=VMEM_LIMIT),
        name="plan",
    )(route, ltri, utri)


def _invert_slots(dest_ref, meta_ref, src_ref, n_assign, n_slots):
    def fill(start, stop):
        def body(s, c):
            src_ref[s] = 0
            return c

        lax.fori_loop(start, stop, body, 0)

    def per_expert(e, c):
        first = meta_ref[LANES + e]
        fill(first + meta_ref[e], first + meta_ref[2 * LANES + e])
        return c

    lax.fori_loop(0, N_EXPERTS, per_expert, 0)
    last = N_EXPERTS - 1
    fill(meta_ref[LANES + last] + meta_ref[2 * LANES + last], n_slots)

    def scatter(a, c):
        src_ref[dest_ref[a]] = a
        return c

    lax.fori_loop(0, n_assign, scatter, 0, unroll=8)


def _row_gather(src_hbm, idx_of_row, buf, sem, n_rows):
    def body(r, c):
        pltpu.make_async_copy(src_hbm.at[idx_of_row(r)],
                              buf.at[pl.ds(pl.multiple_of(r * CHUNKS, CHUNKS), CHUNKS), :], sem).start()
        return c

    lax.fori_loop(0, n_rows, body, 0, unroll=8)


def _row_wait(src_hbm, buf, sem, n_rows):
    def body(r, c):
        pltpu.make_async_copy(src_hbm.at[0], buf.at[pl.ds(0, CHUNKS), :], sem).wait()
        return c

    lax.fori_loop(0, n_rows, body, 0, unroll=8)


def _gathered_rows(buf, n_rows):
    return jnp.concatenate([buf[pl.ds(c, n_rows, stride=CHUNKS), :] for c in range(CHUNKS)], axis=1)


def _expert_kernel(te_ref, dest_ref, meta_ref, xn_hbm, wg_ref, wu_ref, wd_ref, ys_ref,
                   xbuf, sem, wg_bf, wu_bf, wd_bf, src_ref, *, n_tiles):
    i = pl.program_id(0)
    ts = SLOT_TILE
    ahead = GATHER_SLOTS - 1

    @pl.when(i == 0)
    def _():
        _invert_slots(dest_ref, meta_ref, src_ref, dest_ref.shape[0], n_tiles * ts)

    def is_active(t):
        return (t < n_tiles) & (te_ref[jnp.minimum(t, n_tiles - 1)] < N_EXPERTS)

    def token_of(tile, r):
        return src_ref[tile * ts + r] >> 1

    def start_tile(tile):
        sl = tile % GATHER_SLOTS
        _row_gather(xn_hbm, lambda r: token_of(tile, r), xbuf.at[sl], sem.at[sl], ts)

    def start_rows(tile, r0, n):
        sl = tile % GATHER_SLOTS
        for r in range(r0, r0 + n):
            pltpu.make_async_copy(xn_hbm.at[token_of(tile, r)],
                                  xbuf.at[sl, pl.ds(r * CHUNKS, CHUNKS), :], sem.at[sl]).start()

    @pl.when(i == 0)
    def _():
        for t in range(ahead):
            pl.when(is_active(t))(functools.partial(start_tile, t))

    e = te_ref[i]
    active = e < N_EXPERTS
    prev = te_ref[jnp.maximum(i - 1, 0)]

    @pl.when(active & ((i == 0) | (e != prev)))
    def _():
        wg_bf[...] = wg_ref[...].astype(BF16)
        wu_bf[...] = wu_ref[...].astype(BF16)
        wd_bf[...] = wd_ref[...].astype(BF16)

    slot = i % GATHER_SLOTS

    @pl.when(active)
    def _():
        _row_wait(xn_hbm, xbuf.at[slot], sem.at[slot], ts)

    def compute(prefetch):
        groups = 2 * EXPERT_SPLIT + DOWN_SPLIT
        per = ts // groups
        state = {"g": 0}

        def issue():
            if prefetch:
                start_rows(i + ahead, state["g"] * per, per)
            state["g"] += 1

        x = _gathered_rows(xbuf.at[slot], ts).astype(BF16)
        wcol = D_EXPERT // EXPERT_SPLIT
        a_parts, u_parts = [], []
        for c in range(EXPERT_SPLIT):
            issue()
            a_parts.append(jnp.dot(x, wg_bf[:, c * wcol:(c + 1) * wcol], preferred_element_type=F32))
            issue()
            u_parts.append(jnp.dot(x, wu_bf[:, c * wcol:(c + 1) * wcol], preferred_element_type=F32))
        a = jnp.concatenate(a_parts, axis=1)
        u = jnp.concatenate(u_parts, axis=1)
        hmid = (a * (1.0 / (1.0 + jnp.exp(-a))) * u).astype(BF16)
        ycol = D_MODEL // DOWN_SPLIT
        for c in range(DOWN_SPLIT):
            issue()
            y = jnp.dot(hmid, wd_bf[:, c * ycol:(c + 1) * ycol], preferred_element_type=F32)
            for cc in range(ycol // LANES):
                ys_ref[pl.ds(c * (ycol // LANES) + cc, ts, stride=CHUNKS), :] = y[:, cc * LANES:(cc + 1) * LANES]

    ahead_active = is_active(i + ahead)
    pl.when(active & ahead_active)(functools.partial(compute, True))
    pl.when(active & jnp.logical_not(ahead_active))(functools.partial(compute, False))

    @pl.when(jnp.logical_not(active))
    def _():
        ys_ref[...] = jnp.zeros(ys_ref.shape, F32)


def _experts(te, dest_flat, meta_flat, xn3d, w_gate, w_up, w_down, n_tiles):
    ts = SLOT_TILE
    wsel = lambda i, te, dest, meta: (jnp.minimum(te[i], N_EXPERTS - 1), 0, 0)
    kernel = functools.partial(_expert_kernel, n_tiles=n_tiles)
    grid_spec = pltpu.PrefetchScalarGridSpec(
        num_scalar_prefetch=3,
        grid=(n_tiles,),
        in_specs=[pl.BlockSpec(memory_space=pl.ANY),
                  pl.BlockSpec((None, D_MODEL, D_EXPERT), wsel),
                  pl.BlockSpec((None, D_MODEL, D_EXPERT), wsel),
                  pl.BlockSpec((None, D_EXPERT, D_MODEL), wsel)],
        out_specs=pl.BlockSpec((ts * CHUNKS, LANES), lambda i, te, dest, meta: (i, 0)),
        scratch_shapes=[pltpu.VMEM((GATHER_SLOTS, ts * CHUNKS, LANES), F32),
                        pltpu.SemaphoreType.DMA((GATHER_SLOTS,)),
                        pltpu.VMEM((D_MODEL, D_EXPERT), BF16),
                        pltpu.VMEM((D_MODEL, D_EXPERT), BF16),
                        pltpu.VMEM((D_EXPERT, D_MODEL), BF16),
                        pltpu.SMEM((n_tiles * ts,), I32)],
    )
    return pl.pallas_call(
        kernel,
        grid_spec=grid_spec,
        out_shape=jax.ShapeDtypeStruct((n_tiles * ts * CHUNKS, LANES), F32),
        compiler_params=_cparams(("arbitrary",)),
        name="experts",
    )(te, dest_flat, meta_flat, xn3d, w_gate, w_up, w_down)


def _combine_kernel(dest_ref, ys_hbm, h_ref, route_ref, y_ref, gbuf, sem, *, tm, n_steps, tok_off):
    i = pl.program_id(0)
    slot = i % 2
    nrow = 2 * tm

    def start(step, sl):
        base = (tok_off + step * tm) * 2
        _row_gather(ys_hbm, lambda r: dest_ref[base + r], gbuf.at[sl], sem.at[sl], nrow)

    @pl.when(i == 0)
    def _():
        start(0, 0)

    @pl.when(i + 1 < n_steps)
    def _():
        start(i + 1, 1 - slot)

    _row_wait(ys_hbm, gbuf.at[slot], sem.at[slot], nrow)
    buf = gbuf.at[slot]
    route = route_ref[...]
    w1 = route[:, 2:3]
    w2 = route[:, 3:4]
    for c in range(CHUNKS):
        g1 = buf[pl.ds(c, tm, stride=2 * CHUNKS), :]
        g2 = buf[pl.ds(CHUNKS + c, tm, stride=2 * CHUNKS), :]
        sl = slice(c * LANES, (c + 1) * LANES)
        y_ref[:, sl] = h_ref[:, sl] + w1 * g1 + w2 * g2


def _combine(dest_flat, ys3d, h_pool, route, tm, tok_off, n_tok):
    n_steps = n_tok // tm
    boff = tok_off // tm
    kernel = functools.partial(_combine_kernel, tm=tm, n_steps=n_steps, tok_off=tok_off)
    grid_spec = pltpu.PrefetchScalarGridSpec(
        num_scalar_prefetch=1,
        grid=(n_steps,),
        in_specs=[pl.BlockSpec(memory_space=pl.ANY),
                  pl.BlockSpec((tm, D_MODEL), lambda i, d: (i + boff, 0)),
                  pl.BlockSpec((tm, LANES), lambda i, d: (i + boff, 0))],
        out_specs=pl.BlockSpec((tm, D_MODEL), lambda i, d: (i, 0)),
        scratch_shapes=[pltpu.VMEM((2, 2 * tm * CHUNKS, LANES), F32), pltpu.SemaphoreType.DMA((2,))],
    )
    return pl.pallas_call(
        kernel,
        grid_spec=grid_spec,
        out_shape=jax.ShapeDtypeStruct((n_tok, D_MODEL), F32),
        compiler_params=_cparams(("arbitrary",)),
        name="combine",
    )(dest_flat, ys3d, h_pool, route)


def kernel(x_prompt, x_sample, cache_k, cache_v, state_ret, page_table, norm1, w_in, a_q_norm, a_k_norm,
           a_lambda_q1, a_lambda_k1, a_lambda_q2, a_lambda_k2, a_subln, r_norm, w_o, norm2,
           w_group_router, b_group_router, w_expert_router, b_expert_router, w_gate, w_up, w_down):
    depth = norm1.shape[0]
    assert depth == 1, "single-layer step"
    batch, seq, d = x_prompt.shape
    nsamp, tdec, _ = x_sample.shape
    assert d == D_MODEL and tdec == 1
    assert seq % TOKEN_TILE == 0 and seq % ATTN_TILE == 0 and seq % RET_CHUNK == 0
    assert nsamp % PLAN_TILE == 0 and nsamp % SAMPLE_BLOCK == 0
    past = page_table.shape[1] * cache_k.shape[2]
    n_prompt = batch * seq
    n_total = n_prompt + nsamp
    lam_init = 0.8 - 0.6 * math.exp(-0.3 * 0)
    l = 0

    lams = tuple(a[l].reshape(1, A_QK) for a in (a_lambda_q1, a_lambda_k1, a_lambda_q2, a_lambda_k2))
    w_in_bf = w_in[l].astype(BF16)
    w_o_bf = w_o[l].astype(BF16)
    g1 = norm1[l].reshape(1, D_MODEL)
    n2 = norm2[l].reshape(1, D_MODEL)
    qg = jnp.tile(a_q_norm[l], LANES // A_QK).reshape(1, LANES)
    kg = jnp.tile(a_k_norm[l], LANES // A_QK).reshape(1, LANES)
    asub = a_subln[l].reshape(1, A_ROW)
    rnorm = r_norm[l].reshape(1, R_V)
    ones = _segment_ones()
    w_r = jnp.concatenate([w_group_router[l], w_expert_router[l]], axis=1)
    w_r = jnp.pad(w_r, ((0, 0), (0, LANES - w_r.shape[1])))
    wr_hi = w_r.astype(BF16)
    wr_lo = (w_r - wr_hi.astype(F32)).astype(BF16)
    wr_cat = jnp.concatenate([wr_hi, wr_lo], axis=1)
    b_r = jnp.concatenate([b_group_router[l], b_expert_router[l]])
    b_r = jnp.pad(b_r, (0, LANES - b_r.shape[0])).reshape(1, LANES)

    pos_s = jnp.full((nsamp,), past, dtype=jnp.int32)
    xs = x_sample.reshape(nsamp, D_MODEL)
    (sq_bf, k_s, _, v_s, _, srq, srk, srv, srg) = _proj(
        xs, g1, w_in_bf, qg, kg, _rope_tables(pos_s), _ret_tables(pos_s), ones, PLAN_TILE)
    pos_p = jnp.arange(seq)
    xp = x_prompt.reshape(n_prompt, D_MODEL)
    (q_bf, k_p, k_bf, v_p, v_bf, rq, rk, rv, rg) = _proj(
        xp, g1, w_in_bf, qg, kg, _rope_tables(pos_p), _ret_tables(pos_p), ones, TOKEN_TILE)
    a_o, a_o_s = _attn(q_bf, k_bf, v_bf, lams, batch, seq, lam_init, page_table, sq_bf.astype(F32),
                       k_s, v_s, cache_k[l], cache_v[l])

    r_o_s, r_state_s = _ret_step(srq, srk, srv, state_ret[l].reshape(nsamp, R_QW, R_V))
    sample_rows = _mix_out(xs, a_o_s, r_o_s, srg, asub, rnorm, w_o_bf, n2, wr_cat, wr_hi, b_r,
                           lam_init, PLAN_TILE)

    r_o, r_state_p = _retention(rq, rk, rv, batch, seq)
    h_pool, xn_pool, route = _mix_out(xp, a_o, r_o, rg, asub, rnorm, w_o_bf, n2, wr_cat, wr_hi, b_r,
                                      lam_init, TOKEN_TILE, tail=sample_rows)
    n_pool = h_pool.shape[0]

    n_assign = 2 * n_total
    n_tiles = -(-(n_assign + N_EXPERTS * (SLOT_TILE - 1)) // SLOT_TILE)
    n_slots = n_tiles * SLOT_TILE
    dest128, te128, meta = _plan(route, n_total, n_tiles)
    dest_flat = dest128[:, :2].reshape(-1)
    te = te128[:n_tiles, 0]
    ys = _experts(te, dest_flat, meta.reshape(-1), xn_pool.reshape(n_pool, CHUNKS, LANES),
                  w_gate[l].reshape(N_EXPERTS, D_MODEL, D_EXPERT),
                  w_up[l].reshape(N_EXPERTS, D_MODEL, D_EXPERT),
                  w_down[l].reshape(N_EXPERTS, D_EXPERT, D_MODEL), n_tiles)
    ys3d = ys.reshape(n_slots, CHUNKS, LANES)
    y_p = _combine(dest_flat, ys3d, h_pool, route, TOKEN_TILE, 0, n_prompt)
    y_s = _combine(dest_flat, ys3d, h_pool, route, PLAN_TILE, n_prompt, nsamp)

    return (y_p.reshape(batch, seq, D_MODEL),
            y_s.reshape(nsamp, 1, D_MODEL),
            k_p.reshape(1, batch, seq, A_HEADS, A_ROW),
            v_p.reshape(1, batch, seq, A_HEADS, A_ROW),
            r_state_p.reshape(1, batch, R_HEADS, R_QK, R_V),
            k_s.reshape(1, nsamp, 1, A_HEADS, A_ROW),
            v_s.reshape(1, nsamp, 1, A_HEADS, A_ROW),
            r_state_s.reshape(1, nsamp, R_HEADS, R_QK, R_V))
```

```python
import functools
import math

import numpy as np
import jax
import jax.numpy as jnp
from jax import lax
from jax.experimental import pallas as pl
from jax.experimental.pallas import tpu as pltpu

F32 = jnp.float32
BF16 = jnp.bfloat16
I32 = jnp.int32

LANES = 128
SUBLANES = 8
CHUNKS = 8

D_MODEL = 1024
A_HEADS = 4
A_QK = 64
A_ROW = 2 * A_QK
A_WIDTH = A_HEADS * A_ROW
ROPE_THETA = 500000.0
ROPE_DIM = A_QK // 4
R_HEADS = 4
R_QK = 64
R_V = 128
R_QW = R_HEADS * R_QK
R_WIDTH = R_HEADS * R_V
R_THETA = 10000.0
N_GROUPS = 4
EXPERTS_PER_GROUP = 8
N_EXPERTS = N_GROUPS * EXPERTS_PER_GROUP
D_EXPERT = 512
EPS = 1e-6
NEG = -1e30

TOKEN_TILE = 256
ATTN_TILE = 512
ONES_ROWS = 16
LOG2E = 1.4426950408889634
RET_CHUNK = 256
PLAN_TILE = 128
PLAN_UNROLL = 3
SLOT_TILE = 256
SAMPLE_BLOCK = 8
VMEM_LIMIT = 56 * 1024 * 1024


def _cparams(sem, vmem=VMEM_LIMIT):
    return pltpu.CompilerParams(dimension_semantics=sem, vmem_limit_bytes=vmem)


def _rope_tables(pos):
    half = ROPE_DIM // 2
    inv = ROPE_THETA ** (-jnp.arange(half, dtype=F32) / half)
    ang = pos.astype(F32)[:, None] * inv[None, :]
    cos, sin = jnp.cos(ang), jnp.sin(ang)
    n = pos.shape[0]
    ones = jnp.ones((n, A_QK - ROPE_DIM), F32)
    zeros = jnp.zeros((n, A_QK - ROPE_DIM), F32)
    zh = jnp.zeros((n, half), F32)
    c = jnp.concatenate([cos, cos, ones], axis=1)
    s_next = jnp.concatenate([-sin, zh, zeros], axis=1)
    s_prev = jnp.concatenate([zh, sin, zeros], axis=1)
    rep = LANES // A_QK
    return tuple(jnp.tile(t, (1, rep)) for t in (c, s_next, s_prev))


def _ret_tables(pos):
    half = R_QK // 2
    inv = 1.0 / (R_THETA ** jnp.linspace(0.0, 1.0, half, dtype=F32))
    ang = pos.astype(F32)[:, None] * inv[None, :]
    cos, sin = jnp.cos(ang), jnp.sin(ang)
    z = jnp.zeros_like(sin)
    c = jnp.stack([cos, cos], axis=-1).reshape(-1, R_QK)
    s_next = jnp.stack([-sin, z], axis=-1).reshape(-1, R_QK)
    s_prev = jnp.stack([z, sin], axis=-1).reshape(-1, R_QK)
    rep = LANES // R_QK
    return tuple(jnp.tile(t, (1, rep)) for t in (c, s_next, s_prev))


def _segment_ones():
    seg = np.arange(LANES) // A_QK
    return jnp.asarray((seg[:, None] == seg[None, :]).astype(np.float32), dtype=BF16)


def _proj_kernel(x_ref, g1_ref, w_ref, qg_ref, kg_ref, rc_ref, rn_ref, rp_ref,
                 tc_ref, tn_ref, tp_ref, ones_ref,
                 qbf_ref, k_ref, kbf_ref, v_ref, vbf_ref, rq_ref, rk_ref, rv_ref, rg_ref):
    tm = x_ref.shape[0]
    x = x_ref[...]
    ms = jnp.mean(x * x, axis=-1, keepdims=True)
    n = (x * lax.rsqrt(ms + EPS) * g1_ref[...]).astype(BF16)
    h = jnp.dot(n, w_ref[...], preferred_element_type=F32)
    ones = ones_ref[...]
    rc, rn, rp = rc_ref[...], rn_ref[...], rp_ref[...]

    def head_norm_rope(xh, gain):
        sq = xh * xh
        hi = sq.astype(BF16)
        lo = (sq - hi.astype(F32)).astype(BF16)
        ssq = (jnp.dot(hi, ones, preferred_element_type=F32)
               + jnp.dot(lo, ones, preferred_element_type=F32))
        y = xh * lax.rsqrt(ssq * (1.0 / A_QK) + EPS) * gain
        half = ROPE_DIM // 2
        return (y * rc + pltpu.roll(y, LANES - half, 1) * rn + pltpu.roll(y, half, 1) * rp)

    for hd in range(A_HEADS):
        sl = slice(hd * A_ROW, (hd + 1) * A_ROW)
        q = head_norm_rope(h[:, sl], qg_ref[...])
        qbf_ref[:, sl] = (q * (A_QK ** -0.5 * LOG2E)).astype(BF16)
        k = head_norm_rope(h[:, A_WIDTH + hd * A_ROW:A_WIDTH + (hd + 1) * A_ROW], kg_ref[...])
        k_ref[pl.ds(hd, tm, stride=A_HEADS), :] = k
        kbf_ref[:, sl] = k.astype(BF16)
    o = 2 * A_WIDTH
    v = h[:, o:o + A_WIDTH]
    for hd in range(A_HEADS):
        v_ref[pl.ds(hd, tm, stride=A_HEADS), :] = v[:, hd * A_ROW:(hd + 1) * A_ROW]
    vbf_ref[...] = v.astype(BF16)
    o += A_WIDTH
    tc, tn, tp = tc_ref[...], tn_ref[...], tp_ref[...]

    def pair_rotate(xs):
        return xs * tc + pltpu.roll(xs, LANES - 1, 1) * tn + pltpu.roll(xs, 1, 1) * tp

    for j in range(R_QW // LANES):
        sl = slice(j * LANES, (j + 1) * LANES)
        rq_ref[:, sl] = pair_rotate(h[:, o + j * LANES:o + (j + 1) * LANES])
        rk_ref[:, sl] = pair_rotate(h[:, o + R_QW + j * LANES:o + R_QW + (j + 1) * LANES]) * (R_QK ** -0.5)
    o += 2 * R_QW
    rv_ref[...] = h[:, o:o + R_WIDTH]
    rg_ref[...] = h[:, o + R_WIDTH:o + 2 * R_WIDTH]


def _proj(x2d, g1, w_bf, qg, kg, rope_t, ret_t, ones, tm):
    t = x2d.shape[0]
    table_tiles = rope_t[0].shape[0] // tm
    row = lambda w: pl.BlockSpec((tm, w), lambda i: (i, 0))
    table = pl.BlockSpec((tm, LANES), lambda i: (i % table_tiles, 0))
    full = lambda a: pl.BlockSpec(a.shape, lambda i: (0,) * a.ndim)
    out_shape = (
        jax.ShapeDtypeStruct((t, A_WIDTH), BF16),
        jax.ShapeDtypeStruct((t * A_HEADS, A_ROW), F32),
        jax.ShapeDtypeStruct((t, A_WIDTH), BF16),
        jax.ShapeDtypeStruct((t * A_HEADS, A_ROW), F32),
        jax.ShapeDtypeStruct((t, A_WIDTH), BF16),
        jax.ShapeDtypeStruct((t, R_QW), F32),
        jax.ShapeDtypeStruct((t, R_QW), F32),
        jax.ShapeDtypeStruct((t, R_WIDTH), F32),
        jax.ShapeDtypeStruct((t, R_WIDTH), F32),
    )
    return pl.pallas_call(
        _proj_kernel,
        grid=(t // tm,),
        in_specs=[row(D_MODEL), full(g1), full(w_bf), full(qg), full(kg)]
                 + [table] * 6 + [full(ones)],
        out_specs=tuple(pl.BlockSpec((tm * s.shape[0] // t, s.shape[1]), lambda i: (i, 0)) for s in out_shape),
        out_shape=out_shape,
        compiler_params=_cparams(("parallel",)),
        name="proj",
    )(x2d, g1, w_bf, qg, kg, *rope_t, *ret_t, ones)


def _lambda(lq1_ref, lk1_ref, lq2_ref, lk2_ref, lam_init):
    s1 = jnp.sum(lq1_ref[...] * lk1_ref[...], axis=-1, keepdims=True)
    s2 = jnp.sum(lq2_ref[...] * lk2_ref[...], axis=-1, keepdims=True)
    return jnp.exp(s1) - jnp.exp(s2) + lam_init


def _attn_kernel(pt_ref, q_ref, k_ref, v_ref, lq1_ref, lk1_ref, lq2_ref, lk2_ref,
                 sq_ref, skn_ref, svn_ref, bias_ref, nbias_ref, *rest, tile, lam_init, npages):
    k_pages = rest[:npages]
    v_pages = rest[npages:2 * npages]
    o_ref, os_ref, m_ref, l_ref, acc_ref, qq_ref, sa_ref, sb_ref = rest[2 * npages:]
    del pt_ref
    lam = _lambda(lq1_ref, lk1_ref, lq2_ref, lk2_ref, lam_init)
    _paged_step(sq_ref, skn_ref, svn_ref, bias_ref, nbias_ref, lam, k_pages, v_pages, os_ref)

    i = pl.program_id(2)
    q = q_ref[...]
    lane = lax.broadcasted_iota(I32, q.shape, 1)
    zero = jnp.zeros_like(q)
    qq = jnp.concatenate([jnp.where(lane < A_QK, q, zero), jnp.where(lane >= A_QK, q, zero)], axis=0)
    qq_ref[...] = qq
    m_ref[...] = jnp.full(m_ref.shape, NEG, F32)
    l_ref[...] = jnp.zeros(l_ref.shape, F32)
    acc_ref[...] = jnp.zeros(acc_ref.shape, F32)

    ones_rows = jnp.ones((ONES_ROWS, tile), BF16)

    def scores(j, s_ref):
        k = k_ref[pl.ds(pl.multiple_of(j * tile, tile), tile), :]
        s_ref[...] = lax.dot_general(k, qq_ref[...], (((1,), (1,)), ((), ())), preferred_element_type=F32)

    def accumulate(j, s_ref, masked):
        v = v_ref[pl.ds(pl.multiple_of(j * tile, tile), tile), :]
        vt = jnp.concatenate([v.T, ones_rows], axis=0)
        s = s_ref[...]
        if masked:
            key = lax.broadcasted_iota(I32, s.shape, 0)
            qry = lax.broadcasted_iota(I32, s.shape, 1)
            qry = jnp.where(qry >= tile, qry - tile, qry)
            s = jnp.where(key <= qry, s, NEG)
        m_prev = m_ref[...]
        m_new = jnp.maximum(m_prev, jnp.max(s, axis=0, keepdims=True))
        alpha = jnp.exp2(m_prev - m_new)
        p = jnp.exp2(s - m_new).astype(BF16)
        pv = jnp.dot(vt, p, preferred_element_type=F32)
        acc_ref[...] = alpha * acc_ref[...] + pv[:A_ROW, :]
        l_ref[...] = alpha * l_ref[...] + pv[A_ROW:A_ROW + 1, :]
        m_ref[...] = m_new

    scores(0, sa_ref)

    def pair(t, carry):
        j = 2 * t
        scores(j + 1, sb_ref)
        accumulate(j, sa_ref, False)
        scores(j + 2, sa_ref)
        accumulate(j + 1, sb_ref, False)
        return carry

    lax.fori_loop(0, i // 2, pair, 0)

    @pl.when(i % 2 == 0)
    def _():
        accumulate(i, sa_ref, True)

    @pl.when(i % 2 == 1)
    def _():
        scores(i, sb_ref)
        accumulate(i - 1, sa_ref, False)
        accumulate(i, sb_ref, True)

    o1 = acc_ref[:, :tile] / l_ref[:, :tile]
    o2 = acc_ref[:, tile:] / l_ref[:, tile:]
    o_ref[...] = (o1 - lam * o2).T


def _attn(q_bf, k_bf, v_bf, lams, batch, seq, lam_init, page_table, sq, k_new, v_new, cache_k, cache_v):
    tile = ATTN_TILE
    nq = seq // tile
    nseq, npages = page_table.shape
    assert nseq == batch * A_HEADS * nq, "one decode sequence per prompt query block"
    n_phys, page = cache_k.shape[0], cache_k.shape[1]
    prow = page * A_HEADS
    ck = cache_k.reshape(n_phys, prow, A_ROW)
    cv = cache_v.reshape(n_phys, prow, A_ROW)
    r = np.arange(2 * A_HEADS)[:, None] % A_HEADS
    bias = jnp.asarray(np.where(np.arange(prow)[None, :] % A_HEADS == r, 0.0, NEG).astype(np.float32))
    nbias = jnp.asarray(np.where(np.arange(LANES)[None, :] == r, 0.0, NEG).astype(np.float32))
    step = lambda b, h, i: (b * A_HEADS + h) * nq + i
    lam_spec = pl.BlockSpec((1, A_QK), lambda b, h, i, pt: (0, 0))
    tok = pl.BlockSpec((None, 1, A_WIDTH), lambda b, h, i, pt: (step(b, h, i), 0, 0))
    full = lambda a: pl.BlockSpec(a.shape, lambda b, h, i, pt: (0,) * a.ndim)

    def page_spec(jj):
        return pl.BlockSpec((None, prow, A_ROW), lambda b, h, i, pt: (pt[step(b, h, i) * npages + jj], 0, 0))

    kernel = functools.partial(_attn_kernel, tile=tile, lam_init=lam_init, npages=npages)
    grid_spec = pltpu.PrefetchScalarGridSpec(
        num_scalar_prefetch=1,
        grid=(batch, A_HEADS, nq),
        in_specs=[pl.BlockSpec((tile, A_ROW), lambda b, h, i, pt: (b * nq + i, h)),
                  pl.BlockSpec((seq, A_ROW), lambda b, h, i, pt: (b, h)),
                  pl.BlockSpec((seq, A_ROW), lambda b, h, i, pt: (b, h))] + [lam_spec] * 4
                 + [tok, tok, tok, full(bias), full(nbias)]
                 + [page_spec(jj) for jj in range(npages)] * 2,
        out_specs=(pl.BlockSpec((tile, A_ROW), lambda b, h, i, pt: (b * nq + i, h)), tok),
        scratch_shapes=[pltpu.VMEM((1, 2 * tile), F32), pltpu.VMEM((1, 2 * tile), F32),
                        pltpu.VMEM((A_ROW, 2 * tile), F32), pltpu.VMEM((2 * tile, A_ROW), BF16),
                        pltpu.VMEM((tile, 2 * tile), F32), pltpu.VMEM((tile, 2 * tile), F32)],
    )
    a_o, a_o_s = pl.pallas_call(
        kernel,
        grid_spec=grid_spec,
        out_shape=(jax.ShapeDtypeStruct((batch * seq, A_WIDTH), F32),
                   jax.ShapeDtypeStruct((nseq, 1, A_WIDTH), F32)),
        compiler_params=_cparams(("parallel", "parallel", "arbitrary")),
        name="attn",
    )(page_table.reshape(-1), q_bf, k_bf, v_bf, *lams,
      sq.reshape(nseq, 1, A_WIDTH), k_new.reshape(nseq, 1, A_WIDTH), v_new.reshape(nseq, 1, A_WIDTH),
      bias, nbias, *([ck] * npages), *([cv] * npages))
    return a_o, a_o_s.reshape(nseq, A_WIDTH)


def _ret_decay():
    return [math.log(1.0 - 2.0 ** (-5.0 - h)) for h in range(R_HEADS)]


def _ret_tables_chunk(chunk):
    log_g = jnp.log(1.0 - 2.0 ** (-5.0 - jnp.arange(R_HEADS, dtype=F32)))
    idx = jnp.arange(chunk, dtype=F32)
    diff = idx[:, None] - idx[None, :]
    dmask = jnp.where(diff >= 0, jnp.exp(jnp.maximum(diff, 0.0)[None] * log_g[:, None, None]), 0.0)
    q_dec = jnp.exp((idx + 1.0)[:, None] * log_g[None, :])
    k_dec = jnp.exp((chunk - 1.0 - idx)[:, None] * log_g[None, :])
    q_dec = jnp.repeat(q_dec, R_QK, axis=1)
    k_dec = jnp.repeat(k_dec, R_QK, axis=1)
    g_chunk = jnp.exp(chunk * log_g)
    g_rows = jnp.broadcast_to(jnp.repeat(g_chunk, R_QK)[:, None], (R_QW, R_V))
    return dmask, q_dec, k_dec, g_rows


def _ret_kernel(q_ref, k_ref, v_ref, dmask_ref, qdec_ref, kdec_ref, grow_ref,
                o_ref, st_ref, state_ref):
    c = pl.program_id(1)

    @pl.when(c == 0)
    def _():
        state_ref[...] = jnp.zeros(state_ref.shape, F32)

    q = q_ref[...]
    k = k_ref[...]
    qd = (q * qdec_ref[...]).astype(BF16)
    kd = (k * kdec_ref[...]).astype(BF16)
    qb = q.astype(BF16)
    kb = k.astype(BF16)
    vb = v_ref[...].astype(BF16)
    for h in range(R_HEADS):
        ks = slice(h * R_QK, (h + 1) * R_QK)
        vs = slice(h * R_V, (h + 1) * R_V)
        state = state_ref[ks, :]
        inner = lax.dot_general(qb[:, ks], kb[:, ks], (((1,), (1,)), ((), ())),
                                preferred_element_type=F32) * dmask_ref[h]
        o = (jnp.dot(inner.astype(BF16), vb[:, vs], preferred_element_type=F32)
             + jnp.dot(qd[:, ks], state.astype(BF16), preferred_element_type=F32))
        o_ref[:, vs] = o
        upd = lax.dot_general(kd[:, ks], vb[:, vs], (((0,), (0,)), ((), ())),
                              preferred_element_type=F32)
        state_ref[ks, :] = grow_ref[ks, :] * state + upd
    st_ref[...] = state_ref[...]


def _retention(rq, rk, rv, batch, seq):
    chunk = RET_CHUNK
    nc = seq // chunk
    dmask, q_dec, k_dec, g_rows = _ret_tables_chunk(chunk)
    row = lambda w: pl.BlockSpec((chunk, w), lambda b, c: (b * nc + c, 0))
    full = lambda a: pl.BlockSpec(a.shape, lambda b, c: (0,) * a.ndim)
    return pl.pallas_call(
        _ret_kernel,
        grid=(batch, nc),
        in_specs=[row(R_QW), row(R_QW), row(R_WIDTH), full(dmask), full(q_dec), full(k_dec), full(g_rows)],
        out_specs=(row(R_WIDTH), pl.BlockSpec((None, R_QW, R_V), lambda b, c: (b, 0, 0))),
        out_shape=(jax.ShapeDtypeStruct((batch * seq, R_WIDTH), F32),
                   jax.ShapeDtypeStruct((batch, R_QW, R_V), F32)),
        scratch_shapes=[pltpu.VMEM((R_QW, R_V), F32)],
        compiler_params=_cparams(("parallel", "arbitrary")),
        name="retention",
    )(rq, rk, rv, dmask, q_dec, k_dec, g_rows)


def _ret_step_kernel(q_ref, k_ref, v_ref, state_ref, grow_ref, o_ref, ns_ref):
    g = grow_ref[...]
    for t in range(SAMPLE_BLOCK):
        kcol = jnp.broadcast_to(k_ref[t:t + 1, :], (LANES, R_QW)).T
        qcol = jnp.broadcast_to(q_ref[t:t + 1, :], (LANES, R_QW)).T
        vrows = jnp.concatenate(
            [jnp.broadcast_to(v_ref[t:t + 1, h * R_V:(h + 1) * R_V], (R_QK, R_V)) for h in range(R_HEADS)],
            axis=0)
        new = g * state_ref[t] + kcol * vrows
        ns_ref[t] = new
        qn = qcol * new
        for h in range(R_HEADS):
            o_ref[t:t + 1, h * R_V:(h + 1) * R_V] = jnp.sum(qn[h * R_QK:(h + 1) * R_QK, :], axis=0, keepdims=True)


def _ret_step(rq, rk, rv, state):
    n = rq.shape[0]
    bb = SAMPLE_BLOCK
    log_g = jnp.log(1.0 - 2.0 ** (-5.0 - jnp.arange(R_HEADS, dtype=F32)))
    g_rows = jnp.broadcast_to(jnp.repeat(jnp.exp(log_g), R_QK)[:, None], (R_QW, R_V))
    row = lambda w: pl.BlockSpec((bb, w), lambda i: (i, 0))
    st = pl.BlockSpec((bb, R_QW, R_V), lambda i: (i, 0, 0))
    return pl.pallas_call(
        _ret_step_kernel,
        grid=(n // bb,),
        in_specs=[row(R_QW), row(R_QW), row(R_WIDTH), st, pl.BlockSpec((R_QW, R_V), lambda i: (0, 0))],
        out_specs=(row(R_WIDTH), st),
        out_shape=(jax.ShapeDtypeStruct((n, R_WIDTH), F32), jax.ShapeDtypeStruct((n, R_QW, R_V), F32)),
        compiler_params=_cparams(("parallel",)),
        name="ret_step",
    )(rq, rk, rv, state, g_rows)


def _paged_step(q_ref, kn_ref, vn_ref, bias_ref, nbias_ref, lam, k_refs, v_refs, o_ref):
    npages = len(k_refs)
    q = q_ref[...]
    nrow = 2 * A_HEADS

    def head_rows(x, n):
        row = lax.broadcasted_iota(I32, (n, A_ROW), 0)
        out = jnp.zeros((n, A_ROW), F32)
        for h in range(A_HEADS):
            out = jnp.where(row == h, jnp.broadcast_to(x[:, h * A_ROW:(h + 1) * A_ROW], (n, A_ROW)), out)
        return out

    row8 = lax.broadcasted_iota(I32, (nrow, A_ROW), 0)
    lane8 = lax.broadcasted_iota(I32, (nrow, A_ROW), 1)
    q4 = head_rows(q, nrow)
    q8 = q4 + pltpu.roll(q4, A_HEADS, 0)
    qm = jnp.where((row8 < A_HEADS) == (lane8 < A_QK), q8, 0.0).astype(BF16)

    nt = (((1,), (1,)), ((), ()))
    bias = bias_ref[...]
    s = [lax.dot_general(qm, k_refs[j][...].astype(BF16), nt, preferred_element_type=F32) + bias
         for j in range(npages)]
    kn = head_rows(kn_ref[...], LANES).astype(BF16)
    s.append(lax.dot_general(qm, kn, nt, preferred_element_type=F32) + nbias_ref[...])
    m = functools.reduce(jnp.maximum, [jnp.max(x, axis=-1, keepdims=True) for x in s])
    p = [jnp.exp2(x - m) for x in s]
    l = functools.reduce(lambda a, b: a + b, [jnp.sum(x, axis=-1, keepdims=True) for x in p])
    inv = 1.0 / l
    vs = [v_refs[j][...].astype(BF16) for j in range(npages)] + [head_rows(vn_ref[...], LANES).astype(BF16)]
    out = jnp.zeros((nrow, A_ROW), F32)
    for pj, vj in zip(p, vs):
        pn = pj * inv
        first = lax.broadcasted_iota(I32, pn.shape, 0) < A_HEADS
        w8 = jnp.where(first, pn - lam * pltpu.roll(pn, A_HEADS, 0), 0.0).astype(BF16)
        out = out + jnp.dot(w8, vj, preferred_element_type=F32)
    for h in range(A_HEADS):
        o_ref[:, h * A_ROW:(h + 1) * A_ROW] = out[h:h + 1, :]


def _mix_tile(x_ref, a_ref, r_ref, g_ref, asub_ref, rnorm_ref, wo_ref, n2_ref,
              wrc_ref, wrh_ref, br_ref, h_ref, xn_ref, route_ref, *, lam_init):
    tm = x_ref.shape[0]
    parts = []
    for hd in range(A_HEADS):
        a = a_ref[:, hd * A_ROW:(hd + 1) * A_ROW]
        ms = jnp.mean(a * a, axis=-1, keepdims=True)
        parts.append((a * lax.rsqrt(ms + EPS) * asub_ref[...] * (1.0 - lam_init)).astype(BF16))
    for hd in range(R_HEADS):
        sl = slice(hd * R_V, (hd + 1) * R_V)
        r = r_ref[:, sl]
        ms = jnp.mean(r * r, axis=-1, keepdims=True)
        gate = g_ref[:, sl]
        gate = gate * (1.0 / (1.0 + jnp.exp(-gate)))
        parts.append((r * lax.rsqrt(ms + EPS) * rnorm_ref[...] * gate).astype(BF16))
    merged = jnp.concatenate(parts, axis=1)
    h = x_ref[...] + jnp.dot(merged, wo_ref[...], preferred_element_type=F32)
    h_ref[...] = h
    ms = jnp.mean(h * h, axis=-1, keepdims=True)
    xn = h * lax.rsqrt(ms + EPS) * n2_ref[...]
    for c in range(CHUNKS):
        xn_ref[pl.ds(c, tm, stride=CHUNKS), :] = xn[:, c * LANES:(c + 1) * LANES]
    xh = xn.astype(BF16)
    xl = (xn - xh.astype(F32)).astype(BF16)
    both = jnp.dot(xh, wrc_ref[...], preferred_element_type=F32)
    logits = (both[:, :LANES] + jnp.dot(xl, wrh_ref[...], preferred_element_type=F32)
              + both[:, LANES:]) + br_ref[...]
    lane = lax.broadcasted_iota(I32, logits.shape, 1).astype(F32)
    big = float(LANES)
    gl = jnp.where(lane < N_GROUPS, logits, NEG)
    gmax = jnp.max(gl, axis=-1, keepdims=True)
    gidx = jnp.min(jnp.where(gl == gmax, lane, big), axis=-1, keepdims=True)
    gsum = jnp.sum(jnp.where(lane < N_GROUPS, jnp.exp(gl - gmax), 0.0), axis=-1, keepdims=True)
    gprob = 1.0 / gsum
    lo = N_GROUPS + EXPERTS_PER_GROUP * gidx
    el = jnp.where((lane >= lo) & (lane < lo + EXPERTS_PER_GROUP), logits, NEG)
    v1 = jnp.max(el, axis=-1, keepdims=True)
    i1 = jnp.min(jnp.where(el == v1, lane, big), axis=-1, keepdims=True)
    el2 = jnp.where(lane == i1, NEG, el)
    v2 = jnp.max(el2, axis=-1, keepdims=True)
    i2 = jnp.min(jnp.where(el2 == v2, lane, big), axis=-1, keepdims=True)
    e = jnp.exp(v2 - v1)
    w1 = gprob / (1.0 + e)
    w2 = gprob * e / (1.0 + e)
    e1 = i1 - N_GROUPS
    e2 = i2 - N_GROUPS
    route_ref[...] = jnp.where(lane == 0, e1, jnp.where(lane == 1, e2, jnp.where(
        lane == 2, w1, jnp.where(lane == 3, w2, 0.0))))


def _mix_kernel(*refs, lam_init, n_tiles, has_tail):
    if not has_tail:
        _mix_tile(*refs, lam_init=lam_init)
        return
    ins, (th_ref, txn_ref, troute_ref), outs = refs[:11], refs[11:14], refs[14:]
    h_ref, xn_ref, route_ref = outs
    i = pl.program_id(0)

    @pl.when(i < n_tiles)
    def _():
        _mix_tile(*ins, *outs, lam_init=lam_init)

    @pl.when(i == n_tiles)
    def _():
        tm = h_ref.shape[0]
        nt = th_ref.shape[0]
        h_ref[:nt, :] = th_ref[...]
        h_ref[nt:, :] = jnp.zeros((tm - nt, D_MODEL), F32)
        xn_ref[:nt * CHUNKS, :] = txn_ref[...]
        xn_ref[nt * CHUNKS:, :] = jnp.zeros(((tm - nt) * CHUNKS, LANES), F32)
        route_ref[:nt, :] = troute_ref[...]
        route_ref[nt:, :] = jnp.zeros((tm - nt, LANES), F32)


def _mix_out(x2d, a_o, r_o, rg, asub, rnorm, wo_bf, n2, wr_cat, wr_hi, br, lam_init, tm, tail=None):
    t = x2d.shape[0]
    n_tiles = t // tm
    has_tail = tail is not None
    n_out = n_tiles + (1 if has_tail else 0)
    row = lambda w: pl.BlockSpec((tm, w), lambda i: (jnp.minimum(i, n_tiles - 1), 0))
    full = lambda a: pl.BlockSpec(a.shape, lambda i: (0,) * a.ndim)
    out_shape = (jax.ShapeDtypeStruct((n_out * tm, D_MODEL), F32),
                 jax.ShapeDtypeStruct((n_out * tm * CHUNKS, LANES), F32),
                 jax.ShapeDtypeStruct((n_out * tm, LANES), F32))
    out_specs = (pl.BlockSpec((tm, D_MODEL), lambda i: (i, 0)),
                 pl.BlockSpec((tm * CHUNKS, LANES), lambda i: (i, 0)),
                 pl.BlockSpec((tm, LANES), lambda i: (i, 0)))
    in_specs = [row(D_MODEL), row(A_WIDTH), row(R_WIDTH), row(R_WIDTH), full(asub), full(rnorm),
                full(wo_bf), full(n2), full(wr_cat), full(wr_hi), full(br)]
    args = [x2d, a_o, r_o, rg, asub, rnorm, wo_bf, n2, wr_cat, wr_hi, br]
    if has_tail:
        assert tail[0].shape[0] <= tm
        in_specs += [full(a) for a in tail]
        args += list(tail)
    kernel = functools.partial(_mix_kernel, lam_init=lam_init, n_tiles=n_tiles, has_tail=has_tail)
    return pl.pallas_call(
        kernel,
        grid=(n_out,),
        in_specs=in_specs,
        out_specs=out_specs,
        out_shape=out_shape,
        compiler_params=_cparams(("arbitrary",)),
        name="mix_out",
    )(*args)


def _plan_kernel(route_ref, ltri_ref, utri_ref, dest_ref, te_ref, meta_ref, *, n, slot_tile):
    tile = PLAN_TILE
    lane = lax.broadcasted_iota(I32, (tile, LANES), 1)

    def block(b):
        r = route_ref[pl.ds(pl.multiple_of(b * tile, tile), tile), :]
        e1 = r[:, 0:1].astype(I32)
        e2 = r[:, 1:2].astype(I32)
        return jnp.where((lane == e1) | (lane == e2), 1.0, 0.0), e1, e2

    def count(b, c):
        return c + jnp.sum(block(b)[0], axis=0, keepdims=True)

    cnt = lax.fori_loop(0, n // tile, count, jnp.zeros((1, LANES), F32), unroll=PLAN_UNROLL)
    ntile = jnp.floor((cnt + (slot_tile - 1)) * (1.0 / slot_tile))
    nt8 = jnp.broadcast_to(ntile, (SUBLANES, LANES)).astype(BF16)
    base_t = jnp.dot(nt8, utri_ref[...], preferred_element_type=F32)[0:1, :]
    base = base_t * slot_tile
    ends = base_t + ntile
    tl = lax.broadcasted_iota(I32, te_ref.shape, 0).astype(F32)
    el = lax.broadcasted_iota(I32, te_ref.shape, 1)
    hit = jnp.where((el < N_EXPERTS) & (ends <= tl), 1.0, 0.0)
    te_ref[...] = jnp.broadcast_to(jnp.sum(hit, axis=-1, keepdims=True), te_ref.shape).astype(I32)
    mrow = lax.broadcasted_iota(I32, meta_ref.shape, 0)
    meta_ref[...] = jnp.where(mrow == 0, cnt, jnp.where(mrow == 1, base, jnp.where(
        mrow == 2, ntile * slot_tile, 0.0))).astype(I32)

    def place(b, run):
        onehot, e1, e2 = block(b)
        rank = jnp.dot(ltri_ref[...], onehot.astype(BF16), preferred_element_type=F32)
        pos = base + run + rank
        d1 = jnp.sum(jnp.where(lane == e1, pos, 0.0), axis=-1, keepdims=True)
        d2 = jnp.sum(jnp.where(lane == e2, pos, 0.0), axis=-1, keepdims=True)
        dest_ref[pl.ds(pl.multiple_of(b * tile, tile), tile), :] = jnp.where(
            lane == 0, d1, jnp.where(lane == 1, d2, 0.0)).astype(I32)
        return run + jnp.sum(onehot, axis=0, keepdims=True)

    lax.fori_loop(0, n // tile, place, jnp.zeros((1, LANES), F32), unroll=PLAN_UNROLL)


def _plan(route, n, n_slot_tiles):
    tile = PLAN_TILE
    te_rows = -(-n_slot_tiles // SUBLANES) * SUBLANES
    ii = np.arange(tile)
    ltri = jnp.asarray((ii[None, :] < ii[:, None]).astype(np.float32), dtype=BF16)
    ee = np.arange(LANES)
    utri = jnp.asarray((ee[:, None] < ee[None, :]).astype(np.float32), dtype=BF16)
    kernel = functools.partial(_plan_kernel, n=n, slot_tile=SLOT_TILE)
    return pl.pallas_call(
        kernel,
        out_shape=(jax.ShapeDtypeStruct((n, LANES), I32),
                   jax.ShapeDtypeStruct((te_rows, LANES), I32),
                   jax.ShapeDtypeStruct((SUBLANES, LANES), I32)),
        compiler_params=pltpu.CompilerParams(vmem_limit_bytes=VMEM_LIMIT),
        name="plan",
    )(route, ltri, utri)


def _total_tiles(meta_ref):
    last = N_EXPERTS - 1
    return (meta_ref[LANES + last] + meta_ref[2 * LANES + last]) // SLOT_TILE


def _dispatch_kernel(dest_ref, meta_ref, xn_ref, xs_hbm, zeros_ref, zsem, ssem, *, tm, n_steps, n_total, n_tiles):
    i = pl.program_id(0)
    ts = SLOT_TILE

    def zero_tiles(start):
        def one(tile):
            cp = pltpu.make_async_copy(zeros_ref, xs_hbm.at[pl.ds(tile * ts, ts)], zsem)
            if start:
                cp.start()
            else:
                cp.wait()

        def per_expert(e, c):
            first = meta_ref[LANES + e]
            reserved = meta_ref[2 * LANES + e]

            @pl.when(meta_ref[e] < reserved)
            def _():
                one((first + reserved) // ts - 1)

            return c

        lax.fori_loop(0, N_EXPERTS, per_expert, 0)

        def trailing(t, c):
            one(t)
            return c

        lax.fori_loop(_total_tiles(meta_ref), n_tiles, trailing, 0)

    @pl.when(i == 0)
    def _():
        zeros_ref[...] = jnp.zeros(zeros_ref.shape, F32)
        zero_tiles(True)
        zero_tiles(False)

    def scatter(n_tok):
        base = 2 * i * tm

        def start(r, c):
            src = xn_ref.at[pl.ds(pl.multiple_of(r * CHUNKS, CHUNKS), CHUNKS), :]
            for k in range(2):
                pltpu.make_async_copy(src, xs_hbm.at[dest_ref[base + 2 * r + k]], ssem).start()
            return c

        lax.fori_loop(0, n_tok, start, 0, unroll=8)

        def wait(r, c):
            pltpu.make_async_copy(xn_ref.at[pl.ds(0, CHUNKS), :], xs_hbm.at[0], ssem).wait()
            return c

        lax.fori_loop(0, 2 * n_tok, wait, 0, unroll=8)

    rem = n_total - (n_steps - 1) * tm
    if rem == tm:
        scatter(tm)
    else:
        pl.when(i < n_steps - 1)(functools.partial(scatter, tm))
        pl.when(i == n_steps - 1)(functools.partial(scatter, rem))


def _dispatch(dest_flat, meta_flat, xn_pool, n_total, n_tiles):
    tm = TOKEN_TILE
    n_steps = -(-n_total // tm)
    assert xn_pool.shape[0] >= n_steps * tm * CHUNKS
    kernel = functools.partial(_dispatch_kernel, tm=tm, n_steps=n_steps, n_total=n_total, n_tiles=n_tiles)
    grid_spec = pltpu.PrefetchScalarGridSpec(
        num_scalar_prefetch=2,
        grid=(n_steps,),
        in_specs=[pl.BlockSpec((tm * CHUNKS, LANES), lambda i, d, m: (i, 0))],
        out_specs=pl.BlockSpec(memory_space=pl.ANY),
        scratch_shapes=[pltpu.VMEM((SLOT_TILE, CHUNKS, LANES), F32),
                        pltpu.SemaphoreType.DMA, pltpu.SemaphoreType.DMA],
    )
    return pl.pallas_call(
        kernel,
        grid_spec=grid_spec,
        out_shape=jax.ShapeDtypeStruct((n_tiles * SLOT_TILE, CHUNKS, LANES), F32),
        compiler_params=_cparams(("arbitrary",)),
        name="dispatch",
    )(dest_flat, meta_flat, xn_pool)


def _row_gather(src_hbm, idx_of_row, buf, sem, n_rows):
    def body(r, c):
        pltpu.make_async_copy(src_hbm.at[idx_of_row(r)],
                              buf.at[pl.ds(pl.multiple_of(r * CHUNKS, CHUNKS), CHUNKS), :], sem).start()
        return c

    lax.fori_loop(0, n_rows, body, 0, unroll=8)


def _row_wait(src_hbm, buf, sem, n_rows):
    def body(r, c):
        pltpu.make_async_copy(src_hbm.at[0], buf.at[pl.ds(0, CHUNKS), :], sem).wait()
        return c

    lax.fori_loop(0, n_rows, body, 0, unroll=8)


def _gathered_rows(buf, n_rows):
    return jnp.concatenate([buf[pl.ds(c, n_rows, stride=CHUNKS), :] for c in range(CHUNKS)], axis=1)


def _expert_kernel(te_ref, meta_ref, xs_ref, wg_ref, wu_ref, wd_ref, ys_ref, wg_bf, wu_bf, wd_bf):
    i = pl.program_id(0)
    ts = SLOT_TILE
    e = te_ref[i]
    active = e < N_EXPERTS
    prev = te_ref[jnp.maximum(i - 1, 0)]

    @pl.when(active & ((i == 0) | (e != prev)))
    def _():
        wg_bf[...] = wg_ref[...].astype(BF16)
        wu_bf[...] = wu_ref[...].astype(BF16)
        wd_bf[...] = wd_ref[...].astype(BF16)

    @pl.when(active)
    def _():
        x = _gathered_rows(xs_ref, ts).astype(BF16)
        a = jnp.dot(x, wg_bf[...], preferred_element_type=F32)
        u = jnp.dot(x, wu_bf[...], preferred_element_type=F32)
        hmid = (a * (1.0 / (1.0 + jnp.exp(-a))) * u).astype(BF16)
        y = jnp.dot(hmid, wd_bf[...], preferred_element_type=F32)
        for c in range(CHUNKS):
            ys_ref[pl.ds(c, ts, stride=CHUNKS), :] = y[:, c * LANES:(c + 1) * LANES]

    @pl.when(jnp.logical_not(active))
    def _():
        ys_ref[...] = jnp.zeros(ys_ref.shape, F32)


def _experts(te, meta_flat, xs, w_gate, w_up, w_down, n_tiles):
    ts = SLOT_TILE
    wsel = lambda i, te, meta: (jnp.minimum(te[i], N_EXPERTS - 1), 0, 0)
    xsel = lambda i, te, meta: (jnp.minimum(i, _total_tiles(meta) - 1), 0)
    grid_spec = pltpu.PrefetchScalarGridSpec(
        num_scalar_prefetch=2,
        grid=(n_tiles,),
        in_specs=[pl.BlockSpec((ts * CHUNKS, LANES), xsel),
                  pl.BlockSpec((None, D_MODEL, D_EXPERT), wsel),
                  pl.BlockSpec((None, D_MODEL, D_EXPERT), wsel),
                  pl.BlockSpec((None, D_EXPERT, D_MODEL), wsel)],
        out_specs=pl.BlockSpec((ts * CHUNKS, LANES), lambda i, te, meta: (i, 0)),
        scratch_shapes=[pltpu.VMEM((D_MODEL, D_EXPERT), BF16),
                        pltpu.VMEM((D_MODEL, D_EXPERT), BF16),
                        pltpu.VMEM((D_EXPERT, D_MODEL), BF16)],
    )
    return pl.pallas_call(
        _expert_kernel,
        grid_spec=grid_spec,
        out_shape=jax.ShapeDtypeStruct((n_tiles * ts * CHUNKS, LANES), F32),
        compiler_params=_cparams(("arbitrary",)),
        name="experts",
    )(te, meta_flat, xs.reshape(n_tiles * ts * CHUNKS, LANES), w_gate, w_up, w_down)


def _combine_kernel(dest_ref, ys_hbm, h_ref, route_ref, y_ref, gbuf, sem, *, tm, n_steps, tok_off):
    i = pl.program_id(0)
    slot = i % 2
    nrow = 2 * tm

    def start(step, sl):
        base = (tok_off + step * tm) * 2
        _row_gather(ys_hbm, lambda r: dest_ref[base + r], gbuf.at[sl], sem.at[sl], nrow)

    @pl.when(i == 0)
    def _():
        start(0, 0)

    @pl.when(i + 1 < n_steps)
    def _():
        start(i + 1, 1 - slot)

    _row_wait(ys_hbm, gbuf.at[slot], sem.at[slot], nrow)
    buf = gbuf.at[slot]
    route = route_ref[...]
    w1 = route[:, 2:3]
    w2 = route[:, 3:4]
    for c in range(CHUNKS):
        g1 = buf[pl.ds(c, tm, stride=2 * CHUNKS), :]
        g2 = buf[pl.ds(CHUNKS + c, tm, stride=2 * CHUNKS), :]
        sl = slice(c * LANES, (c + 1) * LANES)
        y_ref[:, sl] = h_ref[:, sl] + w1 * g1 + w2 * g2


def _combine(dest_flat, ys3d, h_pool, route, tm, tok_off, n_tok):
    n_steps = n_tok // tm
    boff = tok_off // tm
    kernel = functools.partial(_combine_kernel, tm=tm, n_steps=n_steps, tok_off=tok_off)
    grid_spec = pltpu.PrefetchScalarGridSpec(
        num_scalar_prefetch=1,
        grid=(n_steps,),
        in_specs=[pl.BlockSpec(memory_space=pl.ANY),
                  pl.BlockSpec((tm, D_MODEL), lambda i, d: (i + boff, 0)),
                  pl.BlockSpec((tm, LANES), lambda i, d: (i + boff, 0))],
        out_specs=pl.BlockSpec((tm, D_MODEL), lambda i, d: (i, 0)),
        scratch_shapes=[pltpu.VMEM((2, 2 * tm * CHUNKS, LANES), F32), pltpu.SemaphoreType.DMA((2,))],
    )
    return pl.pallas_call(
        kernel,
        grid_spec=grid_spec,
        out_shape=jax.ShapeDtypeStruct((n_tok, D_MODEL), F32),
        compiler_params=_cparams(("arbitrary",)),
        name="combine",
    )(dest_flat, ys3d, h_pool, route)


def kernel(x_prompt, x_sample, cache_k, cache_v, state_ret, page_table, norm1, w_in, a_q_norm, a_k_norm,
           a_lambda_q1, a_lambda_k1, a_lambda_q2, a_lambda_k2, a_subln, r_norm, w_o, norm2,
           w_group_router, b_group_router, w_expert_router, b_expert_router, w_gate, w_up, w_down):
    depth = norm1.shape[0]
    assert depth == 1, "single-layer step"
    batch, seq, d = x_prompt.shape
    nsamp, tdec, _ = x_sample.shape
    assert d == D_MODEL and tdec == 1
    assert seq % TOKEN_TILE == 0 and seq % ATTN_TILE == 0 and seq % RET_CHUNK == 0
    assert nsamp % PLAN_TILE == 0 and nsamp % SAMPLE_BLOCK == 0
    past = page_table.shape[1] * cache_k.shape[2]
    n_prompt = batch * seq
    n_total = n_prompt + nsamp
    lam_init = 0.8 - 0.6 * math.exp(-0.3 * 0)
    l = 0

    lams = tuple(a[l].reshape(1, A_QK) for a in (a_lambda_q1, a_lambda_k1, a_lambda_q2, a_lambda_k2))
    w_in_bf = w_in[l].astype(BF16)
    w_o_bf = w_o[l].astype(BF16)
    g1 = norm1[l].reshape(1, D_MODEL)
    n2 = norm2[l].reshape(1, D_MODEL)
    qg = jnp.tile(a_q_norm[l], LANES // A_QK).reshape(1, LANES)
    kg = jnp.tile(a_k_norm[l], LANES // A_QK).reshape(1, LANES)
    asub = a_subln[l].reshape(1, A_ROW)
    rnorm = r_norm[l].reshape(1, R_V)
    ones = _segment_ones()
    w_r = jnp.concatenate([w_group_router[l], w_expert_router[l]], axis=1)
    w_r = jnp.pad(w_r, ((0, 0), (0, LANES - w_r.shape[1])))
    wr_hi = w_r.astype(BF16)
    wr_lo = (w_r - wr_hi.astype(F32)).astype(BF16)
    wr_cat = jnp.concatenate([wr_hi, wr_lo], axis=1)
    b_r = jnp.concatenate([b_group_router[l], b_expert_router[l]])
    b_r = jnp.pad(b_r, (0, LANES - b_r.shape[0])).reshape(1, LANES)

    pos_s = jnp.full((nsamp,), past, dtype=jnp.int32)
    xs = x_sample.reshape(nsamp, D_MODEL)
    (sq_bf, k_s, _, v_s, _, srq, srk, srv, srg) = _proj(
        xs, g1, w_in_bf, qg, kg, _rope_tables(pos_s), _ret_tables(pos_s), ones, PLAN_TILE)
    pos_p = jnp.arange(seq)
    xp = x_prompt.reshape(n_prompt, D_MODEL)
    (q_bf, k_p, k_bf, v_p, v_bf, rq, rk, rv, rg) = _proj(
        xp, g1, w_in_bf, qg, kg, _rope_tables(pos_p), _ret_tables(pos_p), ones, TOKEN_TILE)
    a_o, a_o_s = _attn(q_bf, k_bf, v_bf, lams, batch, seq, lam_init, page_table, sq_bf.astype(F32),
                       k_s, v_s, cache_k[l], cache_v[l])

    r_o_s, r_state_s = _ret_step(srq, srk, srv, state_ret[l].reshape(nsamp, R_QW, R_V))
    sample_rows = _mix_out(xs, a_o_s, r_o_s, srg, asub, rnorm, w_o_bf, n2, wr_cat, wr_hi, b_r,
                           lam_init, PLAN_TILE)

    r_o, r_state_p = _retention(rq, rk, rv, batch, seq)
    h_pool, xn_pool, route = _mix_out(xp, a_o, r_o, rg, asub, rnorm, w_o_bf, n2, wr_cat, wr_hi, b_r,
                                      lam_init, TOKEN_TILE, tail=sample_rows)
    n_pool = h_pool.shape[0]

    n_assign = 2 * n_total
    n_tiles = -(-(n_assign + N_EXPERTS * (SLOT_TILE - 1)) // SLOT_TILE)
    n_slots = n_tiles * SLOT_TILE
    dest128, te128, meta = _plan(route, n_total, n_tiles)
    dest_flat = dest128[:, :2].reshape(-1)
    te = te128[:n_tiles, 0]
    meta_flat = meta.reshape(-1)
    xs = _dispatch(dest_flat, meta_flat, xn_pool, n_total, n_tiles)
    ys = _experts(te, meta_flat, xs,
                  w_gate[l].reshape(N_EXPERTS, D_MODEL, D_EXPERT),
                  w_up[l].reshape(N_EXPERTS, D_MODEL, D_EXPERT),
                  w_down[l].reshape(N_EXPERTS, D_EXPERT, D_MODEL), n_tiles)
    ys3d = ys.reshape(n_slots, CHUNKS, LANES)
    y_p = _combine(dest_flat, ys3d, h_pool, route, TOKEN_TILE, 0, n_prompt)
    y_s = _combine(dest_flat, ys3d, h_pool, route, PLAN_TILE, n_prompt, nsamp)

    return (y_p.reshape(batch, seq, D_MODEL),
            y_s.reshape(nsamp, 1, D_MODEL),
            k_p.reshape(1, batch, seq, A_HEADS, A_ROW),
            v_p.reshape(1, batch, seq, A_HEADS, A_ROW),
            r_state_p.reshape(1, batch, R_HEADS, R_QK, R_V),
            k_s.reshape(1, nsamp, 1, A_HEADS, A_ROW),
            v_s.reshape(1, nsamp, 1, A_HEADS, A_ROW),
            r_state_s.reshape(1, nsamp, R_HEADS, R_QK, R_V))
```

```python
import functools
import math

import numpy as np
import jax
import jax.numpy as jnp
from jax import lax
from jax.experimental import pallas as pl
from jax.experimental.pallas import tpu as pltpu

F32 = jnp.float32
BF16 = jnp.bfloat16
I32 = jnp.int32

LANES = 128
SUBLANES = 8
CHUNKS = 8

D_MODEL = 1024
A_HEADS = 4
A_QK = 64
A_ROW = 2 * A_QK
A_WIDTH = A_HEADS * A_ROW
ROPE_THETA = 500000.0
ROPE_DIM = A_QK // 4
R_HEADS = 4
R_QK = 64
R_V = 128
R_QW = R_HEADS * R_QK
R_WIDTH = R_HEADS * R_V
R_THETA = 10000.0
N_GROUPS = 4
EXPERTS_PER_GROUP = 8
N_EXPERTS = N_GROUPS * EXPERTS_PER_GROUP
D_EXPERT = 512
EPS = 1e-6
NEG = -1e30

TOKEN_TILE = 256
COMBINE_TILE = 512
ATTN_TILE = 512
ONES_ROWS = 16
LOG2E = 1.4426950408889634
PLAN_TILE = 128
PLAN_UNROLL = 3
SLOT_TILE = 256
VMEM_LIMIT = 56 * 1024 * 1024


def _cparams(sem, vmem=VMEM_LIMIT):
    return pltpu.CompilerParams(dimension_semantics=sem, vmem_limit_bytes=vmem)


def _rope_tables(pos):
    half = ROPE_DIM // 2
    inv = ROPE_THETA ** (-jnp.arange(half, dtype=F32) / half)
    ang = pos.astype(F32)[:, None] * inv[None, :]
    cos, sin = jnp.cos(ang), jnp.sin(ang)
    n = pos.shape[0]
    ones = jnp.ones((n, A_QK - ROPE_DIM), F32)
    zeros = jnp.zeros((n, A_QK - ROPE_DIM), F32)
    zh = jnp.zeros((n, half), F32)
    c = jnp.concatenate([cos, cos, ones], axis=1)
    s_next = jnp.concatenate([-sin, zh, zeros], axis=1)
    s_prev = jnp.concatenate([zh, sin, zeros], axis=1)
    rep = LANES // A_QK
    return tuple(jnp.tile(t, (1, rep)) for t in (c, s_next, s_prev))


def _ret_tables(pos):
    half = R_QK // 2
    inv = 1.0 / (R_THETA ** jnp.linspace(0.0, 1.0, half, dtype=F32))
    ang = pos.astype(F32)[:, None] * inv[None, :]
    cos, sin = jnp.cos(ang), jnp.sin(ang)
    z = jnp.zeros_like(sin)
    c = jnp.stack([cos, cos], axis=-1).reshape(-1, R_QK)
    s_next = jnp.stack([-sin, z], axis=-1).reshape(-1, R_QK)
    s_prev = jnp.stack([z, sin], axis=-1).reshape(-1, R_QK)
    rep = LANES // R_QK
    return tuple(jnp.tile(t, (1, rep)) for t in (c, s_next, s_prev))


def _segment_ones():
    seg = np.arange(LANES) // A_QK
    return jnp.asarray((seg[:, None] == seg[None, :]).astype(np.float32), dtype=BF16)


def _proj_kernel(x_ref, g1_ref, w_ref, qg_ref, kg_ref, rc_ref, rn_ref, rp_ref,
                 tc_ref, tn_ref, tp_ref, ones_ref,
                 qbf_ref, k_ref, kbf_ref, v_ref, vbf_ref, rq_ref, rk_ref, rv_ref, rg_ref):
    tm = x_ref.shape[0]
    x = x_ref[...]
    ms = jnp.mean(x * x, axis=-1, keepdims=True)
    n = (x * lax.rsqrt(ms + EPS) * g1_ref[...]).astype(BF16)
    h = jnp.dot(n, w_ref[...], preferred_element_type=F32)
    ones = ones_ref[...]
    rc, rn, rp = rc_ref[...], rn_ref[...], rp_ref[...]

    def head_norm_rope(xh, gain):
        sq = xh * xh
        hi = sq.astype(BF16)
        lo = (sq - hi.astype(F32)).astype(BF16)
        ssq = (jnp.dot(hi, ones, preferred_element_type=F32)
               + jnp.dot(lo, ones, preferred_element_type=F32))
        y = xh * lax.rsqrt(ssq * (1.0 / A_QK) + EPS) * gain
        half = ROPE_DIM // 2
        return (y * rc + pltpu.roll(y, LANES - half, 1) * rn + pltpu.roll(y, half, 1) * rp)

    for hd in range(A_HEADS):
        sl = slice(hd * A_ROW, (hd + 1) * A_ROW)
        q = head_norm_rope(h[:, sl], qg_ref[...])
        qbf_ref[:, sl] = (q * (A_QK ** -0.5 * LOG2E)).astype(BF16)
        k = head_norm_rope(h[:, A_WIDTH + hd * A_ROW:A_WIDTH + (hd + 1) * A_ROW], kg_ref[...])
        k_ref[pl.ds(hd, tm, stride=A_HEADS), :] = k
        kbf_ref[:, sl] = k.astype(BF16)
    o = 2 * A_WIDTH
    v = h[:, o:o + A_WIDTH]
    for hd in range(A_HEADS):
        v_ref[pl.ds(hd, tm, stride=A_HEADS), :] = v[:, hd * A_ROW:(hd + 1) * A_ROW]
    vbf_ref[...] = v.astype(BF16)
    o += A_WIDTH
    tc, tn, tp = tc_ref[...], tn_ref[...], tp_ref[...]

    def pair_rotate(xs):
        return xs * tc + pltpu.roll(xs, LANES - 1, 1) * tn + pltpu.roll(xs, 1, 1) * tp

    for j in range(R_QW // LANES):
        sl = slice(j * LANES, (j + 1) * LANES)
        rq_ref[:, sl] = pair_rotate(h[:, o + j * LANES:o + (j + 1) * LANES])
        rk_ref[:, sl] = pair_rotate(h[:, o + R_QW + j * LANES:o + R_QW + (j + 1) * LANES]) * (R_QK ** -0.5)
    o += 2 * R_QW
    rv_ref[...] = h[:, o:o + R_WIDTH]
    rg_ref[...] = h[:, o + R_WIDTH:o + 2 * R_WIDTH]


def _proj(x2d, g1, w_bf, qg, kg, rope_t, ret_t, ones, tm):
    t = x2d.shape[0]
    table_tiles = rope_t[0].shape[0] // tm
    row = lambda w: pl.BlockSpec((tm, w), lambda i: (i, 0))
    table = pl.BlockSpec((tm, LANES), lambda i: (i % table_tiles, 0))
    full = lambda a: pl.BlockSpec(a.shape, lambda i: (0,) * a.ndim)
    out_shape = (
        jax.ShapeDtypeStruct((t, A_WIDTH), BF16),
        jax.ShapeDtypeStruct((t * A_HEADS, A_ROW), F32),
        jax.ShapeDtypeStruct((t, A_WIDTH), BF16),
        jax.ShapeDtypeStruct((t * A_HEADS, A_ROW), F32),
        jax.ShapeDtypeStruct((t, A_WIDTH), BF16),
        jax.ShapeDtypeStruct((t, R_QW), F32),
        jax.ShapeDtypeStruct((t, R_QW), F32),
        jax.ShapeDtypeStruct((t, R_WIDTH), F32),
        jax.ShapeDtypeStruct((t, R_WIDTH), F32),
    )
    return pl.pallas_call(
        _proj_kernel,
        grid=(t // tm,),
        in_specs=[row(D_MODEL), full(g1), full(w_bf), full(qg), full(kg)]
                 + [table] * 6 + [full(ones)],
        out_specs=tuple(pl.BlockSpec((tm * s.shape[0] // t, s.shape[1]), lambda i: (i, 0)) for s in out_shape),
        out_shape=out_shape,
        compiler_params=_cparams(("parallel",)),
        name="proj",
    )(x2d, g1, w_bf, qg, kg, *rope_t, *ret_t, ones)


def _lambda(lq1_ref, lk1_ref, lq2_ref, lk2_ref, lam_init):
    s1 = jnp.sum(lq1_ref[...] * lk1_ref[...], axis=-1, keepdims=True)
    s2 = jnp.sum(lq2_ref[...] * lk2_ref[...], axis=-1, keepdims=True)
    return jnp.exp(s1) - jnp.exp(s2) + lam_init


def _attn_kernel(pt_ref, q_ref, k_ref, v_ref, lq1_ref, lk1_ref, lq2_ref, lk2_ref,
                 sq_ref, skn_ref, svn_ref, bias_ref, nbias_ref,
                 rq_ref, rk_ref, rv_ref, dmask_ref, qdec_ref, kdec_ref, grow_ref,
                 srq_ref, srk_ref, srv_ref, sstate_ref, sgrow_ref, *rest, tile, lam_init, npages):
    k_pages = rest[:npages]
    v_pages = rest[npages:2 * npages]
    (o_ref, os_ref, ro_ref, rst_ref, sro_ref, sns_ref,
     m_ref, l_ref, acc_ref, qq_ref, sa_ref, sb_ref, rstate_ref) = rest[2 * npages:]
    del pt_ref
    lam = _lambda(lq1_ref, lk1_ref, lq2_ref, lk2_ref, lam_init)
    _paged_step(sq_ref, skn_ref, svn_ref, bias_ref, nbias_ref, lam, k_pages, v_pages, os_ref)
    _ret_token(srq_ref, srk_ref, srv_ref, sstate_ref, sgrow_ref, sro_ref, sns_ref)
    first = (pl.program_id(1) == 0) & (pl.program_id(2) == 0)
    _ret_chunk(rq_ref, rk_ref, rv_ref, dmask_ref, qdec_ref, kdec_ref, grow_ref, ro_ref, rst_ref, rstate_ref, first)

    i = pl.program_id(2)
    q = q_ref[...]
    lane = lax.broadcasted_iota(I32, q.shape, 1)
    zero = jnp.zeros_like(q)
    qq = jnp.concatenate([jnp.where(lane < A_QK, q, zero), jnp.where(lane >= A_QK, q, zero)], axis=0)
    qq_ref[...] = qq
    m_ref[...] = jnp.full(m_ref.shape, NEG, F32)
    l_ref[...] = jnp.zeros(l_ref.shape, F32)
    acc_ref[...] = jnp.zeros(acc_ref.shape, F32)

    ones_rows = jnp.ones((ONES_ROWS, tile), BF16)

    def scores(j, s_ref):
        k = k_ref[pl.ds(pl.multiple_of(j * tile, tile), tile), :]
        s_ref[...] = lax.dot_general(k, qq_ref[...], (((1,), (1,)), ((), ())), preferred_element_type=F32)

    def accumulate(j, s_ref, masked):
        v = v_ref[pl.ds(pl.multiple_of(j * tile, tile), tile), :]
        vt = jnp.concatenate([v.T, ones_rows], axis=0)
        s = s_ref[...]
        if masked:
            key = lax.broadcasted_iota(I32, s.shape, 0)
            qry = lax.broadcasted_iota(I32, s.shape, 1)
            qry = jnp.where(qry >= tile, qry - tile, qry)
            s = jnp.where(key <= qry, s, NEG)
        m_prev = m_ref[...]
        m_new = jnp.maximum(m_prev, jnp.max(s, axis=0, keepdims=True))
        alpha = jnp.exp2(m_prev - m_new)
        p = jnp.exp2(s - m_new).astype(BF16)
        pv = jnp.dot(vt, p, preferred_element_type=F32)
        acc_ref[...] = alpha * acc_ref[...] + pv[:A_ROW, :]
        l_ref[...] = alpha * l_ref[...] + pv[A_ROW:A_ROW + 1, :]
        m_ref[...] = m_new

    scores(0, sa_ref)

    def pair(t, carry):
        j = 2 * t
        scores(j + 1, sb_ref)
        accumulate(j, sa_ref, False)
        scores(j + 2, sa_ref)
        accumulate(j + 1, sb_ref, False)
        return carry

    lax.fori_loop(0, i // 2, pair, 0)

    @pl.when(i % 2 == 0)
    def _():
        accumulate(i, sa_ref, True)

    @pl.when(i % 2 == 1)
    def _():
        scores(i, sb_ref)
        accumulate(i - 1, sa_ref, False)
        accumulate(i, sb_ref, True)

    o1 = acc_ref[:, :tile] / l_ref[:, :tile]
    o2 = acc_ref[:, tile:] / l_ref[:, tile:]
    o_ref[...] = (o1 - lam * o2).T


def _mixers(q_bf, k_bf, v_bf, lams, batch, seq, lam_init, page_table, sq, k_new, v_new, cache_k, cache_v,
            rq, rk, rv, srq, srk, srv, sstate):
    tile = ATTN_TILE
    nq = seq // tile
    nseq, npages = page_table.shape
    steps_per_seq = A_HEADS * nq
    assert nseq == batch * steps_per_seq, "one decode sequence per prompt query block"
    assert seq % (steps_per_seq * SUBLANES) == 0, "one retention chunk per grid step"
    chunk = seq // steps_per_seq
    dmask, q_dec, k_dec, g_rows = _ret_tables_chunk(chunk)
    _, _, _, g_token = _ret_tables_chunk(1)
    n_phys, page = cache_k.shape[0], cache_k.shape[1]
    prow = page * A_HEADS
    ck = cache_k.reshape(n_phys, prow, A_ROW)
    cv = cache_v.reshape(n_phys, prow, A_ROW)
    r = np.arange(2 * A_HEADS)[:, None] % A_HEADS
    bias = jnp.asarray(np.where(np.arange(prow)[None, :] % A_HEADS == r, 0.0, NEG).astype(np.float32))
    nbias = jnp.asarray(np.where(np.arange(LANES)[None, :] == r, 0.0, NEG).astype(np.float32))
    step = lambda b, h, i: (b * A_HEADS + h) * nq + i
    lam_spec = pl.BlockSpec((1, A_QK), lambda b, h, i, pt: (0, 0))
    tok = pl.BlockSpec((None, 1, A_WIDTH), lambda b, h, i, pt: (step(b, h, i), 0, 0))
    full = lambda a: pl.BlockSpec(a.shape, lambda b, h, i, pt: (0,) * a.ndim)

    def page_spec(jj):
        return pl.BlockSpec((None, prow, A_ROW), lambda b, h, i, pt: (pt[step(b, h, i) * npages + jj], 0, 0))

    crow = lambda w: pl.BlockSpec((chunk, w), lambda b, h, i, pt: (step(b, h, i), 0))
    stok = lambda w: pl.BlockSpec((None, 1, w), lambda b, h, i, pt: (step(b, h, i), 0, 0))
    sstate_spec = pl.BlockSpec((None, R_QW, R_V), lambda b, h, i, pt: (step(b, h, i), 0, 0))
    kernel = functools.partial(_attn_kernel, tile=tile, lam_init=lam_init, npages=npages)
    grid_spec = pltpu.PrefetchScalarGridSpec(
        num_scalar_prefetch=1,
        grid=(batch, A_HEADS, nq),
        in_specs=[pl.BlockSpec((tile, A_ROW), lambda b, h, i, pt: (b * nq + i, h)),
                  pl.BlockSpec((seq, A_ROW), lambda b, h, i, pt: (b, h)),
                  pl.BlockSpec((seq, A_ROW), lambda b, h, i, pt: (b, h))] + [lam_spec] * 4
                 + [tok, tok, tok, full(bias), full(nbias)]
                 + [crow(R_QW), crow(R_QW), crow(R_WIDTH), full(dmask), full(q_dec), full(k_dec), full(g_rows)]
                 + [stok(R_QW), stok(R_QW), stok(R_WIDTH), sstate_spec, full(g_token)]
                 + [page_spec(jj) for jj in range(npages)] * 2,
        out_specs=(pl.BlockSpec((tile, A_ROW), lambda b, h, i, pt: (b * nq + i, h)), tok,
                   crow(R_WIDTH), pl.BlockSpec((None, R_QW, R_V), lambda b, h, i, pt: (b, 0, 0)),
                   stok(R_WIDTH), sstate_spec),
        scratch_shapes=[pltpu.VMEM((1, 2 * tile), F32), pltpu.VMEM((1, 2 * tile), F32),
                        pltpu.VMEM((A_ROW, 2 * tile), F32), pltpu.VMEM((2 * tile, A_ROW), BF16),
                        pltpu.VMEM((tile, 2 * tile), F32), pltpu.VMEM((tile, 2 * tile), F32),
                        pltpu.VMEM((R_QW, R_V), F32)],
    )
    a_o, a_o_s, r_o, r_state, r_o_s, r_state_s = pl.pallas_call(
        kernel,
        grid_spec=grid_spec,
        out_shape=(jax.ShapeDtypeStruct((batch * seq, A_WIDTH), F32),
                   jax.ShapeDtypeStruct((nseq, 1, A_WIDTH), F32),
                   jax.ShapeDtypeStruct((batch * seq, R_WIDTH), F32),
                   jax.ShapeDtypeStruct((batch, R_QW, R_V), F32),
                   jax.ShapeDtypeStruct((nseq, 1, R_WIDTH), F32),
                   jax.ShapeDtypeStruct((nseq, R_QW, R_V), F32)),
        compiler_params=_cparams(("arbitrary", "arbitrary", "arbitrary")),
        name="mixers",
    )(page_table.reshape(-1), q_bf, k_bf, v_bf, *lams,
      sq.reshape(nseq, 1, A_WIDTH), k_new.reshape(nseq, 1, A_WIDTH), v_new.reshape(nseq, 1, A_WIDTH),
      bias, nbias, rq, rk, rv, dmask, q_dec, k_dec, g_rows,
      srq.reshape(nseq, 1, R_QW), srk.reshape(nseq, 1, R_QW), srv.reshape(nseq, 1, R_WIDTH), sstate, g_token,
      *([ck] * npages), *([cv] * npages))
    return (a_o, a_o_s.reshape(nseq, A_WIDTH), r_o, r_state,
            r_o_s.reshape(nseq, R_WIDTH), r_state_s)


def _ret_decay():
    return [math.log(1.0 - 2.0 ** (-5.0 - h)) for h in range(R_HEADS)]


def _ret_tables_chunk(chunk):
    log_g = jnp.log(1.0 - 2.0 ** (-5.0 - jnp.arange(R_HEADS, dtype=F32)))
    idx = jnp.arange(chunk, dtype=F32)
    diff = idx[:, None] - idx[None, :]
    dmask = jnp.where(diff >= 0, jnp.exp(jnp.maximum(diff, 0.0)[None] * log_g[:, None, None]), 0.0)
    q_dec = jnp.exp((idx + 1.0)[:, None] * log_g[None, :])
    k_dec = jnp.exp((chunk - 1.0 - idx)[:, None] * log_g[None, :])
    q_dec = jnp.repeat(q_dec, R_QK, axis=1)
    k_dec = jnp.repeat(k_dec, R_QK, axis=1)
    g_chunk = jnp.exp(chunk * log_g)
    g_rows = jnp.broadcast_to(jnp.repeat(g_chunk, R_QK)[:, None], (R_QW, R_V))
    return dmask, q_dec, k_dec, g_rows


def _ret_chunk(q_ref, k_ref, v_ref, dmask_ref, qdec_ref, kdec_ref, grow_ref, o_ref, st_ref, state_ref, first):
    q = q_ref[...]
    k = k_ref[...]
    qd = (q * qdec_ref[...]).astype(BF16)
    kd = (k * kdec_ref[...]).astype(BF16)
    qb = q.astype(BF16)
    kb = k.astype(BF16)
    vb = v_ref[...].astype(BF16)
    carried = jnp.where(first, 0.0, state_ref[...])
    for h in range(R_HEADS):
        ks = slice(h * R_QK, (h + 1) * R_QK)
        vs = slice(h * R_V, (h + 1) * R_V)
        state = carried[ks, :]
        inner = lax.dot_general(qb[:, ks], kb[:, ks], (((1,), (1,)), ((), ())),
                                preferred_element_type=F32) * dmask_ref[h]
        o = (jnp.dot(inner.astype(BF16), vb[:, vs], preferred_element_type=F32)
             + jnp.dot(qd[:, ks], state.astype(BF16), preferred_element_type=F32))
        o_ref[:, vs] = o
        upd = lax.dot_general(kd[:, ks], vb[:, vs], (((0,), (0,)), ((), ())),
                              preferred_element_type=F32)
        state_ref[ks, :] = grow_ref[ks, :] * state + upd
    st_ref[...] = state_ref[...]


def _ret_token(q_ref, k_ref, v_ref, state_ref, grow_ref, o_ref, ns_ref):
    kcol = jnp.broadcast_to(k_ref[...], (LANES, R_QW)).T
    qcol = jnp.broadcast_to(q_ref[...], (LANES, R_QW)).T
    vrows = jnp.concatenate(
        [jnp.broadcast_to(v_ref[:, h * R_V:(h + 1) * R_V], (R_QK, R_V)) for h in range(R_HEADS)], axis=0)
    new = grow_ref[...] * state_ref[...] + kcol * vrows
    ns_ref[...] = new
    qn = qcol * new
    for h in range(R_HEADS):
        o_ref[:, h * R_V:(h + 1) * R_V] = jnp.sum(qn[h * R_QK:(h + 1) * R_QK, :], axis=0, keepdims=True)


def _paged_step(q_ref, kn_ref, vn_ref, bias_ref, nbias_ref, lam, k_refs, v_refs, o_ref):
    npages = len(k_refs)
    q = q_ref[...]
    nrow = 2 * A_HEADS

    def head_rows(x, n):
        row = lax.broadcasted_iota(I32, (n, A_ROW), 0)
        out = jnp.zeros((n, A_ROW), F32)
        for h in range(A_HEADS):
            out = jnp.where(row == h, jnp.broadcast_to(x[:, h * A_ROW:(h + 1) * A_ROW], (n, A_ROW)), out)
        return out

    row8 = lax.broadcasted_iota(I32, (nrow, A_ROW), 0)
    lane8 = lax.broadcasted_iota(I32, (nrow, A_ROW), 1)
    q4 = head_rows(q, nrow)
    q8 = q4 + pltpu.roll(q4, A_HEADS, 0)
    qm = jnp.where((row8 < A_HEADS) == (lane8 < A_QK), q8, 0.0).astype(BF16)

    nt = (((1,), (1,)), ((), ()))
    bias = bias_ref[...]
    s = [lax.dot_general(qm, k_refs[j][...].astype(BF16), nt, preferred_element_type=F32) + bias
         for j in range(npages)]
    kn = head_rows(kn_ref[...], LANES).astype(BF16)
    s.append(lax.dot_general(qm, kn, nt, preferred_element_type=F32) + nbias_ref[...])
    m = functools.reduce(jnp.maximum, [jnp.max(x, axis=-1, keepdims=True) for x in s])
    p = [jnp.exp2(x - m) for x in s]
    l = functools.reduce(lambda a, b: a + b, [jnp.sum(x, axis=-1, keepdims=True) for x in p])
    inv = 1.0 / l
    vs = [v_refs[j][...].astype(BF16) for j in range(npages)] + [head_rows(vn_ref[...], LANES).astype(BF16)]
    out = jnp.zeros((nrow, A_ROW), F32)
    for pj, vj in zip(p, vs):
        pn = pj * inv
        first = lax.broadcasted_iota(I32, pn.shape, 0) < A_HEADS
        w8 = jnp.where(first, pn - lam * pltpu.roll(pn, A_HEADS, 0), 0.0).astype(BF16)
        out = out + jnp.dot(w8, vj, preferred_element_type=F32)
    for h in range(A_HEADS):
        o_ref[:, h * A_ROW:(h + 1) * A_ROW] = out[h:h + 1, :]


def _mix_tile(x_ref, a_ref, r_ref, g_ref, asub_ref, rnorm_ref, wo_ref, n2_ref,
              wrc_ref, wrh_ref, br_ref, h_ref, xn_ref, route_ref, *, lam_init):
    tm = x_ref.shape[0]
    parts = []
    for hd in range(A_HEADS):
        a = a_ref[:, hd * A_ROW:(hd + 1) * A_ROW]
        ms = jnp.mean(a * a, axis=-1, keepdims=True)
        parts.append((a * lax.rsqrt(ms + EPS) * asub_ref[...] * (1.0 - lam_init)).astype(BF16))
    for hd in range(R_HEADS):
        sl = slice(hd * R_V, (hd + 1) * R_V)
        r = r_ref[:, sl]
        ms = jnp.mean(r * r, axis=-1, keepdims=True)
        gate = g_ref[:, sl]
        gate = gate * (1.0 / (1.0 + jnp.exp(-gate)))
        parts.append((r * lax.rsqrt(ms + EPS) * rnorm_ref[...] * gate).astype(BF16))
    merged = jnp.concatenate(parts, axis=1)
    h = x_ref[...] + jnp.dot(merged, wo_ref[...], preferred_element_type=F32)
    h_ref[...] = h
    ms = jnp.mean(h * h, axis=-1, keepdims=True)
    xn = h * lax.rsqrt(ms + EPS) * n2_ref[...]
    for c in range(CHUNKS):
        xn_ref[pl.ds(c, tm, stride=CHUNKS), :] = xn[:, c * LANES:(c + 1) * LANES]
    xh = xn.astype(BF16)
    xl = (xn - xh.astype(F32)).astype(BF16)
    both = jnp.dot(xh, wrc_ref[...], preferred_element_type=F32)
    logits = (both[:, :LANES] + jnp.dot(xl, wrh_ref[...], preferred_element_type=F32)
              + both[:, LANES:]) + br_ref[...]
    lane = lax.broadcasted_iota(I32, logits.shape, 1).astype(F32)
    big = float(LANES)
    gl = jnp.where(lane < N_GROUPS, logits, NEG)
    gmax = jnp.max(gl, axis=-1, keepdims=True)
    gidx = jnp.min(jnp.where(gl == gmax, lane, big), axis=-1, keepdims=True)
    gsum = jnp.sum(jnp.where(lane < N_GROUPS, jnp.exp(gl - gmax), 0.0), axis=-1, keepdims=True)
    gprob = 1.0 / gsum
    lo = N_GROUPS + EXPERTS_PER_GROUP * gidx
    el = jnp.where((lane >= lo) & (lane < lo + EXPERTS_PER_GROUP), logits, NEG)
    v1 = jnp.max(el, axis=-1, keepdims=True)
    i1 = jnp.min(jnp.where(el == v1, lane, big), axis=-1, keepdims=True)
    el2 = jnp.where(lane == i1, NEG, el)
    v2 = jnp.max(el2, axis=-1, keepdims=True)
    i2 = jnp.min(jnp.where(el2 == v2, lane, big), axis=-1, keepdims=True)
    e = jnp.exp(v2 - v1)
    w1 = gprob / (1.0 + e)
    w2 = gprob * e / (1.0 + e)
    e1 = i1 - N_GROUPS
    e2 = i2 - N_GROUPS
    route_ref[...] = jnp.where(lane == 0, e1, jnp.where(lane == 1, e2, jnp.where(
        lane == 2, w1, jnp.where(lane == 3, w2, 0.0))))


def _mix_kernel(*refs, lam_init, n_tiles, has_tail):
    if not has_tail:
        _mix_tile(*refs, lam_init=lam_init)
        return
    ins, (th_ref, txn_ref, troute_ref), outs = refs[:11], refs[11:14], refs[14:]
    h_ref, xn_ref, route_ref = outs
    i = pl.program_id(0)

    @pl.when(i < n_tiles)
    def _():
        _mix_tile(*ins, *outs, lam_init=lam_init)

    @pl.when(i == n_tiles)
    def _():
        tm = h_ref.shape[0]
        nt = th_ref.shape[0]
        h_ref[:nt, :] = th_ref[...]
        h_ref[nt:, :] = jnp.zeros((tm - nt, D_MODEL), F32)
        xn_ref[:nt * CHUNKS, :] = txn_ref[...]
        xn_ref[nt * CHUNKS:, :] = jnp.zeros(((tm - nt) * CHUNKS, LANES), F32)
        route_ref[:nt, :] = troute_ref[...]
        route_ref[nt:, :] = jnp.zeros((tm - nt, LANES), F32)


def _mix_out(x2d, a_o, r_o, rg, asub, rnorm, wo_bf, n2, wr_cat, wr_hi, br, lam_init, tm, tail=None):
    t = x2d.shape[0]
    n_tiles = t // tm
    has_tail = tail is not None
    n_out = n_tiles + (1 if has_tail else 0)
    row = lambda w: pl.BlockSpec((tm, w), lambda i: (jnp.minimum(i, n_tiles - 1), 0))
    full = lambda a: pl.BlockSpec(a.shape, lambda i: (0,) * a.ndim)
    out_shape = (jax.ShapeDtypeStruct((n_out * tm, D_MODEL), F32),
                 jax.ShapeDtypeStruct((n_out * tm * CHUNKS, LANES), F32),
                 jax.ShapeDtypeStruct((n_out * tm, LANES), F32))
    out_specs = (pl.BlockSpec((tm, D_MODEL), lambda i: (i, 0)),
                 pl.BlockSpec((tm * CHUNKS, LANES), lambda i: (i, 0)),
                 pl.BlockSpec((tm, LANES), lambda i: (i, 0)))
    in_specs = [row(D_MODEL), row(A_WIDTH), row(R_WIDTH), row(R_WIDTH), full(asub), full(rnorm),
                full(wo_bf), full(n2), full(wr_cat), full(wr_hi), full(br)]
    args = [x2d, a_o, r_o, rg, asub, rnorm, wo_bf, n2, wr_cat, wr_hi, br]
    if has_tail:
        assert tail[0].shape[0] <= tm
        in_specs += [full(a) for a in tail]
        args += list(tail)
    kernel = functools.partial(_mix_kernel, lam_init=lam_init, n_tiles=n_tiles, has_tail=has_tail)
    return pl.pallas_call(
        kernel,
        grid=(n_out,),
        in_specs=in_specs,
        out_specs=out_specs,
        out_shape=out_shape,
        compiler_params=_cparams(("arbitrary",)),
        name="mix_out",
    )(*args)


def _plan_kernel(route_ref, ltri_ref, utri_ref, dest_ref, te_ref, meta_ref, *, n, slot_tile):
    tile = PLAN_TILE
    lane = lax.broadcasted_iota(I32, (tile, LANES), 1)

    def block(b):
        r = route_ref[pl.ds(pl.multiple_of(b * tile, tile), tile), :]
        e1 = r[:, 0:1].astype(I32)
        e2 = r[:, 1:2].astype(I32)
        return jnp.where((lane == e1) | (lane == e2), 1.0, 0.0), e1, e2

    def count(b, c):
        return c + jnp.sum(block(b)[0], axis=0, keepdims=True)

    cnt = lax.fori_loop(0, n // tile, count, jnp.zeros((1, LANES), F32), unroll=PLAN_UNROLL)
    ntile = jnp.floor((cnt + (slot_tile - 1)) * (1.0 / slot_tile))
    nt8 = jnp.broadcast_to(ntile, (SUBLANES, LANES)).astype(BF16)
    base_t = jnp.dot(nt8, utri_ref[...], preferred_element_type=F32)[0:1, :]
    base = base_t * slot_tile
    ends = base_t + ntile
    tl = lax.broadcasted_iota(I32, te_ref.shape, 0).astype(F32)
    el = lax.broadcasted_iota(I32, te_ref.shape, 1)
    hit = jnp.where((el < N_EXPERTS) & (ends <= tl), 1.0, 0.0)
    te_ref[...] = jnp.broadcast_to(jnp.sum(hit, axis=-1, keepdims=True), te_ref.shape).astype(I32)
    mrow = lax.broadcasted_iota(I32, meta_ref.shape, 0)
    meta_ref[...] = jnp.where(mrow == 0, cnt, jnp.where(mrow == 1, base, jnp.where(
        mrow == 2, ntile * slot_tile, 0.0))).astype(I32)

    def place(b, run):
        onehot, e1, e2 = block(b)
        rank = jnp.dot(ltri_ref[...], onehot.astype(BF16), preferred_element_type=F32)
        pos = base + run + rank
        d1 = jnp.sum(jnp.where(lane == e1, pos, 0.0), axis=-1, keepdims=True)
        d2 = jnp.sum(jnp.where(lane == e2, pos, 0.0), axis=-1, keepdims=True)
        dest_ref[pl.ds(pl.multiple_of(b * tile, tile), tile), :] = jnp.where(
            lane == 0, d1, jnp.where(lane == 1, d2, 0.0)).astype(I32)
        return run + jnp.sum(onehot, axis=0, keepdims=True)

    lax.fori_loop(0, n // tile, place, jnp.zeros((1, LANES), F32), unroll=PLAN_UNROLL)


def _plan(route, n, n_slot_tiles):
    tile = PLAN_TILE
    te_rows = -(-n_slot_tiles // SUBLANES) * SUBLANES
    ii = np.arange(tile)
    ltri = jnp.asarray((ii[None, :] < ii[:, None]).astype(np.float32), dtype=BF16)
    ee = np.arange(LANES)
    utri = jnp.asarray((ee[:, None] < ee[None, :]).astype(np.float32), dtype=BF16)
    kernel = functools.partial(_plan_kernel, n=n, slot_tile=SLOT_TILE)
    return pl.pallas_call(
        kernel,
        out_shape=(jax.ShapeDtypeStruct((n, LANES), I32),
                   jax.ShapeDtypeStruct((te_rows, LANES), I32),
                   jax.ShapeDtypeStruct((SUBLANES, LANES), I32)),
        compiler_params=pltpu.CompilerParams(vmem_limit_bytes=VMEM_LIMIT),
        name="plan",
    )(route, ltri, utri)


def _total_tiles(meta_ref):
    last = N_EXPERTS - 1
    return (meta_ref[LANES + last] + meta_ref[2 * LANES + last]) // SLOT_TILE


def _dispatch_kernel(dest_ref, meta_ref, xn_ref, xs_hbm, zeros_ref, zsem, ssem, *, tm, n_steps, n_total, n_tiles):
    i = pl.program_id(0)
    ts = SLOT_TILE

    def zero_tiles(start):
        def one(tile):
            cp = pltpu.make_async_copy(zeros_ref, xs_hbm.at[pl.ds(tile * ts, ts)], zsem)
            if start:
                cp.start()
            else:
                cp.wait()

        def per_expert(e, c):
            first = meta_ref[LANES + e]
            reserved = meta_ref[2 * LANES + e]

            @pl.when(meta_ref[e] < reserved)
            def _():
                one((first + reserved) // ts - 1)

            return c

        lax.fori_loop(0, N_EXPERTS, per_expert, 0)

        def trailing(t, c):
            one(t)
            return c

        lax.fori_loop(_total_tiles(meta_ref), n_tiles, trailing, 0)

    @pl.when(i == 0)
    def _():
        zeros_ref[...] = jnp.zeros(zeros_ref.shape, F32)
        zero_tiles(True)
        zero_tiles(False)

    def scatter(n_tok):
        base = 2 * i * tm

        def start(r, c):
            src = xn_ref.at[pl.ds(pl.multiple_of(r * CHUNKS, CHUNKS), CHUNKS), :]
            for k in range(2):
                pltpu.make_async_copy(src, xs_hbm.at[dest_ref[base + 2 * r + k]], ssem).start()
            return c

        lax.fori_loop(0, n_tok, start, 0, unroll=8)

        def wait(r, c):
            pltpu.make_async_copy(xn_ref.at[pl.ds(0, CHUNKS), :], xs_hbm.at[0], ssem).wait()
            return c

        lax.fori_loop(0, 2 * n_tok, wait, 0, unroll=8)

    rem = n_total - (n_steps - 1) * tm
    if rem == tm:
        scatter(tm)
    else:
        pl.when(i < n_steps - 1)(functools.partial(scatter, tm))
        pl.when(i == n_steps - 1)(functools.partial(scatter, rem))


def _dispatch(dest_flat, meta_flat, xn_pool, n_total, n_tiles):
    tm = TOKEN_TILE
    n_steps = -(-n_total // tm)
    assert xn_pool.shape[0] >= n_steps * tm * CHUNKS
    kernel = functools.partial(_dispatch_kernel, tm=tm, n_steps=n_steps, n_total=n_total, n_tiles=n_tiles)
    grid_spec = pltpu.PrefetchScalarGridSpec(
        num_scalar_prefetch=2,
        grid=(n_steps,),
        in_specs=[pl.BlockSpec((tm * CHUNKS, LANES), lambda i, d, m: (i, 0))],
        out_specs=pl.BlockSpec(memory_space=pl.ANY),
        scratch_shapes=[pltpu.VMEM((SLOT_TILE, CHUNKS, LANES), F32),
                        pltpu.SemaphoreType.DMA, pltpu.SemaphoreType.DMA],
    )
    return pl.pallas_call(
        kernel,
        grid_spec=grid_spec,
        out_shape=jax.ShapeDtypeStruct((n_tiles * SLOT_TILE, CHUNKS, LANES), F32),
        compiler_params=_cparams(("arbitrary",)),
        name="dispatch",
    )(dest_flat, meta_flat, xn_pool)


def _row_gather(src_hbm, idx_of_row, buf, sem, n_rows):
    def body(r, c):
        pltpu.make_async_copy(src_hbm.at[idx_of_row(r)],
                              buf.at[pl.ds(pl.multiple_of(r * CHUNKS, CHUNKS), CHUNKS), :], sem).start()
        return c

    lax.fori_loop(0, n_rows, body, 0, unroll=8)


def _row_wait(src_hbm, buf, sem, n_rows):
    def body(r, c):
        pltpu.make_async_copy(src_hbm.at[0], buf.at[pl.ds(0, CHUNKS), :], sem).wait()
        return c

    lax.fori_loop(0, n_rows, body, 0, unroll=8)


def _gathered_rows(buf, n_rows):
    return jnp.concatenate([buf[pl.ds(c, n_rows, stride=CHUNKS), :] for c in range(CHUNKS)], axis=1)


def _expert_kernel(te_ref, meta_ref, xs_ref, wg_ref, wu_ref, wd_ref, ys_ref, wg_bf, wu_bf, wd_bf):
    i = pl.program_id(0)
    ts = SLOT_TILE
    e = te_ref[i]
    active = e < N_EXPERTS
    prev = te_ref[jnp.maximum(i - 1, 0)]

    @pl.when(active & ((i == 0) | (e != prev)))
    def _():
        wg_bf[...] = wg_ref[...].astype(BF16)
        wu_bf[...] = wu_ref[...].astype(BF16)
        wd_bf[...] = wd_ref[...].astype(BF16)

    @pl.when(active)
    def _():
        x = _gathered_rows(xs_ref, ts).astype(BF16)
        a = jnp.dot(x, wg_bf[...], preferred_element_type=F32)
        u = jnp.dot(x, wu_bf[...], preferred_element_type=F32)
        hmid = (a * (1.0 / (1.0 + jnp.exp(-a))) * u).astype(BF16)
        y = jnp.dot(hmid, wd_bf[...], preferred_element_type=F32)
        for c in range(CHUNKS):
            ys_ref[pl.ds(c, ts, stride=CHUNKS), :] = y[:, c * LANES:(c + 1) * LANES]

    @pl.when(jnp.logical_not(active))
    def _():
        ys_ref[...] = jnp.zeros(ys_ref.shape, F32)


def _experts(te, meta_flat, xs, w_gate, w_up, w_down, n_tiles):
    ts = SLOT_TILE
    wsel = lambda i, te, meta: (jnp.minimum(te[i], N_EXPERTS - 1), 0, 0)
    xsel = lambda i, te, meta: (jnp.minimum(i, _total_tiles(meta) - 1), 0)
    grid_spec = pltpu.PrefetchScalarGridSpec(
        num_scalar_prefetch=2,
        grid=(n_tiles,),
        in_specs=[pl.BlockSpec((ts * CHUNKS, LANES), xsel),
                  pl.BlockSpec((None, D_MODEL, D_EXPERT), wsel),
                  pl.BlockSpec((None, D_MODEL, D_EXPERT), wsel),
                  pl.BlockSpec((None, D_EXPERT, D_MODEL), wsel)],
        out_specs=pl.BlockSpec((ts * CHUNKS, LANES), lambda i, te, meta: (i, 0)),
        scratch_shapes=[pltpu.VMEM((D_MODEL, D_EXPERT), BF16),
                        pltpu.VMEM((D_MODEL, D_EXPERT), BF16),
                        pltpu.VMEM((D_EXPERT, D_MODEL), BF16)],
    )
    return pl.pallas_call(
        _expert_kernel,
        grid_spec=grid_spec,
        out_shape=jax.ShapeDtypeStruct((n_tiles * ts * CHUNKS, LANES), F32),
        compiler_params=_cparams(("arbitrary",)),
        name="experts",
    )(te, meta_flat, xs.reshape(n_tiles * ts * CHUNKS, LANES), w_gate, w_up, w_down)


def _combine_kernel(dest_ref, ys_hbm, h_ref, route_ref, y_ref, gbuf, sem, *, tm, n_steps, tok_off):
    i = pl.program_id(0)
    slot = i % 2
    nrow = 2 * tm

    def start(step, sl):
        base = (tok_off + step * tm) * 2
        _row_gather(ys_hbm, lambda r: dest_ref[base + r], gbuf.at[sl], sem.at[sl], nrow)

    @pl.when(i == 0)
    def _():
        start(0, 0)

    @pl.when(i + 1 < n_steps)
    def _():
        start(i + 1, 1 - slot)

    _row_wait(ys_hbm, gbuf.at[slot], sem.at[slot], nrow)
    buf = gbuf.at[slot]
    route = route_ref[...]
    w1 = route[:, 2:3]
    w2 = route[:, 3:4]
    for c in range(CHUNKS):
        g1 = buf[pl.ds(c, tm, stride=2 * CHUNKS), :]
        g2 = buf[pl.ds(CHUNKS + c, tm, stride=2 * CHUNKS), :]
        sl = slice(c * LANES, (c + 1) * LANES)
        y_ref[:, sl] = h_ref[:, sl] + w1 * g1 + w2 * g2


def _combine(dest_flat, ys3d, h_pool, route, tm, tok_off, n_tok):
    n_steps = n_tok // tm
    boff = tok_off // tm
    kernel = functools.partial(_combine_kernel, tm=tm, n_steps=n_steps, tok_off=tok_off)
    grid_spec = pltpu.PrefetchScalarGridSpec(
        num_scalar_prefetch=1,
        grid=(n_steps,),
        in_specs=[pl.BlockSpec(memory_space=pl.ANY),
                  pl.BlockSpec((tm, D_MODEL), lambda i, d: (i + boff, 0)),
                  pl.BlockSpec((tm, LANES), lambda i, d: (i + boff, 0))],
        out_specs=pl.BlockSpec((tm, D_MODEL), lambda i, d: (i, 0)),
        scratch_shapes=[pltpu.VMEM((2, 2 * tm * CHUNKS, LANES), F32), pltpu.SemaphoreType.DMA((2,))],
    )
    return pl.pallas_call(
        kernel,
        grid_spec=grid_spec,
        out_shape=jax.ShapeDtypeStruct((n_tok, D_MODEL), F32),
        compiler_params=_cparams(("arbitrary",)),
        name="combine",
    )(dest_flat, ys3d, h_pool, route)


def kernel(x_prompt, x_sample, cache_k, cache_v, state_ret, page_table, norm1, w_in, a_q_norm, a_k_norm,
           a_lambda_q1, a_lambda_k1, a_lambda_q2, a_lambda_k2, a_subln, r_norm, w_o, norm2,
           w_group_router, b_group_router, w_expert_router, b_expert_router, w_gate, w_up, w_down):
    depth = norm1.shape[0]
    assert depth == 1, "single-layer step"
    batch, seq, d = x_prompt.shape
    nsamp, tdec, _ = x_sample.shape
    assert d == D_MODEL and tdec == 1
    assert seq % TOKEN_TILE == 0 and seq % ATTN_TILE == 0
    assert nsamp % PLAN_TILE == 0
    past = page_table.shape[1] * cache_k.shape[2]
    n_prompt = batch * seq
    assert n_prompt % COMBINE_TILE == 0
    n_total = n_prompt + nsamp
    lam_init = 0.8 - 0.6 * math.exp(-0.3 * 0)
    l = 0

    lams = tuple(a[l].reshape(1, A_QK) for a in (a_lambda_q1, a_lambda_k1, a_lambda_q2, a_lambda_k2))
    w_in_bf = w_in[l].astype(BF16)
    w_o_bf = w_o[l].astype(BF16)
    g1 = norm1[l].reshape(1, D_MODEL)
    n2 = norm2[l].reshape(1, D_MODEL)
    qg = jnp.tile(a_q_norm[l], LANES // A_QK).reshape(1, LANES)
    kg = jnp.tile(a_k_norm[l], LANES // A_QK).reshape(1, LANES)
    asub = a_subln[l].reshape(1, A_ROW)
    rnorm = r_norm[l].reshape(1, R_V)
    ones = _segment_ones()
    w_r = jnp.concatenate([w_group_router[l], w_expert_router[l]], axis=1)
    w_r = jnp.pad(w_r, ((0, 0), (0, LANES - w_r.shape[1])))
    wr_hi = w_r.astype(BF16)
    wr_lo = (w_r - wr_hi.astype(F32)).astype(BF16)
    wr_cat = jnp.concatenate([wr_hi, wr_lo], axis=1)
    b_r = jnp.concatenate([b_group_router[l], b_expert_router[l]])
    b_r = jnp.pad(b_r, (0, LANES - b_r.shape[0])).reshape(1, LANES)

    pos_s = jnp.full((nsamp,), past, dtype=jnp.int32)
    xs = x_sample.reshape(nsamp, D_MODEL)
    (sq_bf, k_s, _, v_s, _, srq, srk, srv, srg) = _proj(
        xs, g1, w_in_bf, qg, kg, _rope_tables(pos_s), _ret_tables(pos_s), ones, PLAN_TILE)
    pos_p = jnp.arange(seq)
    xp = x_prompt.reshape(n_prompt, D_MODEL)
    (q_bf, k_p, k_bf, v_p, v_bf, rq, rk, rv, rg) = _proj(
        xp, g1, w_in_bf, qg, kg, _rope_tables(pos_p), _ret_tables(pos_p), ones, TOKEN_TILE)
    a_o, a_o_s, r_o, r_state_p, r_o_s, r_state_s = _mixers(
        q_bf, k_bf, v_bf, lams, batch, seq, lam_init, page_table, sq_bf.astype(F32), k_s, v_s,
        cache_k[l], cache_v[l], rq, rk, rv, srq, srk, srv, state_ret[l].reshape(nsamp, R_QW, R_V))

    sample_rows = _mix_out(xs, a_o_s, r_o_s, srg, asub, rnorm, w_o_bf, n2, wr_cat, wr_hi, b_r,
                           lam_init, PLAN_TILE)
    h_pool, xn_pool, route = _mix_out(xp, a_o, r_o, rg, asub, rnorm, w_o_bf, n2, wr_cat, wr_hi, b_r,
                                      lam_init, TOKEN_TILE, tail=sample_rows)
    n_pool = h_pool.shape[0]

    n_assign = 2 * n_total
    n_tiles = -(-(n_assign + N_EXPERTS * (SLOT_TILE - 1)) // SLOT_TILE)
    n_slots = n_tiles * SLOT_TILE
    dest128, te128, meta = _plan(route, n_total, n_tiles)
    dest_flat = dest128[:, :2].reshape(-1)
    te = te128[:n_tiles, 0]
    meta_flat = meta.reshape(-1)
    xs = _dispatch(dest_flat, meta_flat, xn_pool, n_total, n_tiles)
    ys = _experts(te, meta_flat, xs,
                  w_gate[l].reshape(N_EXPERTS, D_MODEL, D_EXPERT),
                  w_up[l].reshape(N_EXPERTS, D_MODEL, D_EXPERT),
                  w_down[l].reshape(N_EXPERTS, D_EXPERT, D_MODEL), n_tiles)
    ys3d = ys.reshape(n_slots, CHUNKS, LANES)
    y_p = _combine(dest_flat, ys3d, h_pool, route, COMBINE_TILE, 0, n_prompt)
    y_s = _combine(dest_flat, ys3d, h_pool, route, PLAN_TILE, n_prompt, nsamp)

    return (y_p.reshape(batch, seq, D_MODEL),
            y_s.reshape(nsamp, 1, D_MODEL),
            k_p.reshape(1, batch, seq, A_HEADS, A_ROW),
            v_p.reshape(1, batch, seq, A_HEADS, A_ROW),
            r_state_p.reshape(1, batch, R_HEADS, R_QK, R_V),
            k_s.reshape(1, nsamp, 1, A_HEADS, A_ROW),
            v_s.reshape(1, nsamp, 1, A_HEADS, A_ROW),
            r_state_s.reshape(1, nsamp, R_HEADS, R_QK, R_V))
```

```python
import functools
import math

import numpy as np
import jax
import jax.numpy as jnp
from jax import lax
from jax.experimental import pallas as pl
from jax.experimental.pallas import tpu as pltpu

F32 = jnp.float32
BF16 = jnp.bfloat16
I32 = jnp.int32

LANES = 128
SUBLANES = 8
CHUNKS = 8

D_MODEL = 1024
A_HEADS = 4
A_QK = 64
A_ROW = 2 * A_QK
A_WIDTH = A_HEADS * A_ROW
ROPE_THETA = 500000.0
ROPE_DIM = A_QK // 4
R_HEADS = 4
R_QK = 64
R_V = 128
R_QW = R_HEADS * R_QK
R_WIDTH = R_HEADS * R_V
R_THETA = 10000.0
N_GROUPS = 4
EXPERTS_PER_GROUP = 8
N_EXPERTS = N_GROUPS * EXPERTS_PER_GROUP
D_EXPERT = 512
EPS = 1e-6
NEG = -1e30

TOKEN_TILE = 256
COMBINE_TILE = 512
DMA_PRIORITIES = 2
ATTN_TILE = 512
ONES_ROWS = 16
LOG2E = 1.4426950408889634
PLAN_TILE = 128
PLAN_UNROLL = 3
SLOT_TILE = 256
VMEM_LIMIT = 56 * 1024 * 1024


def _cparams(sem, vmem=VMEM_LIMIT):
    return pltpu.CompilerParams(dimension_semantics=sem, vmem_limit_bytes=vmem)


def _rope_tables(pos):
    half = ROPE_DIM // 2
    inv = ROPE_THETA ** (-jnp.arange(half, dtype=F32) / half)
    ang = pos.astype(F32)[:, None] * inv[None, :]
    cos, sin = jnp.cos(ang), jnp.sin(ang)
    n = pos.shape[0]
    ones = jnp.ones((n, A_QK - ROPE_DIM), F32)
    zeros = jnp.zeros((n, A_QK - ROPE_DIM), F32)
    zh = jnp.zeros((n, half), F32)
    c = jnp.concatenate([cos, cos, ones], axis=1)
    s_next = jnp.concatenate([-sin, zh, zeros], axis=1)
    s_prev = jnp.concatenate([zh, sin, zeros], axis=1)
    rep = LANES // A_QK
    return tuple(jnp.tile(t, (1, rep)) for t in (c, s_next, s_prev))


def _ret_tables(pos):
    half = R_QK // 2
    inv = 1.0 / (R_THETA ** jnp.linspace(0.0, 1.0, half, dtype=F32))
    ang = pos.astype(F32)[:, None] * inv[None, :]
    cos, sin = jnp.cos(ang), jnp.sin(ang)
    z = jnp.zeros_like(sin)
    c = jnp.stack([cos, cos], axis=-1).reshape(-1, R_QK)
    s_next = jnp.stack([-sin, z], axis=-1).reshape(-1, R_QK)
    s_prev = jnp.stack([z, sin], axis=-1).reshape(-1, R_QK)
    rep = LANES // R_QK
    return tuple(jnp.tile(t, (1, rep)) for t in (c, s_next, s_prev))


def _segment_ones():
    seg = np.arange(LANES) // A_QK
    return jnp.asarray((seg[:, None] == seg[None, :]).astype(np.float32), dtype=BF16)


def _proj_kernel(x_ref, g1_ref, w_ref, qg_ref, kg_ref, rc_ref, rn_ref, rp_ref,
                 tc_ref, tn_ref, tp_ref, ones_ref,
                 qbf_ref, k_ref, kbf_ref, v_ref, vbf_ref, rq_ref, rk_ref, rv_ref, rg_ref):
    tm = x_ref.shape[0]
    x = x_ref[...]
    ms = jnp.mean(x * x, axis=-1, keepdims=True)
    n = (x * lax.rsqrt(ms + EPS) * g1_ref[...]).astype(BF16)
    h = jnp.dot(n, w_ref[...], preferred_element_type=F32)
    ones = ones_ref[...]
    rc, rn, rp = rc_ref[...], rn_ref[...], rp_ref[...]

    def head_norm_rope(xh, gain):
        sq = xh * xh
        hi = sq.astype(BF16)
        lo = (sq - hi.astype(F32)).astype(BF16)
        ssq = (jnp.dot(hi, ones, preferred_element_type=F32)
               + jnp.dot(lo, ones, preferred_element_type=F32))
        y = xh * lax.rsqrt(ssq * (1.0 / A_QK) + EPS) * gain
        half = ROPE_DIM // 2
        return (y * rc + pltpu.roll(y, LANES - half, 1) * rn + pltpu.roll(y, half, 1) * rp)

    for hd in range(A_HEADS):
        sl = slice(hd * A_ROW, (hd + 1) * A_ROW)
        q = head_norm_rope(h[:, sl], qg_ref[...])
        qbf_ref[:, sl] = (q * (A_QK ** -0.5 * LOG2E)).astype(BF16)
        k = head_norm_rope(h[:, A_WIDTH + hd * A_ROW:A_WIDTH + (hd + 1) * A_ROW], kg_ref[...])
        k_ref[pl.ds(hd, tm, stride=A_HEADS), :] = k
        kbf_ref[:, sl] = k.astype(BF16)
    o = 2 * A_WIDTH
    v = h[:, o:o + A_WIDTH]
    for hd in range(A_HEADS):
        v_ref[pl.ds(hd, tm, stride=A_HEADS), :] = v[:, hd * A_ROW:(hd + 1) * A_ROW]
    vbf_ref[...] = v.astype(BF16)
    o += A_WIDTH
    tc, tn, tp = tc_ref[...], tn_ref[...], tp_ref[...]

    def pair_rotate(xs):
        return xs * tc + pltpu.roll(xs, LANES - 1, 1) * tn + pltpu.roll(xs, 1, 1) * tp

    for j in range(R_QW // LANES):
        sl = slice(j * LANES, (j + 1) * LANES)
        rq_ref[:, sl] = pair_rotate(h[:, o + j * LANES:o + (j + 1) * LANES])
        rk_ref[:, sl] = pair_rotate(h[:, o + R_QW + j * LANES:o + R_QW + (j + 1) * LANES]) * (R_QK ** -0.5)
    o += 2 * R_QW
    rv_ref[...] = h[:, o:o + R_WIDTH]
    rg_ref[...] = h[:, o + R_WIDTH:o + 2 * R_WIDTH]


def _proj(x2d, g1, w_bf, qg, kg, rope_t, ret_t, ones, tm):
    t = x2d.shape[0]
    table_tiles = rope_t[0].shape[0] // tm
    row = lambda w: pl.BlockSpec((tm, w), lambda i: (i, 0))
    table = pl.BlockSpec((tm, LANES), lambda i: (i % table_tiles, 0))
    full = lambda a: pl.BlockSpec(a.shape, lambda i: (0,) * a.ndim)
    out_shape = (
        jax.ShapeDtypeStruct((t, A_WIDTH), BF16),
        jax.ShapeDtypeStruct((t * A_HEADS, A_ROW), F32),
        jax.ShapeDtypeStruct((t, A_WIDTH), BF16),
        jax.ShapeDtypeStruct((t * A_HEADS, A_ROW), F32),
        jax.ShapeDtypeStruct((t, A_WIDTH), BF16),
        jax.ShapeDtypeStruct((t, R_QW), F32),
        jax.ShapeDtypeStruct((t, R_QW), F32),
        jax.ShapeDtypeStruct((t, R_WIDTH), F32),
        jax.ShapeDtypeStruct((t, R_WIDTH), F32),
    )
    return pl.pallas_call(
        _proj_kernel,
        grid=(t // tm,),
        in_specs=[row(D_MODEL), full(g1), full(w_bf), full(qg), full(kg)]
                 + [table] * 6 + [full(ones)],
        out_specs=tuple(pl.BlockSpec((tm * s.shape[0] // t, s.shape[1]), lambda i: (i, 0)) for s in out_shape),
        out_shape=out_shape,
        compiler_params=_cparams(("parallel",)),
        name="proj",
    )(x2d, g1, w_bf, qg, kg, *rope_t, *ret_t, ones)


def _lambda(lq1_ref, lk1_ref, lq2_ref, lk2_ref, lam_init):
    s1 = jnp.sum(lq1_ref[...] * lk1_ref[...], axis=-1, keepdims=True)
    s2 = jnp.sum(lq2_ref[...] * lk2_ref[...], axis=-1, keepdims=True)
    return jnp.exp(s1) - jnp.exp(s2) + lam_init


def _attn_kernel(pt_ref, q_ref, k_ref, v_ref, lq1_ref, lk1_ref, lq2_ref, lk2_ref,
                 sq_ref, skn_ref, svn_ref, bias_ref, nbias_ref,
                 rq_ref, rk_ref, rv_ref, dmask_ref, qdec_ref, kdec_ref, grow_ref,
                 srq_ref, srk_ref, srv_ref, sstate_ref, sgrow_ref, *rest, tile, lam_init, npages):
    k_pages = rest[:npages]
    v_pages = rest[npages:2 * npages]
    (o_ref, os_ref, ro_ref, rst_ref, sro_ref, sns_ref,
     m_ref, l_ref, acc_ref, qq_ref, sa_ref, sb_ref, rstate_ref) = rest[2 * npages:]
    del pt_ref
    lam = _lambda(lq1_ref, lk1_ref, lq2_ref, lk2_ref, lam_init)
    _paged_step(sq_ref, skn_ref, svn_ref, bias_ref, nbias_ref, lam, k_pages, v_pages, os_ref)
    _ret_token(srq_ref, srk_ref, srv_ref, sstate_ref, sgrow_ref, sro_ref, sns_ref)
    first = (pl.program_id(1) == 0) & (pl.program_id(2) == 0)
    _ret_chunk(rq_ref, rk_ref, rv_ref, dmask_ref, qdec_ref, kdec_ref, grow_ref, ro_ref, rst_ref, rstate_ref, first)

    i = pl.program_id(2)
    q = q_ref[...]
    lane = lax.broadcasted_iota(I32, q.shape, 1)
    zero = jnp.zeros_like(q)
    qq = jnp.concatenate([jnp.where(lane < A_QK, q, zero), jnp.where(lane >= A_QK, q, zero)], axis=0)
    qq_ref[...] = qq
    m_ref[...] = jnp.full(m_ref.shape, NEG, F32)
    l_ref[...] = jnp.zeros(l_ref.shape, F32)
    acc_ref[...] = jnp.zeros(acc_ref.shape, F32)

    ones_rows = jnp.ones((ONES_ROWS, tile), BF16)

    def scores(j, s_ref):
        k = k_ref[pl.ds(pl.multiple_of(j * tile, tile), tile), :]
        s_ref[...] = lax.dot_general(k, qq_ref[...], (((1,), (1,)), ((), ())), preferred_element_type=F32)

    def accumulate(j, s_ref, masked):
        v = v_ref[pl.ds(pl.multiple_of(j * tile, tile), tile), :]
        vt = jnp.concatenate([v.T, ones_rows], axis=0)
        s = s_ref[...]
        if masked:
            key = lax.broadcasted_iota(I32, s.shape, 0)
            qry = lax.broadcasted_iota(I32, s.shape, 1)
            qry = jnp.where(qry >= tile, qry - tile, qry)
            s = jnp.where(key <= qry, s, NEG)
        m_prev = m_ref[...]
        m_new = jnp.maximum(m_prev, jnp.max(s, axis=0, keepdims=True))
        alpha = jnp.exp2(m_prev - m_new)
        p = jnp.exp2(s - m_new).astype(BF16)
        pv = jnp.dot(vt, p, preferred_element_type=F32)
        acc_ref[...] = alpha * acc_ref[...] + pv[:A_ROW, :]
        l_ref[...] = alpha * l_ref[...] + pv[A_ROW:A_ROW + 1, :]
        m_ref[...] = m_new

    scores(0, sa_ref)

    def pair(t, carry):
        j = 2 * t
        scores(j + 1, sb_ref)
        accumulate(j, sa_ref, False)
        scores(j + 2, sa_ref)
        accumulate(j + 1, sb_ref, False)
        return carry

    lax.fori_loop(0, i // 2, pair, 0)

    @pl.when(i % 2 == 0)
    def _():
        accumulate(i, sa_ref, True)

    @pl.when(i % 2 == 1)
    def _():
        scores(i, sb_ref)
        accumulate(i - 1, sa_ref, False)
        accumulate(i, sb_ref, True)

    o1 = acc_ref[:, :tile] / l_ref[:, :tile]
    o2 = acc_ref[:, tile:] / l_ref[:, tile:]
    o_ref[...] = (o1 - lam * o2).T


def _mixers(q_bf, k_bf, v_bf, lams, batch, seq, lam_init, page_table, sq, k_new, v_new, cache_k, cache_v,
            rq, rk, rv, srq, srk, srv, sstate):
    tile = ATTN_TILE
    nq = seq // tile
    nseq, npages = page_table.shape
    steps_per_seq = A_HEADS * nq
    assert nseq == batch * steps_per_seq, "one decode sequence per prompt query block"
    assert seq % (steps_per_seq * SUBLANES) == 0, "one retention chunk per grid step"
    chunk = seq // steps_per_seq
    dmask, q_dec, k_dec, g_rows = _ret_tables_chunk(chunk)
    _, _, _, g_token = _ret_tables_chunk(1)
    n_phys, page = cache_k.shape[0], cache_k.shape[1]
    prow = page * A_HEADS
    ck = cache_k.reshape(n_phys, prow, A_ROW)
    cv = cache_v.reshape(n_phys, prow, A_ROW)
    r = np.arange(2 * A_HEADS)[:, None] % A_HEADS
    bias = jnp.asarray(np.where(np.arange(prow)[None, :] % A_HEADS == r, 0.0, NEG).astype(np.float32))
    nbias = jnp.asarray(np.where(np.arange(LANES)[None, :] == r, 0.0, NEG).astype(np.float32))
    step = lambda b, h, i: (b * A_HEADS + h) * nq + i
    lam_spec = pl.BlockSpec((1, A_QK), lambda b, h, i, pt: (0, 0))
    tok = pl.BlockSpec((None, 1, A_WIDTH), lambda b, h, i, pt: (step(b, h, i), 0, 0))
    full = lambda a: pl.BlockSpec(a.shape, lambda b, h, i, pt: (0,) * a.ndim)

    def page_spec(jj):
        return pl.BlockSpec((None, prow, A_ROW), lambda b, h, i, pt: (pt[step(b, h, i) * npages + jj], 0, 0))

    crow = lambda w: pl.BlockSpec((chunk, w), lambda b, h, i, pt: (step(b, h, i), 0))
    stok = lambda w: pl.BlockSpec((None, 1, w), lambda b, h, i, pt: (step(b, h, i), 0, 0))
    sstate_spec = pl.BlockSpec((None, R_QW, R_V), lambda b, h, i, pt: (step(b, h, i), 0, 0))
    kernel = functools.partial(_attn_kernel, tile=tile, lam_init=lam_init, npages=npages)
    grid_spec = pltpu.PrefetchScalarGridSpec(
        num_scalar_prefetch=1,
        grid=(batch, A_HEADS, nq),
        in_specs=[pl.BlockSpec((tile, A_ROW), lambda b, h, i, pt: (b * nq + i, h)),
                  pl.BlockSpec((seq, A_ROW), lambda b, h, i, pt: (b, h)),
                  pl.BlockSpec((seq, A_ROW), lambda b, h, i, pt: (b, h))] + [lam_spec] * 4
                 + [tok, tok, tok, full(bias), full(nbias)]
                 + [crow(R_QW), crow(R_QW), crow(R_WIDTH), full(dmask), full(q_dec), full(k_dec), full(g_rows)]
                 + [stok(R_QW), stok(R_QW), stok(R_WIDTH), sstate_spec, full(g_token)]
                 + [page_spec(jj) for jj in range(npages)] * 2,
        out_specs=(pl.BlockSpec((tile, A_ROW), lambda b, h, i, pt: (b * nq + i, h)), tok,
                   crow(R_WIDTH), pl.BlockSpec((None, R_QW, R_V), lambda b, h, i, pt: (b, 0, 0)),
                   stok(R_WIDTH), sstate_spec),
        scratch_shapes=[pltpu.VMEM((1, 2 * tile), F32), pltpu.VMEM((1, 2 * tile), F32),
                        pltpu.VMEM((A_ROW, 2 * tile), F32), pltpu.VMEM((2 * tile, A_ROW), BF16),
                        pltpu.VMEM((tile, 2 * tile), F32), pltpu.VMEM((tile, 2 * tile), F32),
                        pltpu.VMEM((R_QW, R_V), F32)],
    )
    a_o, a_o_s, r_o, r_state, r_o_s, r_state_s = pl.pallas_call(
        kernel,
        grid_spec=grid_spec,
        out_shape=(jax.ShapeDtypeStruct((batch * seq, A_WIDTH), F32),
                   jax.ShapeDtypeStruct((nseq, 1, A_WIDTH), F32),
                   jax.ShapeDtypeStruct((batch * seq, R_WIDTH), F32),
                   jax.ShapeDtypeStruct((batch, R_QW, R_V), F32),
                   jax.ShapeDtypeStruct((nseq, 1, R_WIDTH), F32),
                   jax.ShapeDtypeStruct((nseq, R_QW, R_V), F32)),
        compiler_params=_cparams(("arbitrary", "arbitrary", "arbitrary")),
        name="mixers",
    )(page_table.reshape(-1), q_bf, k_bf, v_bf, *lams,
      sq.reshape(nseq, 1, A_WIDTH), k_new.reshape(nseq, 1, A_WIDTH), v_new.reshape(nseq, 1, A_WIDTH),
      bias, nbias, rq, rk, rv, dmask, q_dec, k_dec, g_rows,
      srq.reshape(nseq, 1, R_QW), srk.reshape(nseq, 1, R_QW), srv.reshape(nseq, 1, R_WIDTH), sstate, g_token,
      *([ck] * npages), *([cv] * npages))
    return (a_o, a_o_s.reshape(nseq, A_WIDTH), r_o, r_state,
            r_o_s.reshape(nseq, R_WIDTH), r_state_s)


def _ret_decay():
    return [math.log(1.0 - 2.0 ** (-5.0 - h)) for h in range(R_HEADS)]


def _ret_tables_chunk(chunk):
    log_g = jnp.log(1.0 - 2.0 ** (-5.0 - jnp.arange(R_HEADS, dtype=F32)))
    idx = jnp.arange(chunk, dtype=F32)
    diff = idx[:, None] - idx[None, :]
    dmask = jnp.where(diff >= 0, jnp.exp(jnp.maximum(diff, 0.0)[None] * log_g[:, None, None]), 0.0)
    q_dec = jnp.exp((idx + 1.0)[:, None] * log_g[None, :])
    k_dec = jnp.exp((chunk - 1.0 - idx)[:, None] * log_g[None, :])
    q_dec = jnp.repeat(q_dec, R_QK, axis=1)
    k_dec = jnp.repeat(k_dec, R_QK, axis=1)
    g_chunk = jnp.exp(chunk * log_g)
    g_rows = jnp.broadcast_to(jnp.repeat(g_chunk, R_QK)[:, None], (R_QW, R_V))
    return dmask, q_dec, k_dec, g_rows


def _ret_chunk(q_ref, k_ref, v_ref, dmask_ref, qdec_ref, kdec_ref, grow_ref, o_ref, st_ref, state_ref, first):
    q = q_ref[...]
    k = k_ref[...]
    qd = (q * qdec_ref[...]).astype(BF16)
    kd = (k * kdec_ref[...]).astype(BF16)
    qb = q.astype(BF16)
    kb = k.astype(BF16)
    vb = v_ref[...].astype(BF16)
    carried = jnp.where(first, 0.0, state_ref[...])
    for h in range(R_HEADS):
        ks = slice(h * R_QK, (h + 1) * R_QK)
        vs = slice(h * R_V, (h + 1) * R_V)
        state = carried[ks, :]
        inner = lax.dot_general(qb[:, ks], kb[:, ks], (((1,), (1,)), ((), ())),
                                preferred_element_type=F32) * dmask_ref[h]
        o = (jnp.dot(inner.astype(BF16), vb[:, vs], preferred_element_type=F32)
             + jnp.dot(qd[:, ks], state.astype(BF16), preferred_element_type=F32))
        o_ref[:, vs] = o
        upd = lax.dot_general(kd[:, ks], vb[:, vs], (((0,), (0,)), ((), ())),
                              preferred_element_type=F32)
        state_ref[ks, :] = grow_ref[ks, :] * state + upd
    st_ref[...] = state_ref[...]


def _ret_token(q_ref, k_ref, v_ref, state_ref, grow_ref, o_ref, ns_ref):
    kcol = jnp.broadcast_to(k_ref[...], (LANES, R_QW)).T
    qcol = jnp.broadcast_to(q_ref[...], (LANES, R_QW)).T
    vrows = jnp.concatenate(
        [jnp.broadcast_to(v_ref[:, h * R_V:(h + 1) * R_V], (R_QK, R_V)) for h in range(R_HEADS)], axis=0)
    new = grow_ref[...] * state_ref[...] + kcol * vrows
    ns_ref[...] = new
    qn = qcol * new
    for h in range(R_HEADS):
        o_ref[:, h * R_V:(h + 1) * R_V] = jnp.sum(qn[h * R_QK:(h + 1) * R_QK, :], axis=0, keepdims=True)


def _paged_step(q_ref, kn_ref, vn_ref, bias_ref, nbias_ref, lam, k_refs, v_refs, o_ref):
    npages = len(k_refs)
    q = q_ref[...]
    nrow = 2 * A_HEADS

    def head_rows(x, n):
        row = lax.broadcasted_iota(I32, (n, A_ROW), 0)
        out = jnp.zeros((n, A_ROW), F32)
        for h in range(A_HEADS):
            out = jnp.where(row == h, jnp.broadcast_to(x[:, h * A_ROW:(h + 1) * A_ROW], (n, A_ROW)), out)
        return out

    row8 = lax.broadcasted_iota(I32, (nrow, A_ROW), 0)
    lane8 = lax.broadcasted_iota(I32, (nrow, A_ROW), 1)
    q4 = head_rows(q, nrow)
    q8 = q4 + pltpu.roll(q4, A_HEADS, 0)
    qm = jnp.where((row8 < A_HEADS) == (lane8 < A_QK), q8, 0.0).astype(BF16)

    nt = (((1,), (1,)), ((), ()))
    bias = bias_ref[...]
    s = [lax.dot_general(qm, k_refs[j][...].astype(BF16), nt, preferred_element_type=F32) + bias
         for j in range(npages)]
    kn = head_rows(kn_ref[...], LANES).astype(BF16)
    s.append(lax.dot_general(qm, kn, nt, preferred_element_type=F32) + nbias_ref[...])
    m = functools.reduce(jnp.maximum, [jnp.max(x, axis=-1, keepdims=True) for x in s])
    p = [jnp.exp2(x - m) for x in s]
    l = functools.reduce(lambda a, b: a + b, [jnp.sum(x, axis=-1, keepdims=True) for x in p])
    inv = 1.0 / l
    vs = [v_refs[j][...].astype(BF16) for j in range(npages)] + [head_rows(vn_ref[...], LANES).astype(BF16)]
    out = jnp.zeros((nrow, A_ROW), F32)
    for pj, vj in zip(p, vs):
        pn = pj * inv
        first = lax.broadcasted_iota(I32, pn.shape, 0) < A_HEADS
        w8 = jnp.where(first, pn - lam * pltpu.roll(pn, A_HEADS, 0), 0.0).astype(BF16)
        out = out + jnp.dot(w8, vj, preferred_element_type=F32)
    for h in range(A_HEADS):
        o_ref[:, h * A_ROW:(h + 1) * A_ROW] = out[h:h + 1, :]


def _mix_tile(x_ref, a_ref, r_ref, g_ref, asub_ref, rnorm_ref, wo_ref, n2_ref,
              wrc_ref, wrh_ref, br_ref, h_ref, xn_ref, route_ref, *, lam_init):
    tm = x_ref.shape[0]
    parts = []
    for hd in range(A_HEADS):
        a = a_ref[:, hd * A_ROW:(hd + 1) * A_ROW]
        ms = jnp.mean(a * a, axis=-1, keepdims=True)
        parts.append((a * lax.rsqrt(ms + EPS) * asub_ref[...] * (1.0 - lam_init)).astype(BF16))
    for hd in range(R_HEADS):
        sl = slice(hd * R_V, (hd + 1) * R_V)
        r = r_ref[:, sl]
        ms = jnp.mean(r * r, axis=-1, keepdims=True)
        gate = g_ref[:, sl]
        gate = gate * (1.0 / (1.0 + jnp.exp(-gate)))
        parts.append((r * lax.rsqrt(ms + EPS) * rnorm_ref[...] * gate).astype(BF16))
    merged = jnp.concatenate(parts, axis=1)
    h = x_ref[...] + jnp.dot(merged, wo_ref[...], preferred_element_type=F32)
    h_ref[...] = h
    ms = jnp.mean(h * h, axis=-1, keepdims=True)
    xn = h * lax.rsqrt(ms + EPS) * n2_ref[...]
    for c in range(CHUNKS):
        xn_ref[pl.ds(c, tm, stride=CHUNKS), :] = xn[:, c * LANES:(c + 1) * LANES]
    xh = xn.astype(BF16)
    xl = (xn - xh.astype(F32)).astype(BF16)
    both = jnp.dot(xh, wrc_ref[...], preferred_element_type=F32)
    logits = (both[:, :LANES] + jnp.dot(xl, wrh_ref[...], preferred_element_type=F32)
              + both[:, LANES:]) + br_ref[...]
    lane = lax.broadcasted_iota(I32, logits.shape, 1).astype(F32)
    big = float(LANES)
    gl = jnp.where(lane < N_GROUPS, logits, NEG)
    gmax = jnp.max(gl, axis=-1, keepdims=True)
    gidx = jnp.min(jnp.where(gl == gmax, lane, big), axis=-1, keepdims=True)
    gsum = jnp.sum(jnp.where(lane < N_GROUPS, jnp.exp(gl - gmax), 0.0), axis=-1, keepdims=True)
    gprob = 1.0 / gsum
    lo = N_GROUPS + EXPERTS_PER_GROUP * gidx
    el = jnp.where((lane >= lo) & (lane < lo + EXPERTS_PER_GROUP), logits, NEG)
    v1 = jnp.max(el, axis=-1, keepdims=True)
    i1 = jnp.min(jnp.where(el == v1, lane, big), axis=-1, keepdims=True)
    el2 = jnp.where(lane == i1, NEG, el)
    v2 = jnp.max(el2, axis=-1, keepdims=True)
    i2 = jnp.min(jnp.where(el2 == v2, lane, big), axis=-1, keepdims=True)
    e = jnp.exp(v2 - v1)
    w1 = gprob / (1.0 + e)
    w2 = gprob * e / (1.0 + e)
    e1 = i1 - N_GROUPS
    e2 = i2 - N_GROUPS
    route_ref[...] = jnp.where(lane == 0, e1, jnp.where(lane == 1, e2, jnp.where(
        lane == 2, w1, jnp.where(lane == 3, w2, 0.0))))


def _mix_kernel(*refs, lam_init, n_tiles, has_tail):
    if not has_tail:
        _mix_tile(*refs, lam_init=lam_init)
        return
    ins, (th_ref, txn_ref, troute_ref), outs = refs[:11], refs[11:14], refs[14:]
    h_ref, xn_ref, route_ref = outs
    i = pl.program_id(0)

    @pl.when(i < n_tiles)
    def _():
        _mix_tile(*ins, *outs, lam_init=lam_init)

    @pl.when(i == n_tiles)
    def _():
        tm = h_ref.shape[0]
        nt = th_ref.shape[0]
        h_ref[:nt, :] = th_ref[...]
        h_ref[nt:, :] = jnp.zeros((tm - nt, D_MODEL), F32)
        xn_ref[:nt * CHUNKS, :] = txn_ref[...]
        xn_ref[nt * CHUNKS:, :] = jnp.zeros(((tm - nt) * CHUNKS, LANES), F32)
        route_ref[:nt, :] = troute_ref[...]
        route_ref[nt:, :] = jnp.zeros((tm - nt, LANES), F32)


def _mix_out(x2d, a_o, r_o, rg, asub, rnorm, wo_bf, n2, wr_cat, wr_hi, br, lam_init, tm, tail=None):
    t = x2d.shape[0]
    n_tiles = t // tm
    has_tail = tail is not None
    n_out = n_tiles + (1 if has_tail else 0)
    row = lambda w: pl.BlockSpec((tm, w), lambda i: (jnp.minimum(i, n_tiles - 1), 0))
    full = lambda a: pl.BlockSpec(a.shape, lambda i: (0,) * a.ndim)
    out_shape = (jax.ShapeDtypeStruct((n_out * tm, D_MODEL), F32),
                 jax.ShapeDtypeStruct((n_out * tm * CHUNKS, LANES), F32),
                 jax.ShapeDtypeStruct((n_out * tm, LANES), F32))
    out_specs = (pl.BlockSpec((tm, D_MODEL), lambda i: (i, 0)),
                 pl.BlockSpec((tm * CHUNKS, LANES), lambda i: (i, 0)),
                 pl.BlockSpec((tm, LANES), lambda i: (i, 0)))
    in_specs = [row(D_MODEL), row(A_WIDTH), row(R_WIDTH), row(R_WIDTH), full(asub), full(rnorm),
                full(wo_bf), full(n2), full(wr_cat), full(wr_hi), full(br)]
    args = [x2d, a_o, r_o, rg, asub, rnorm, wo_bf, n2, wr_cat, wr_hi, br]
    if has_tail:
        assert tail[0].shape[0] <= tm
        in_specs += [full(a) for a in tail]
        args += list(tail)
    kernel = functools.partial(_mix_kernel, lam_init=lam_init, n_tiles=n_tiles, has_tail=has_tail)
    return pl.pallas_call(
        kernel,
        grid=(n_out,),
        in_specs=in_specs,
        out_specs=out_specs,
        out_shape=out_shape,
        compiler_params=_cparams(("arbitrary",)),
        name="mix_out",
    )(*args)


def _plan_kernel(route_ref, ltri_ref, utri_ref, dest_ref, te_ref, meta_ref, *, n, slot_tile):
    tile = PLAN_TILE
    lane = lax.broadcasted_iota(I32, (tile, LANES), 1)

    def block(b):
        r = route_ref[pl.ds(pl.multiple_of(b * tile, tile), tile), :]
        e1 = r[:, 0:1].astype(I32)
        e2 = r[:, 1:2].astype(I32)
        return jnp.where((lane == e1) | (lane == e2), 1.0, 0.0), e1, e2

    def count(b, c):
        return c + jnp.sum(block(b)[0], axis=0, keepdims=True)

    cnt = lax.fori_loop(0, n // tile, count, jnp.zeros((1, LANES), F32), unroll=PLAN_UNROLL)
    ntile = jnp.floor((cnt + (slot_tile - 1)) * (1.0 / slot_tile))
    nt8 = jnp.broadcast_to(ntile, (SUBLANES, LANES)).astype(BF16)
    base_t = jnp.dot(nt8, utri_ref[...], preferred_element_type=F32)[0:1, :]
    base = base_t * slot_tile
    ends = base_t + ntile
    tl = lax.broadcasted_iota(I32, te_ref.shape, 0).astype(F32)
    el = lax.broadcasted_iota(I32, te_ref.shape, 1)
    hit = jnp.where((el < N_EXPERTS) & (ends <= tl), 1.0, 0.0)
    te_ref[...] = jnp.broadcast_to(jnp.sum(hit, axis=-1, keepdims=True), te_ref.shape).astype(I32)
    mrow = lax.broadcasted_iota(I32, meta_ref.shape, 0)
    meta_ref[...] = jnp.where(mrow == 0, cnt, jnp.where(mrow == 1, base, jnp.where(
        mrow == 2, ntile * slot_tile, 0.0))).astype(I32)

    def place(b, run):
        onehot, e1, e2 = block(b)
        rank = jnp.dot(ltri_ref[...], onehot.astype(BF16), preferred_element_type=F32)
        pos = base + run + rank
        d1 = jnp.sum(jnp.where(lane == e1, pos, 0.0), axis=-1, keepdims=True)
        d2 = jnp.sum(jnp.where(lane == e2, pos, 0.0), axis=-1, keepdims=True)
        dest_ref[pl.ds(pl.multiple_of(b * tile, tile), tile), :] = jnp.where(
            lane == 0, d1, jnp.where(lane == 1, d2, 0.0)).astype(I32)
        return run + jnp.sum(onehot, axis=0, keepdims=True)

    lax.fori_loop(0, n // tile, place, jnp.zeros((1, LANES), F32), unroll=PLAN_UNROLL)


def _plan(route, n, n_slot_tiles):
    tile = PLAN_TILE
    te_rows = -(-n_slot_tiles // SUBLANES) * SUBLANES
    ii = np.arange(tile)
    ltri = jnp.asarray((ii[None, :] < ii[:, None]).astype(np.float32), dtype=BF16)
    ee = np.arange(LANES)
    utri = jnp.asarray((ee[:, None] < ee[None, :]).astype(np.float32), dtype=BF16)
    kernel = functools.partial(_plan_kernel, n=n, slot_tile=SLOT_TILE)
    return pl.pallas_call(
        kernel,
        out_shape=(jax.ShapeDtypeStruct((n, LANES), I32),
                   jax.ShapeDtypeStruct((te_rows, LANES), I32),
                   jax.ShapeDtypeStruct((SUBLANES, LANES), I32)),
        compiler_params=pltpu.CompilerParams(vmem_limit_bytes=VMEM_LIMIT),
        name="plan",
    )(route, ltri, utri)


def _total_tiles(meta_ref):
    last = N_EXPERTS - 1
    return (meta_ref[LANES + last] + meta_ref[2 * LANES + last]) // SLOT_TILE


def _dispatch_kernel(dest_ref, meta_ref, xn_ref, xs_hbm, zeros_ref, zsem, ssem, *, tm, n_steps, n_total, n_tiles):
    i = pl.program_id(0)
    ts = SLOT_TILE

    def zero_tiles(start):
        def one(tile):
            cp = pltpu.make_async_copy(zeros_ref, xs_hbm.at[pl.ds(tile * ts, ts)], zsem)
            if start:
                cp.start()
            else:
                cp.wait()

        def per_expert(e, c):
            first = meta_ref[LANES + e]
            reserved = meta_ref[2 * LANES + e]

            @pl.when(meta_ref[e] < reserved)
            def _():
                one((first + reserved) // ts - 1)

            return c

        lax.fori_loop(0, N_EXPERTS, per_expert, 0)

        def trailing(t, c):
            one(t)
            return c

        lax.fori_loop(_total_tiles(meta_ref), n_tiles, trailing, 0)

    @pl.when(i == 0)
    def _():
        zeros_ref[...] = jnp.zeros(zeros_ref.shape, F32)
        zero_tiles(True)
        zero_tiles(False)

    def scatter(n_tok):
        base = 2 * i * tm

        def start(r, c):
            src = xn_ref.at[pl.ds(pl.multiple_of(r * CHUNKS, CHUNKS), CHUNKS), :]
            for k in range(2):
                pltpu.make_async_copy(src, xs_hbm.at[dest_ref[base + 2 * r + k]], ssem).start(
                    priority=k % DMA_PRIORITIES)
            return c

        lax.fori_loop(0, n_tok, start, 0, unroll=8)

        def wait(r, c):
            pltpu.make_async_copy(xn_ref.at[pl.ds(0, CHUNKS), :], xs_hbm.at[0], ssem).wait()
            return c

        lax.fori_loop(0, 2 * n_tok, wait, 0, unroll=8)

    rem = n_total - (n_steps - 1) * tm
    if rem == tm:
        scatter(tm)
    else:
        pl.when(i < n_steps - 1)(functools.partial(scatter, tm))
        pl.when(i == n_steps - 1)(functools.partial(scatter, rem))


def _dispatch(dest_flat, meta_flat, xn_pool, n_total, n_tiles):
    tm = TOKEN_TILE
    n_steps = -(-n_total // tm)
    assert xn_pool.shape[0] >= n_steps * tm * CHUNKS
    kernel = functools.partial(_dispatch_kernel, tm=tm, n_steps=n_steps, n_total=n_total, n_tiles=n_tiles)
    grid_spec = pltpu.PrefetchScalarGridSpec(
        num_scalar_prefetch=2,
        grid=(n_steps,),
        in_specs=[pl.BlockSpec((tm * CHUNKS, LANES), lambda i, d, m: (i, 0))],
        out_specs=pl.BlockSpec(memory_space=pl.ANY),
        scratch_shapes=[pltpu.VMEM((SLOT_TILE, CHUNKS, LANES), F32),
                        pltpu.SemaphoreType.DMA, pltpu.SemaphoreType.DMA],
    )
    return pl.pallas_call(
        kernel,
        grid_spec=grid_spec,
        out_shape=jax.ShapeDtypeStruct((n_tiles * SLOT_TILE, CHUNKS, LANES), F32),
        compiler_params=_cparams(("arbitrary",)),
        name="dispatch",
    )(dest_flat, meta_flat, xn_pool)


def _row_gather(src_hbm, idx_of_row, buf, sem, n_rows):
    def body(j, c):
        for k in range(DMA_PRIORITIES):
            r = DMA_PRIORITIES * j + k
            pltpu.make_async_copy(src_hbm.at[idx_of_row(r)],
                                  buf.at[pl.ds(pl.multiple_of(r * CHUNKS, CHUNKS), CHUNKS), :],
                                  sem).start(priority=k)
        return c

    assert n_rows % DMA_PRIORITIES == 0
    lax.fori_loop(0, n_rows // DMA_PRIORITIES, body, 0, unroll=8)


def _row_wait(src_hbm, buf, sem, n_rows):
    def body(r, c):
        pltpu.make_async_copy(src_hbm.at[0], buf.at[pl.ds(0, CHUNKS), :], sem).wait()
        return c

    lax.fori_loop(0, n_rows, body, 0, unroll=8)


def _gathered_rows(buf, n_rows):
    return jnp.concatenate([buf[pl.ds(c, n_rows, stride=CHUNKS), :] for c in range(CHUNKS)], axis=1)


def _expert_kernel(te_ref, meta_ref, xs_ref, wg_ref, wu_ref, wd_ref, ys_ref, wg_bf, wu_bf, wd_bf):
    i = pl.program_id(0)
    ts = SLOT_TILE
    e = te_ref[i]
    active = e < N_EXPERTS
    prev = te_ref[jnp.maximum(i - 1, 0)]

    @pl.when(active & ((i == 0) | (e != prev)))
    def _():
        wg_bf[...] = wg_ref[...].astype(BF16)
        wu_bf[...] = wu_ref[...].astype(BF16)
        wd_bf[...] = wd_ref[...].astype(BF16)

    @pl.when(active)
    def _():
        x = _gathered_rows(xs_ref, ts).astype(BF16)
        a = jnp.dot(x, wg_bf[...], preferred_element_type=F32)
        u = jnp.dot(x, wu_bf[...], preferred_element_type=F32)
        hmid = (a * (1.0 / (1.0 + jnp.exp(-a))) * u).astype(BF16)
        y = jnp.dot(hmid, wd_bf[...], preferred_element_type=F32)
        for c in range(CHUNKS):
            ys_ref[pl.ds(c, ts, stride=CHUNKS), :] = y[:, c * LANES:(c + 1) * LANES]

    @pl.when(jnp.logical_not(active))
    def _():
        ys_ref[...] = jnp.zeros(ys_ref.shape, F32)


def _experts(te, meta_flat, xs, w_gate, w_up, w_down, n_tiles):
    ts = SLOT_TILE
    wsel = lambda i, te, meta: (jnp.minimum(te[i], N_EXPERTS - 1), 0, 0)
    xsel = lambda i, te, meta: (jnp.minimum(i, _total_tiles(meta) - 1), 0)
    grid_spec = pltpu.PrefetchScalarGridSpec(
        num_scalar_prefetch=2,
        grid=(n_tiles,),
        in_specs=[pl.BlockSpec((ts * CHUNKS, LANES), xsel),
                  pl.BlockSpec((None, D_MODEL, D_EXPERT), wsel),
                  pl.BlockSpec((None, D_MODEL, D_EXPERT), wsel),
                  pl.BlockSpec((None, D_EXPERT, D_MODEL), wsel)],
        out_specs=pl.BlockSpec((ts * CHUNKS, LANES), lambda i, te, meta: (i, 0)),
        scratch_shapes=[pltpu.VMEM((D_MODEL, D_EXPERT), BF16),
                        pltpu.VMEM((D_MODEL, D_EXPERT), BF16),
                        pltpu.VMEM((D_EXPERT, D_MODEL), BF16)],
    )
    return pl.pallas_call(
        _expert_kernel,
        grid_spec=grid_spec,
        out_shape=jax.ShapeDtypeStruct((n_tiles * ts * CHUNKS, LANES), F32),
        compiler_params=_cparams(("arbitrary",)),
        name="experts",
    )(te, meta_flat, xs.reshape(n_tiles * ts * CHUNKS, LANES), w_gate, w_up, w_down)


def _combine_kernel(dest_ref, ys_hbm, h_ref, route_ref, y_ref, gbuf, sem, *, tm, n_steps, tok_off):
    i = pl.program_id(0)
    slot = i % 2
    nrow = 2 * tm

    def start(step, sl):
        base = (tok_off + step * tm) * 2
        _row_gather(ys_hbm, lambda r: dest_ref[base + r], gbuf.at[sl], sem.at[sl], nrow)

    @pl.when(i == 0)
    def _():
        start(0, 0)

    @pl.when(i + 1 < n_steps)
    def _():
        start(i + 1, 1 - slot)

    _row_wait(ys_hbm, gbuf.at[slot], sem.at[slot], nrow)
    buf = gbuf.at[slot]
    route = route_ref[...]
    w1 = route[:, 2:3]
    w2 = route[:, 3:4]
    for c in range(CHUNKS):
        g1 = buf[pl.ds(c, tm, stride=2 * CHUNKS), :]
        g2 = buf[pl.ds(CHUNKS + c, tm, stride=2 * CHUNKS), :]
        sl = slice(c * LANES, (c + 1) * LANES)
        y_ref[:, sl] = h_ref[:, sl] + w1 * g1 + w2 * g2


def _combine(dest_flat, ys3d, h_pool, route, tm, tok_off, n_tok):
    n_steps = n_tok // tm
    boff = tok_off // tm
    kernel = functools.partial(_combine_kernel, tm=tm, n_steps=n_steps, tok_off=tok_off)
    grid_spec = pltpu.PrefetchScalarGridSpec(
        num_scalar_prefetch=1,
        grid=(n_steps,),
        in_specs=[pl.BlockSpec(memory_space=pl.ANY),
                  pl.BlockSpec((tm, D_MODEL), lambda i, d: (i + boff, 0)),
                  pl.BlockSpec((tm, LANES), lambda i, d: (i + boff, 0))],
        out_specs=pl.BlockSpec((tm, D_MODEL), lambda i, d: (i, 0)),
        scratch_shapes=[pltpu.VMEM((2, 2 * tm * CHUNKS, LANES), F32), pltpu.SemaphoreType.DMA((2,))],
    )
    return pl.pallas_call(
        kernel,
        grid_spec=grid_spec,
        out_shape=jax.ShapeDtypeStruct((n_tok, D_MODEL), F32),
        compiler_params=_cparams(("arbitrary",)),
        name="combine",
    )(dest_flat, ys3d, h_pool, route)


def kernel(x_prompt, x_sample, cache_k, cache_v, state_ret, page_table, norm1, w_in, a_q_norm, a_k_norm,
           a_lambda_q1, a_lambda_k1, a_lambda_q2, a_lambda_k2, a_subln, r_norm, w_o, norm2,
           w_group_router, b_group_router, w_expert_router, b_expert_router, w_gate, w_up, w_down):
    depth = norm1.shape[0]
    assert depth == 1, "single-layer step"
    batch, seq, d = x_prompt.shape
    nsamp, tdec, _ = x_sample.shape
    assert d == D_MODEL and tdec == 1
    assert seq % TOKEN_TILE == 0 and seq % ATTN_TILE == 0
    assert nsamp % PLAN_TILE == 0
    past = page_table.shape[1] * cache_k.shape[2]
    n_prompt = batch * seq
    assert n_prompt % COMBINE_TILE == 0
    n_total = n_prompt + nsamp
    lam_init = 0.8 - 0.6 * math.exp(-0.3 * 0)
    l = 0

    lams = tuple(a[l].reshape(1, A_QK) for a in (a_lambda_q1, a_lambda_k1, a_lambda_q2, a_lambda_k2))
    w_in_bf = w_in[l].astype(BF16)
    w_o_bf = w_o[l].astype(BF16)
    g1 = norm1[l].reshape(1, D_MODEL)
    n2 = norm2[l].reshape(1, D_MODEL)
    qg = jnp.tile(a_q_norm[l], LANES // A_QK).reshape(1, LANES)
    kg = jnp.tile(a_k_norm[l], LANES // A_QK).reshape(1, LANES)
    asub = a_subln[l].reshape(1, A_ROW)
    rnorm = r_norm[l].reshape(1, R_V)
    ones = _segment_ones()
    w_r = jnp.concatenate([w_group_router[l], w_expert_router[l]], axis=1)
    w_r = jnp.pad(w_r, ((0, 0), (0, LANES - w_r.shape[1])))
    wr_hi = w_r.astype(BF16)
    wr_lo = (w_r - wr_hi.astype(F32)).astype(BF16)
    wr_cat = jnp.concatenate([wr_hi, wr_lo], axis=1)
    b_r = jnp.concatenate([b_group_router[l], b_expert_router[l]])
    b_r = jnp.pad(b_r, (0, LANES - b_r.shape[0])).reshape(1, LANES)

    pos_s = jnp.full((nsamp,), past, dtype=jnp.int32)
    xs = x_sample.reshape(nsamp, D_MODEL)
    (sq_bf, k_s, _, v_s, _, srq, srk, srv, srg) = _proj(
        xs, g1, w_in_bf, qg, kg, _rope_tables(pos_s), _ret_tables(pos_s), ones, PLAN_TILE)
    pos_p = jnp.arange(seq)
    xp = x_prompt.reshape(n_prompt, D_MODEL)
    (q_bf, k_p, k_bf, v_p, v_bf, rq, rk, rv, rg) = _proj(
        xp, g1, w_in_bf, qg, kg, _rope_tables(pos_p), _ret_tables(pos_p), ones, TOKEN_TILE)
    a_o, a_o_s, r_o, r_state_p, r_o_s, r_state_s = _mixers(
        q_bf, k_bf, v_bf, lams, batch, seq, lam_init, page_table, sq_bf.astype(F32), k_s, v_s,
        cache_k[l], cache_v[l], rq, rk, rv, srq, srk, srv, state_ret[l].reshape(nsamp, R_QW, R_V))

    sample_rows = _mix_out(xs, a_o_s, r_o_s, srg, asub, rnorm, w_o_bf, n2, wr_cat, wr_hi, b_r,
                           lam_init, PLAN_TILE)
    h_pool, xn_pool, route = _mix_out(xp, a_o, r_o, rg, asub, rnorm, w_o_bf, n2, wr_cat, wr_hi, b_r,
                                      lam_init, TOKEN_TILE, tail=sample_rows)
    n_pool = h_pool.shape[0]

    n_assign = 2 * n_total
    n_tiles = -(-(n_assign + N_EXPERTS * (SLOT_TILE - 1)) // SLOT_TILE)
    n_slots = n_tiles * SLOT_TILE
    dest128, te128, meta = _plan(route, n_total, n_tiles)
    dest_flat = dest128[:, :2].reshape(-1)
    te = te128[:n_tiles, 0]
    meta_flat = meta.reshape(-1)
    xs = _dispatch(dest_flat, meta_flat, xn_pool, n_total, n_tiles)
    ys = _experts(te, meta_flat, xs,
                  w_gate[l].reshape(N_EXPERTS, D_MODEL, D_EXPERT),
                  w_up[l].reshape(N_EXPERTS, D_MODEL, D_EXPERT),
                  w_down[l].reshape(N_EXPERTS, D_EXPERT, D_MODEL), n_tiles)
    ys3d = ys.reshape(n_slots, CHUNKS, LANES)
    y_p = _combine(dest_flat, ys3d, h_pool, route, COMBINE_TILE, 0, n_prompt)
    y_s = _combine(dest_flat, ys3d, h_pool, route, PLAN_TILE, n_prompt, nsamp)

    return (y_p.reshape(batch, seq, D_MODEL),
            y_s.reshape(nsamp, 1, D_MODEL),
            k_p.reshape(1, batch, seq, A_HEADS, A_ROW),
            v_p.reshape(1, batch, seq, A_HEADS, A_ROW),
            r_state_p.reshape(1, batch, R_HEADS, R_QK, R_V),
            k_s.reshape(1, nsamp, 1, A_HEADS, A_ROW),
            v_s.reshape(1, nsamp, 1, A_HEADS, A_ROW),
            r_state_s.reshape(1, nsamp, R_HEADS, R_QK, R_V))
```

```python
import functools
import math

import numpy as np
import jax
import jax.numpy as jnp
from jax import lax
from jax.experimental import pallas as pl
from jax.experimental.pallas import tpu as pltpu

F32 = jnp.float32
BF16 = jnp.bfloat16
I32 = jnp.int32

LANES = 128
SUBLANES = 8
CHUNKS = 8

D_MODEL = 1024
A_HEADS = 4
A_QK = 64
A_ROW = 2 * A_QK
A_WIDTH = A_HEADS * A_ROW
ROPE_THETA = 500000.0
ROPE_DIM = A_QK // 4
R_HEADS = 4
R_QK = 64
R_V = 128
R_QW = R_HEADS * R_QK
R_WIDTH = R_HEADS * R_V
R_THETA = 10000.0
N_GROUPS = 4
EXPERTS_PER_GROUP = 8
N_EXPERTS = N_GROUPS * EXPERTS_PER_GROUP
D_EXPERT = 512
EPS = 1e-6
NEG = -1e30

TOKEN_TILE = 256
COMBINE_TILE = 512
DMA_PRIORITIES = 2
ATTN_TILE = 512
ONES_ROWS = 16
LOG2E = 1.4426950408889634
PLAN_TILE = 128
PLAN_UNROLL = 3
SLOT_TILE = 256
VMEM_LIMIT = 56 * 1024 * 1024


def _cparams(sem, vmem=VMEM_LIMIT):
    return pltpu.CompilerParams(dimension_semantics=sem, vmem_limit_bytes=vmem)


def _rope_tables(pos):
    half = ROPE_DIM // 2
    inv = ROPE_THETA ** (-jnp.arange(half, dtype=F32) / half)
    ang = pos.astype(F32)[:, None] * inv[None, :]
    cos, sin = jnp.cos(ang), jnp.sin(ang)
    n = pos.shape[0]
    ones = jnp.ones((n, A_QK - ROPE_DIM), F32)
    zeros = jnp.zeros((n, A_QK - ROPE_DIM), F32)
    zh = jnp.zeros((n, half), F32)
    c = jnp.concatenate([cos, cos, ones], axis=1)
    s_next = jnp.concatenate([-sin, zh, zeros], axis=1)
    s_prev = jnp.concatenate([zh, sin, zeros], axis=1)
    rep = LANES // A_QK
    return tuple(jnp.tile(t, (1, rep)) for t in (c, s_next, s_prev))


def _ret_tables(pos):
    half = R_QK // 2
    inv = 1.0 / (R_THETA ** jnp.linspace(0.0, 1.0, half, dtype=F32))
    ang = pos.astype(F32)[:, None] * inv[None, :]
    cos, sin = jnp.cos(ang), jnp.sin(ang)
    z = jnp.zeros_like(sin)
    c = jnp.stack([cos, cos], axis=-1).reshape(-1, R_QK)
    s_next = jnp.stack([-sin, z], axis=-1).reshape(-1, R_QK)
    s_prev = jnp.stack([z, sin], axis=-1).reshape(-1, R_QK)
    rep = LANES // R_QK
    return tuple(jnp.tile(t, (1, rep)) for t in (c, s_next, s_prev))


def _segment_ones():
    seg = np.arange(LANES) // A_QK
    return jnp.asarray((seg[:, None] == seg[None, :]).astype(np.float32), dtype=BF16)


def _proj_kernel(x_ref, g1_ref, w_ref, qg_ref, kg_ref, rc_ref, rn_ref, rp_ref,
                 tc_ref, tn_ref, tp_ref, ones_ref,
                 qbf_ref, k_ref, kbf_ref, v_ref, vbf_ref, rq_ref, rk_ref, rv_ref, rg_ref):
    tm = x_ref.shape[0]
    x = x_ref[...]
    ms = jnp.mean(x * x, axis=-1, keepdims=True)
    n = (x * lax.rsqrt(ms + EPS) * g1_ref[...]).astype(BF16)
    h = jnp.dot(n, w_ref[...], preferred_element_type=F32)
    ones = ones_ref[...]
    rc, rn, rp = rc_ref[...], rn_ref[...], rp_ref[...]

    def head_norm_rope(xh, gain):
        sq = xh * xh
        hi = sq.astype(BF16)
        lo = (sq - hi.astype(F32)).astype(BF16)
        ssq = (jnp.dot(hi, ones, preferred_element_type=F32)
               + jnp.dot(lo, ones, preferred_element_type=F32))
        y = xh * lax.rsqrt(ssq * (1.0 / A_QK) + EPS) * gain
        half = ROPE_DIM // 2
        return (y * rc + pltpu.roll(y, LANES - half, 1) * rn + pltpu.roll(y, half, 1) * rp)

    for hd in range(A_HEADS):
        sl = slice(hd * A_ROW, (hd + 1) * A_ROW)
        q = head_norm_rope(h[:, sl], qg_ref[...])
        qbf_ref[:, sl] = (q * (A_QK ** -0.5 * LOG2E)).astype(BF16)
        k = head_norm_rope(h[:, A_WIDTH + hd * A_ROW:A_WIDTH + (hd + 1) * A_ROW], kg_ref[...])
        k_ref[pl.ds(hd, tm, stride=A_HEADS), :] = k
        kbf_ref[:, sl] = k.astype(BF16)
    o = 2 * A_WIDTH
    v = h[:, o:o + A_WIDTH]
    for hd in range(A_HEADS):
        v_ref[pl.ds(hd, tm, stride=A_HEADS), :] = v[:, hd * A_ROW:(hd + 1) * A_ROW]
    vbf_ref[...] = v.astype(BF16)
    o += A_WIDTH
    tc, tn, tp = tc_ref[...], tn_ref[...], tp_ref[...]

    def pair_rotate(xs):
        return xs * tc + pltpu.roll(xs, LANES - 1, 1) * tn + pltpu.roll(xs, 1, 1) * tp

    for j in range(R_QW // LANES):
        sl = slice(j * LANES, (j + 1) * LANES)
        rq_ref[:, sl] = pair_rotate(h[:, o + j * LANES:o + (j + 1) * LANES])
        rk_ref[:, sl] = pair_rotate(h[:, o + R_QW + j * LANES:o + R_QW + (j + 1) * LANES]) * (R_QK ** -0.5)
    o += 2 * R_QW
    rv_ref[...] = h[:, o:o + R_WIDTH]
    rg_ref[...] = h[:, o + R_WIDTH:o + 2 * R_WIDTH]


def _proj(x2d, g1, w_bf, qg, kg, rope_t, ret_t, ones, tm):
    t = x2d.shape[0]
    table_tiles = rope_t[0].shape[0] // tm
    row = lambda w: pl.BlockSpec((tm, w), lambda i: (i, 0))
    table = pl.BlockSpec((tm, LANES), lambda i: (i % table_tiles, 0))
    full = lambda a: pl.BlockSpec(a.shape, lambda i: (0,) * a.ndim)
    out_shape = (
        jax.ShapeDtypeStruct((t, A_WIDTH), BF16),
        jax.ShapeDtypeStruct((t * A_HEADS, A_ROW), F32),
        jax.ShapeDtypeStruct((t, A_WIDTH), BF16),
        jax.ShapeDtypeStruct((t * A_HEADS, A_ROW), F32),
        jax.ShapeDtypeStruct((t, A_WIDTH), BF16),
        jax.ShapeDtypeStruct((t, R_QW), F32),
        jax.ShapeDtypeStruct((t, R_QW), F32),
        jax.ShapeDtypeStruct((t, R_WIDTH), F32),
        jax.ShapeDtypeStruct((t, R_WIDTH), F32),
    )
    return pl.pallas_call(
        _proj_kernel,
        grid=(t // tm,),
        in_specs=[row(D_MODEL), full(g1), full(w_bf), full(qg), full(kg)]
                 + [table] * 6 + [full(ones)],
        out_specs=tuple(pl.BlockSpec((tm * s.shape[0] // t, s.shape[1]), lambda i: (i, 0)) for s in out_shape),
        out_shape=out_shape,
        compiler_params=_cparams(("parallel",)),
        name="proj",
    )(x2d, g1, w_bf, qg, kg, *rope_t, *ret_t, ones)


def _lambda(lq1_ref, lk1_ref, lq2_ref, lk2_ref, lam_init):
    s1 = jnp.sum(lq1_ref[...] * lk1_ref[...], axis=-1, keepdims=True)
    s2 = jnp.sum(lq2_ref[...] * lk2_ref[...], axis=-1, keepdims=True)
    return jnp.exp(s1) - jnp.exp(s2) + lam_init


def _attn_kernel(pt_ref, q_ref, k_ref, v_ref, lq1_ref, lk1_ref, lq2_ref, lk2_ref,
                 sq_ref, skn_ref, svn_ref, bias_ref, nbias_ref,
                 rq_ref, rk_ref, rv_ref, dmask_ref, qdec_ref, kdec_ref, grow_ref,
                 srq_ref, srk_ref, srv_ref, sstate_ref, sgrow_ref, *rest, tile, lam_init, npages):
    k_pages = rest[:npages]
    v_pages = rest[npages:2 * npages]
    (o_ref, os_ref, ro_ref, rst_ref, sro_ref, sns_ref,
     m_ref, l_ref, acc_ref, qq_ref, sa_ref, sb_ref, rstate_ref) = rest[2 * npages:]
    del pt_ref
    lam = _lambda(lq1_ref, lk1_ref, lq2_ref, lk2_ref, lam_init)
    _paged_step(sq_ref, skn_ref, svn_ref, bias_ref, nbias_ref, lam, k_pages, v_pages, os_ref)
    _ret_token(srq_ref, srk_ref, srv_ref, sstate_ref, sgrow_ref, sro_ref, sns_ref)
    first = (pl.program_id(1) == 0) & (pl.program_id(2) == 0)
    _ret_chunk(rq_ref, rk_ref, rv_ref, dmask_ref, qdec_ref, kdec_ref, grow_ref, ro_ref, rst_ref, rstate_ref, first)

    i = pl.program_id(2)
    q = q_ref[...]
    lane = lax.broadcasted_iota(I32, q.shape, 1)
    zero = jnp.zeros_like(q)
    qq = jnp.concatenate([jnp.where(lane < A_QK, q, zero), jnp.where(lane >= A_QK, q, zero)], axis=0)
    qq_ref[...] = qq
    m_ref[...] = jnp.full(m_ref.shape, NEG, F32)
    l_ref[...] = jnp.zeros(l_ref.shape, F32)
    acc_ref[...] = jnp.zeros(acc_ref.shape, F32)

    ones_rows = jnp.ones((ONES_ROWS, tile), BF16)

    def scores(j, s_ref):
        k = k_ref[pl.ds(pl.multiple_of(j * tile, tile), tile), :]
        s_ref[...] = lax.dot_general(k, qq_ref[...], (((1,), (1,)), ((), ())), preferred_element_type=F32)

    def accumulate(j, s_ref, masked):
        v = v_ref[pl.ds(pl.multiple_of(j * tile, tile), tile), :]
        vt = jnp.concatenate([v.T, ones_rows], axis=0)
        s = s_ref[...]
        if masked:
            key = lax.broadcasted_iota(I32, s.shape, 0)
            qry = lax.broadcasted_iota(I32, s.shape, 1)
            qry = jnp.where(qry >= tile, qry - tile, qry)
            s = jnp.where(key <= qry, s, NEG)
        m_prev = m_ref[...]
        m_new = jnp.maximum(m_prev, jnp.max(s, axis=0, keepdims=True))
        alpha = jnp.exp2(m_prev - m_new)
        p = jnp.exp2(s - m_new).astype(BF16)
        pv = jnp.dot(vt, p, preferred_element_type=F32)
        acc_ref[...] = alpha * acc_ref[...] + pv[:A_ROW, :]
        l_ref[...] = alpha * l_ref[...] + pv[A_ROW:A_ROW + 1, :]
        m_ref[...] = m_new

    scores(0, sa_ref)

    def pair(t, carry):
        j = 2 * t
        scores(j + 1, sb_ref)
        accumulate(j, sa_ref, False)
        scores(j + 2, sa_ref)
        accumulate(j + 1, sb_ref, False)
        return carry

    lax.fori_loop(0, i // 2, pair, 0)

    @pl.when(i % 2 == 0)
    def _():
        accumulate(i, sa_ref, True)

    @pl.when(i % 2 == 1)
    def _():
        scores(i, sb_ref)
        accumulate(i - 1, sa_ref, False)
        accumulate(i, sb_ref, True)

    o1 = acc_ref[:, :tile] / l_ref[:, :tile]
    o2 = acc_ref[:, tile:] / l_ref[:, tile:]
    o_ref[...] = (o1 - lam * o2).T


def _mixers(q_bf, k_bf, v_bf, lams, batch, seq, lam_init, page_table, sq, k_new, v_new, cache_k, cache_v,
            rq, rk, rv, srq, srk, srv, sstate):
    tile = ATTN_TILE
    nq = seq // tile
    nseq, npages = page_table.shape
    steps_per_seq = A_HEADS * nq
    assert nseq == batch * steps_per_seq, "one decode sequence per prompt query block"
    assert seq % (steps_per_seq * SUBLANES) == 0, "one retention chunk per grid step"
    chunk = seq // steps_per_seq
    dmask, q_dec, k_dec, g_rows = _ret_tables_chunk(chunk)
    _, _, _, g_token = _ret_tables_chunk(1)
    n_phys, page = cache_k.shape[0], cache_k.shape[1]
    prow = page * A_HEADS
    ck = cache_k.reshape(n_phys, prow, A_ROW)
    cv = cache_v.reshape(n_phys, prow, A_ROW)
    r = np.arange(2 * A_HEADS)[:, None] % A_HEADS
    bias = jnp.asarray(np.where(np.arange(prow)[None, :] % A_HEADS == r, 0.0, NEG).astype(np.float32))
    nbias = jnp.asarray(np.where(np.arange(LANES)[None, :] == r, 0.0, NEG).astype(np.float32))
    step = lambda b, h, i: (b * A_HEADS + h) * nq + i
    lam_spec = pl.BlockSpec((1, A_QK), lambda b, h, i, pt: (0, 0))
    tok = pl.BlockSpec((None, 1, A_WIDTH), lambda b, h, i, pt: (step(b, h, i), 0, 0))
    full = lambda a: pl.BlockSpec(a.shape, lambda b, h, i, pt: (0,) * a.ndim)

    def page_spec(jj):
        return pl.BlockSpec((None, prow, A_ROW), lambda b, h, i, pt: (pt[step(b, h, i) * npages + jj], 0, 0))

    crow = lambda w: pl.BlockSpec((chunk, w), lambda b, h, i, pt: (step(b, h, i), 0))
    stok = lambda w: pl.BlockSpec((None, 1, w), lambda b, h, i, pt: (step(b, h, i), 0, 0))
    sstate_spec = pl.BlockSpec((None, R_QW, R_V), lambda b, h, i, pt: (step(b, h, i), 0, 0))
    kernel = functools.partial(_attn_kernel, tile=tile, lam_init=lam_init, npages=npages)
    grid_spec = pltpu.PrefetchScalarGridSpec(
        num_scalar_prefetch=1,
        grid=(batch, A_HEADS, nq),
        in_specs=[pl.BlockSpec((tile, A_ROW), lambda b, h, i, pt: (b * nq + i, h)),
                  pl.BlockSpec((seq, A_ROW), lambda b, h, i, pt: (b, h)),
                  pl.BlockSpec((seq, A_ROW), lambda b, h, i, pt: (b, h))] + [lam_spec] * 4
                 + [tok, tok, tok, full(bias), full(nbias)]
                 + [crow(R_QW), crow(R_QW), crow(R_WIDTH), full(dmask), full(q_dec), full(k_dec), full(g_rows)]
                 + [stok(R_QW), stok(R_QW), stok(R_WIDTH), sstate_spec, full(g_token)]
                 + [page_spec(jj) for jj in range(npages)] * 2,
        out_specs=(pl.BlockSpec((tile, A_ROW), lambda b, h, i, pt: (b * nq + i, h)), tok,
                   crow(R_WIDTH), pl.BlockSpec((None, R_QW, R_V), lambda b, h, i, pt: (b, 0, 0)),
                   stok(R_WIDTH), sstate_spec),
        scratch_shapes=[pltpu.VMEM((1, 2 * tile), F32), pltpu.VMEM((1, 2 * tile), F32),
                        pltpu.VMEM((A_ROW, 2 * tile), F32), pltpu.VMEM((2 * tile, A_ROW), BF16),
                        pltpu.VMEM((tile, 2 * tile), F32), pltpu.VMEM((tile, 2 * tile), F32),
                        pltpu.VMEM((R_QW, R_V), F32)],
    )
    a_o, a_o_s, r_o, r_state, r_o_s, r_state_s = pl.pallas_call(
        kernel,
        grid_spec=grid_spec,
        out_shape=(jax.ShapeDtypeStruct((batch * seq, A_WIDTH), F32),
                   jax.ShapeDtypeStruct((nseq, 1, A_WIDTH), F32),
                   jax.ShapeDtypeStruct((batch * seq, R_WIDTH), F32),
                   jax.ShapeDtypeStruct((batch, R_QW, R_V), F32),
                   jax.ShapeDtypeStruct((nseq, 1, R_WIDTH), F32),
                   jax.ShapeDtypeStruct((nseq, R_QW, R_V), F32)),
        compiler_params=_cparams(("arbitrary", "arbitrary", "arbitrary")),
        name="mixers",
    )(page_table.reshape(-1), q_bf, k_bf, v_bf, *lams,
      sq.reshape(nseq, 1, A_WIDTH), k_new.reshape(nseq, 1, A_WIDTH), v_new.reshape(nseq, 1, A_WIDTH),
      bias, nbias, rq, rk, rv, dmask, q_dec, k_dec, g_rows,
      srq.reshape(nseq, 1, R_QW), srk.reshape(nseq, 1, R_QW), srv.reshape(nseq, 1, R_WIDTH), sstate, g_token,
      *([ck] * npages), *([cv] * npages))
    return (a_o, a_o_s.reshape(nseq, A_WIDTH), r_o, r_state,
            r_o_s.reshape(nseq, R_WIDTH), r_state_s)


def _ret_decay():
    return [math.log(1.0 - 2.0 ** (-5.0 - h)) for h in range(R_HEADS)]


def _ret_tables_chunk(chunk):
    log_g = jnp.log(1.0 - 2.0 ** (-5.0 - jnp.arange(R_HEADS, dtype=F32)))
    idx = jnp.arange(chunk, dtype=F32)
    diff = idx[:, None] - idx[None, :]
    dmask = jnp.where(diff >= 0, jnp.exp(jnp.maximum(diff, 0.0)[None] * log_g[:, None, None]), 0.0)
    q_dec = jnp.exp((idx + 1.0)[:, None] * log_g[None, :])
    k_dec = jnp.exp((chunk - 1.0 - idx)[:, None] * log_g[None, :])
    q_dec = jnp.repeat(q_dec, R_QK, axis=1)
    k_dec = jnp.repeat(k_dec, R_QK, axis=1)
    g_chunk = jnp.exp(chunk * log_g)
    g_rows = jnp.broadcast_to(jnp.repeat(g_chunk, R_QK)[:, None], (R_QW, R_V))
    return dmask, q_dec, k_dec, g_rows


def _ret_chunk(q_ref, k_ref, v_ref, dmask_ref, qdec_ref, kdec_ref, grow_ref, o_ref, st_ref, state_ref, first):
    q = q_ref[...]
    k = k_ref[...]
    qd = (q * qdec_ref[...]).astype(BF16)
    kd = (k * kdec_ref[...]).astype(BF16)
    qb = q.astype(BF16)
    kb = k.astype(BF16)
    vb = v_ref[...].astype(BF16)
    carried = jnp.where(first, 0.0, state_ref[...])
    for h in range(R_HEADS):
        ks = slice(h * R_QK, (h + 1) * R_QK)
        vs = slice(h * R_V, (h + 1) * R_V)
        state = carried[ks, :]
        inner = lax.dot_general(qb[:, ks], kb[:, ks], (((1,), (1,)), ((), ())),
                                preferred_element_type=F32) * dmask_ref[h]
        o = (jnp.dot(inner.astype(BF16), vb[:, vs], preferred_element_type=F32)
             + jnp.dot(qd[:, ks], state.astype(BF16), preferred_element_type=F32))
        o_ref[:, vs] = o
        upd = lax.dot_general(kd[:, ks], vb[:, vs], (((0,), (0,)), ((), ())),
                              preferred_element_type=F32)
        state_ref[ks, :] = grow_ref[ks, :] * state + upd
    st_ref[...] = state_ref[...]


def _ret_token(q_ref, k_ref, v_ref, state_ref, grow_ref, o_ref, ns_ref):
    kcol = jnp.broadcast_to(k_ref[...], (LANES, R_QW)).T
    qcol = jnp.broadcast_to(q_ref[...], (LANES, R_QW)).T
    vrows = jnp.concatenate(
        [jnp.broadcast_to(v_ref[:, h * R_V:(h + 1) * R_V], (R_QK, R_V)) for h in range(R_HEADS)], axis=0)
    new = grow_ref[...] * state_ref[...] + kcol * vrows
    ns_ref[...] = new
    qn = qcol * new
    for h in range(R_HEADS):
        o_ref[:, h * R_V:(h + 1) * R_V] = jnp.sum(qn[h * R_QK:(h + 1) * R_QK, :], axis=0, keepdims=True)


def _paged_step(q_ref, kn_ref, vn_ref, bias_ref, nbias_ref, lam, k_refs, v_refs, o_ref):
    npages = len(k_refs)
    q = q_ref[...]
    nrow = 2 * A_HEADS

    def head_rows(x, n):
        row = lax.broadcasted_iota(I32, (n, A_ROW), 0)
        out = jnp.zeros((n, A_ROW), F32)
        for h in range(A_HEADS):
            out = jnp.where(row == h, jnp.broadcast_to(x[:, h * A_ROW:(h + 1) * A_ROW], (n, A_ROW)), out)
        return out

    row8 = lax.broadcasted_iota(I32, (nrow, A_ROW), 0)
    lane8 = lax.broadcasted_iota(I32, (nrow, A_ROW), 1)
    q4 = head_rows(q, nrow)
    q8 = q4 + pltpu.roll(q4, A_HEADS, 0)
    qm = jnp.where((row8 < A_HEADS) == (lane8 < A_QK), q8, 0.0).astype(BF16)

    nt = (((1,), (1,)), ((), ()))
    bias = bias_ref[...]
    s = [lax.dot_general(qm, k_refs[j][...].astype(BF16), nt, preferred_element_type=F32) + bias
         for j in range(npages)]
    kn = head_rows(kn_ref[...], LANES).astype(BF16)
    s.append(lax.dot_general(qm, kn, nt, preferred_element_type=F32) + nbias_ref[...])
    m = functools.reduce(jnp.maximum, [jnp.max(x, axis=-1, keepdims=True) for x in s])
    p = [jnp.exp2(x - m) for x in s]
    l = functools.reduce(lambda a, b: a + b, [jnp.sum(x, axis=-1, keepdims=True) for x in p])
    inv = 1.0 / l
    vs = [v_refs[j][...].astype(BF16) for j in range(npages)] + [head_rows(vn_ref[...], LANES).astype(BF16)]
    out = jnp.zeros((nrow, A_ROW), F32)
    for pj, vj in zip(p, vs):
        pn = pj * inv
        first = lax.broadcasted_iota(I32, pn.shape, 0) < A_HEADS
        w8 = jnp.where(first, pn - lam * pltpu.roll(pn, A_HEADS, 0), 0.0).astype(BF16)
        out = out + jnp.dot(w8, vj, preferred_element_type=F32)
    for h in range(A_HEADS):
        o_ref[:, h * A_ROW:(h + 1) * A_ROW] = out[h:h + 1, :]


def _mix_tile(x_ref, a_ref, r_ref, g_ref, asub_ref, rnorm_ref, wo_ref, n2_ref,
              wrc_ref, wrh_ref, br_ref, h_ref, xn_ref, route_ref, *, lam_init):
    tm = x_ref.shape[0]
    parts = []
    for hd in range(A_HEADS):
        a = a_ref[:, hd * A_ROW:(hd + 1) * A_ROW]
        ms = jnp.mean(a * a, axis=-1, keepdims=True)
        parts.append((a * lax.rsqrt(ms + EPS) * asub_ref[...] * (1.0 - lam_init)).astype(BF16))
    for hd in range(R_HEADS):
        sl = slice(hd * R_V, (hd + 1) * R_V)
        r = r_ref[:, sl]
        ms = jnp.mean(r * r, axis=-1, keepdims=True)
        gate = g_ref[:, sl]
        gate = gate * (1.0 / (1.0 + jnp.exp(-gate)))
        parts.append((r * lax.rsqrt(ms + EPS) * rnorm_ref[...] * gate).astype(BF16))
    merged = jnp.concatenate(parts, axis=1)
    h = x_ref[...] + jnp.dot(merged, wo_ref[...], preferred_element_type=F32)
    h_ref[...] = h
    ms = jnp.mean(h * h, axis=-1, keepdims=True)
    xn = h * lax.rsqrt(ms + EPS) * n2_ref[...]
    for c in range(CHUNKS):
        xn_ref[pl.ds(c, tm, stride=CHUNKS), :] = xn[:, c * LANES:(c + 1) * LANES]
    xh = xn.astype(BF16)
    xl = (xn - xh.astype(F32)).astype(BF16)
    both = jnp.dot(xh, wrc_ref[...], preferred_element_type=F32)
    logits = (both[:, :LANES] + jnp.dot(xl, wrh_ref[...], preferred_element_type=F32)
              + both[:, LANES:]) + br_ref[...]
    lane = lax.broadcasted_iota(I32, logits.shape, 1).astype(F32)
    big = float(LANES)
    gl = jnp.where(lane < N_GROUPS, logits, NEG)
    gmax = jnp.max(gl, axis=-1, keepdims=True)
    gidx = jnp.min(jnp.where(gl == gmax, lane, big), axis=-1, keepdims=True)
    gsum = jnp.sum(jnp.where(lane < N_GROUPS, jnp.exp(gl - gmax), 0.0), axis=-1, keepdims=True)
    gprob = 1.0 / gsum
    lo = N_GROUPS + EXPERTS_PER_GROUP * gidx
    el = jnp.where((lane >= lo) & (lane < lo + EXPERTS_PER_GROUP), logits, NEG)
    v1 = jnp.max(el, axis=-1, keepdims=True)
    i1 = jnp.min(jnp.where(el == v1, lane, big), axis=-1, keepdims=True)
    el2 = jnp.where(lane == i1, NEG, el)
    v2 = jnp.max(el2, axis=-1, keepdims=True)
    i2 = jnp.min(jnp.where(el2 == v2, lane, big), axis=-1, keepdims=True)
    e = jnp.exp(v2 - v1)
    w1 = gprob / (1.0 + e)
    w2 = gprob * e / (1.0 + e)
    e1 = i1 - N_GROUPS
    e2 = i2 - N_GROUPS
    route_ref[...] = jnp.where(lane == 0, e1, jnp.where(lane == 1, e2, jnp.where(
        lane == 2, w1, jnp.where(lane == 3, w2, 0.0))))


def _mix_kernel(*refs, lam_init, n_tiles, has_tail):
    if not has_tail:
        _mix_tile(*refs, lam_init=lam_init)
        return
    ins, (th_ref, txn_ref, troute_ref), outs = refs[:11], refs[11:14], refs[14:]
    h_ref, xn_ref, route_ref = outs
    i = pl.program_id(0)

    @pl.when(i < n_tiles)
    def _():
        _mix_tile(*ins, *outs, lam_init=lam_init)

    @pl.when(i == n_tiles)
    def _():
        tm = h_ref.shape[0]
        nt = th_ref.shape[0]
        h_ref[:nt, :] = th_ref[...]
        h_ref[nt:, :] = jnp.zeros((tm - nt, D_MODEL), F32)
        xn_ref[:nt * CHUNKS, :] = txn_ref[...]
        xn_ref[nt * CHUNKS:, :] = jnp.zeros(((tm - nt) * CHUNKS, LANES), F32)
        route_ref[:nt, :] = troute_ref[...]
        route_ref[nt:, :] = jnp.zeros((tm - nt, LANES), F32)


def _mix_out(x2d, a_o, r_o, rg, asub, rnorm, wo_bf, n2, wr_cat, wr_hi, br, lam_init, tm, tail=None):
    t = x2d.shape[0]
    n_tiles = t // tm
    has_tail = tail is not None
    n_out = n_tiles + (1 if has_tail else 0)
    row = lambda w: pl.BlockSpec((tm, w), lambda i: (jnp.minimum(i, n_tiles - 1), 0))
    full = lambda a: pl.BlockSpec(a.shape, lambda i: (0,) * a.ndim)
    out_shape = (jax.ShapeDtypeStruct((n_out * tm, D_MODEL), F32),
                 jax.ShapeDtypeStruct((n_out * tm * CHUNKS, LANES), F32),
                 jax.ShapeDtypeStruct((n_out * tm, LANES), F32))
    out_specs = (pl.BlockSpec((tm, D_MODEL), lambda i: (i, 0)),
                 pl.BlockSpec((tm * CHUNKS, LANES), lambda i: (i, 0)),
                 pl.BlockSpec((tm, LANES), lambda i: (i, 0)))
    in_specs = [row(D_MODEL), row(A_WIDTH), row(R_WIDTH), row(R_WIDTH), full(asub), full(rnorm),
                full(wo_bf), full(n2), full(wr_cat), full(wr_hi), full(br)]
    args = [x2d, a_o, r_o, rg, asub, rnorm, wo_bf, n2, wr_cat, wr_hi, br]
    if has_tail:
        assert tail[0].shape[0] <= tm
        in_specs += [full(a) for a in tail]
        args += list(tail)
    kernel = functools.partial(_mix_kernel, lam_init=lam_init, n_tiles=n_tiles, has_tail=has_tail)
    return pl.pallas_call(
        kernel,
        grid=(n_out,),
        in_specs=in_specs,
        out_specs=out_specs,
        out_shape=out_shape,
        compiler_params=_cparams(("arbitrary",)),
        name="mix_out",
    )(*args)


def _plan_kernel(route_ref, ltri_ref, utri_ref, dest_ref, te_ref, meta_ref, *, n, slot_tile):
    tile = PLAN_TILE
    lane = lax.broadcasted_iota(I32, (tile, LANES), 1)

    def block(b):
        r = route_ref[pl.ds(pl.multiple_of(b * tile, tile), tile), :]
        e1 = r[:, 0:1].astype(I32)
        e2 = r[:, 1:2].astype(I32)
        return jnp.where((lane == e1) | (lane == e2), 1.0, 0.0), e1, e2

    def count(b, c):
        return c + jnp.sum(block(b)[0], axis=0, keepdims=True)

    cnt = lax.fori_loop(0, n // tile, count, jnp.zeros((1, LANES), F32), unroll=PLAN_UNROLL)
    ntile = jnp.floor((cnt + (slot_tile - 1)) * (1.0 / slot_tile))
    nt8 = jnp.broadcast_to(ntile, (SUBLANES, LANES)).astype(BF16)
    base_t = jnp.dot(nt8, utri_ref[...], preferred_element_type=F32)[0:1, :]
    base = base_t * slot_tile
    ends = base_t + ntile
    tl = lax.broadcasted_iota(I32, te_ref.shape, 0).astype(F32)
    el = lax.broadcasted_iota(I32, te_ref.shape, 1)
    hit = jnp.where((el < N_EXPERTS) & (ends <= tl), 1.0, 0.0)
    te_ref[...] = jnp.broadcast_to(jnp.sum(hit, axis=-1, keepdims=True), te_ref.shape).astype(I32)
    mrow = lax.broadcasted_iota(I32, meta_ref.shape, 0)
    meta_ref[...] = jnp.where(mrow == 0, cnt, jnp.where(mrow == 1, base, jnp.where(
        mrow == 2, ntile * slot_tile, 0.0))).astype(I32)

    def place(b, run):
        onehot, e1, e2 = block(b)
        rank = jnp.dot(ltri_ref[...], onehot.astype(BF16), preferred_element_type=F32)
        pos = base + run + rank
        d1 = jnp.sum(jnp.where(lane == e1, pos, 0.0), axis=-1, keepdims=True)
        d2 = jnp.sum(jnp.where(lane == e2, pos, 0.0), axis=-1, keepdims=True)
        dest_ref[pl.ds(pl.multiple_of(b * tile, tile), tile), :] = jnp.where(
            lane == 0, d1, jnp.where(lane == 1, d2, 0.0)).astype(I32)
        return run + jnp.sum(onehot, axis=0, keepdims=True)

    lax.fori_loop(0, n // tile, place, jnp.zeros((1, LANES), F32), unroll=PLAN_UNROLL)


def _plan(route, n, n_slot_tiles):
    tile = PLAN_TILE
    te_rows = -(-n_slot_tiles // SUBLANES) * SUBLANES
    ii = np.arange(tile)
    ltri = jnp.asarray((ii[None, :] < ii[:, None]).astype(np.float32), dtype=BF16)
    ee = np.arange(LANES)
    utri = jnp.asarray((ee[:, None] < ee[None, :]).astype(np.float32), dtype=BF16)
    kernel = functools.partial(_plan_kernel, n=n, slot_tile=SLOT_TILE)
    return pl.pallas_call(
        kernel,
        out_shape=(jax.ShapeDtypeStruct((n, LANES), I32),
                   jax.ShapeDtypeStruct((te_rows, LANES), I32),
                   jax.ShapeDtypeStruct((SUBLANES, LANES), I32)),
        compiler_params=pltpu.CompilerParams(vmem_limit_bytes=VMEM_LIMIT),
        name="plan",
    )(route, ltri, utri)


def _total_tiles(meta_ref):
    last = N_EXPERTS - 1
    return (meta_ref[LANES + last] + meta_ref[2 * LANES + last]) // SLOT_TILE


def _dispatch_kernel(dest_ref, meta_ref, xn_ref, xs_hbm, zeros_ref, zsem, ssem, *, tm, n_steps, n_total, n_tiles):
    i = pl.program_id(0)
    ts = SLOT_TILE

    def zero_tiles(start):
        def one(tile):
            cp = pltpu.make_async_copy(zeros_ref, xs_hbm.at[pl.ds(tile * ts, ts)], zsem)
            if start:
                cp.start()
            else:
                cp.wait()

        def per_expert(e, c):
            first = meta_ref[LANES + e]
            reserved = meta_ref[2 * LANES + e]

            @pl.when(meta_ref[e] < reserved)
            def _():
                one((first + reserved) // ts - 1)

            return c

        lax.fori_loop(0, N_EXPERTS, per_expert, 0)

        def trailing(t, c):
            one(t)
            return c

        lax.fori_loop(_total_tiles(meta_ref), n_tiles, trailing, 0)

    @pl.when(i == 0)
    def _():
        zeros_ref[...] = jnp.zeros(zeros_ref.shape, F32)
        zero_tiles(True)
        zero_tiles(False)

    def scatter(n_tok):
        base = 2 * i * tm

        def start(r, c):
            src = xn_ref.at[pl.ds(pl.multiple_of(r * CHUNKS, CHUNKS), CHUNKS), :]
            for k in range(2):
                pltpu.make_async_copy(src, xs_hbm.at[dest_ref[base + 2 * r + k]], ssem).start(
                    priority=k % DMA_PRIORITIES)
            return c

        lax.fori_loop(0, n_tok, start, 0, unroll=8)

        def wait(r, c):
            pltpu.make_async_copy(xn_ref.at[pl.ds(0, CHUNKS), :], xs_hbm.at[0], ssem).wait()
            return c

        lax.fori_loop(0, 2 * n_tok, wait, 0, unroll=8)

    rem = n_total - (n_steps - 1) * tm
    if rem == tm:
        scatter(tm)
    else:
        pl.when(i < n_steps - 1)(functools.partial(scatter, tm))
        pl.when(i == n_steps - 1)(functools.partial(scatter, rem))


def _dispatch(dest_flat, meta_flat, xn_pool, n_total, n_tiles):
    tm = TOKEN_TILE
    n_steps = -(-n_total // tm)
    assert xn_pool.shape[0] >= n_steps * tm * CHUNKS
    kernel = functools.partial(_dispatch_kernel, tm=tm, n_steps=n_steps, n_total=n_total, n_tiles=n_tiles)
    grid_spec = pltpu.PrefetchScalarGridSpec(
        num_scalar_prefetch=2,
        grid=(n_steps,),
        in_specs=[pl.BlockSpec((tm * CHUNKS, LANES), lambda i, d, m: (i, 0))],
        out_specs=pl.BlockSpec(memory_space=pl.ANY),
        scratch_shapes=[pltpu.VMEM((SLOT_TILE, CHUNKS, LANES), F32),
                        pltpu.SemaphoreType.DMA, pltpu.SemaphoreType.DMA],
    )
    return pl.pallas_call(
        kernel,
        grid_spec=grid_spec,
        out_shape=jax.ShapeDtypeStruct((n_tiles * SLOT_TILE, CHUNKS, LANES), F32),
        compiler_params=_cparams(("arbitrary",)),
        name="dispatch",
    )(dest_flat, meta_flat, xn_pool)


def _row_gather(src_hbm, idx_of_row, buf, sem, n_rows, alternate=True):
    def body(j, c):
        for k in range(DMA_PRIORITIES):
            r = DMA_PRIORITIES * j + k
            pltpu.make_async_copy(src_hbm.at[idx_of_row(r)],
                                  buf.at[pl.ds(pl.multiple_of(r * CHUNKS, CHUNKS), CHUNKS), :],
                                  sem).start(priority=k if alternate else 0)
        return c

    assert n_rows % DMA_PRIORITIES == 0
    lax.fori_loop(0, n_rows // DMA_PRIORITIES, body, 0, unroll=8)


def _row_wait(src_hbm, buf, sem, n_rows):
    def body(r, c):
        pltpu.make_async_copy(src_hbm.at[0], buf.at[pl.ds(0, CHUNKS), :], sem).wait()
        return c

    lax.fori_loop(0, n_rows, body, 0, unroll=8)


def _gathered_rows(buf, n_rows):
    return jnp.concatenate([buf[pl.ds(c, n_rows, stride=CHUNKS), :] for c in range(CHUNKS)], axis=1)


def _expert_kernel(te_ref, meta_ref, xs_ref, wg_ref, wu_ref, wd_ref, ys_ref, wg_bf, wu_bf, wd_bf):
    i = pl.program_id(0)
    ts = SLOT_TILE
    e = te_ref[i]
    active = e < N_EXPERTS
    prev = te_ref[jnp.maximum(i - 1, 0)]

    @pl.when(active & ((i == 0) | (e != prev)))
    def _():
        wg_bf[...] = wg_ref[...].astype(BF16)
        wu_bf[...] = wu_ref[...].astype(BF16)
        wd_bf[...] = wd_ref[...].astype(BF16)

    @pl.when(active)
    def _():
        x = _gathered_rows(xs_ref, ts).astype(BF16)
        a = jnp.dot(x, wg_bf[...], preferred_element_type=F32)
        u = jnp.dot(x, wu_bf[...], preferred_element_type=F32)
        hmid = (a * (1.0 / (1.0 + jnp.exp(-a))) * u).astype(BF16)
        y = jnp.dot(hmid, wd_bf[...], preferred_element_type=F32)
        for c in range(CHUNKS):
            ys_ref[pl.ds(c, ts, stride=CHUNKS), :] = y[:, c * LANES:(c + 1) * LANES]

    @pl.when(jnp.logical_not(active))
    def _():
        ys_ref[...] = jnp.zeros(ys_ref.shape, F32)


def _experts(te, meta_flat, xs, w_gate, w_up, w_down, n_tiles):
    ts = SLOT_TILE
    wsel = lambda i, te, meta: (jnp.minimum(te[i], N_EXPERTS - 1), 0, 0)
    xsel = lambda i, te, meta: (jnp.minimum(i, _total_tiles(meta) - 1), 0)
    grid_spec = pltpu.PrefetchScalarGridSpec(
        num_scalar_prefetch=2,
        grid=(n_tiles,),
        in_specs=[pl.BlockSpec((ts * CHUNKS, LANES), xsel),
                  pl.BlockSpec((None, D_MODEL, D_EXPERT), wsel),
                  pl.BlockSpec((None, D_MODEL, D_EXPERT), wsel),
                  pl.BlockSpec((None, D_EXPERT, D_MODEL), wsel)],
        out_specs=pl.BlockSpec((ts * CHUNKS, LANES), lambda i, te, meta: (i, 0)),
        scratch_shapes=[pltpu.VMEM((D_MODEL, D_EXPERT), BF16),
                        pltpu.VMEM((D_MODEL, D_EXPERT), BF16),
                        pltpu.VMEM((D_EXPERT, D_MODEL), BF16)],
    )
    return pl.pallas_call(
        _expert_kernel,
        grid_spec=grid_spec,
        out_shape=jax.ShapeDtypeStruct((n_tiles * ts * CHUNKS, LANES), F32),
        compiler_params=_cparams(("arbitrary",)),
        name="experts",
    )(te, meta_flat, xs.reshape(n_tiles * ts * CHUNKS, LANES), w_gate, w_up, w_down)


def _combine_kernel(dest_ref, ys_hbm, h_hbm, route_ref, y_ref, gbuf, sem, hbuf, hsem, *, tm, n_steps, tok_off):
    i = pl.program_id(0)
    slot = i % 2
    nrow = 2 * tm

    def h_copy(step, sl):
        rows = pl.ds(pl.multiple_of(tok_off + step * tm, tm), tm)
        return pltpu.make_async_copy(h_hbm.at[rows, :], hbuf.at[sl], hsem.at[sl])

    def start(step, sl):
        h_copy(step, sl).start(priority=DMA_PRIORITIES - 1)
        base = (tok_off + step * tm) * 2
        _row_gather(ys_hbm, lambda r: dest_ref[base + r], gbuf.at[sl], sem.at[sl], nrow, alternate=False)

    @pl.when(i == 0)
    def _():
        start(0, 0)

    @pl.when(i + 1 < n_steps)
    def _():
        start(i + 1, 1 - slot)

    _row_wait(ys_hbm, gbuf.at[slot], sem.at[slot], nrow)
    h_copy(i, slot).wait()
    buf = gbuf.at[slot]
    h_tile = hbuf.at[slot]
    route = route_ref[...]
    w1 = route[:, 2:3]
    w2 = route[:, 3:4]
    for c in range(CHUNKS):
        g1 = buf[pl.ds(c, tm, stride=2 * CHUNKS), :]
        g2 = buf[pl.ds(CHUNKS + c, tm, stride=2 * CHUNKS), :]
        sl = slice(c * LANES, (c + 1) * LANES)
        y_ref[:, sl] = h_tile[:, sl] + w1 * g1 + w2 * g2


def _combine(dest_flat, ys3d, h_pool, route, tm, tok_off, n_tok):
    n_steps = n_tok // tm
    boff = tok_off // tm
    kernel = functools.partial(_combine_kernel, tm=tm, n_steps=n_steps, tok_off=tok_off)
    grid_spec = pltpu.PrefetchScalarGridSpec(
        num_scalar_prefetch=1,
        grid=(n_steps,),
        in_specs=[pl.BlockSpec(memory_space=pl.ANY),
                  pl.BlockSpec(memory_space=pl.ANY),
                  pl.BlockSpec((tm, LANES), lambda i, d: (i + boff, 0))],
        out_specs=pl.BlockSpec((tm, D_MODEL), lambda i, d: (i, 0)),
        scratch_shapes=[pltpu.VMEM((2, 2 * tm * CHUNKS, LANES), F32), pltpu.SemaphoreType.DMA((2,)),
                        pltpu.VMEM((2, tm, D_MODEL), F32), pltpu.SemaphoreType.DMA((2,))],
    )
    return pl.pallas_call(
        kernel,
        grid_spec=grid_spec,
        out_shape=jax.ShapeDtypeStruct((n_tok, D_MODEL), F32),
        compiler_params=_cparams(("arbitrary",)),
        name="combine",
    )(dest_flat, ys3d, h_pool, route)


def kernel(x_prompt, x_sample, cache_k, cache_v, state_ret, page_table, norm1, w_in, a_q_norm, a_k_norm,
           a_lambda_q1, a_lambda_k1, a_lambda_q2, a_lambda_k2, a_subln, r_norm, w_o, norm2,
           w_group_router, b_group_router, w_expert_router, b_expert_router, w_gate, w_up, w_down):
    depth = norm1.shape[0]
    assert depth == 1, "single-layer step"
    batch, seq, d = x_prompt.shape
    nsamp, tdec, _ = x_sample.shape
    assert d == D_MODEL and tdec == 1
    assert seq % TOKEN_TILE == 0 and seq % ATTN_TILE == 0
    assert nsamp % PLAN_TILE == 0
    past = page_table.shape[1] * cache_k.shape[2]
    n_prompt = batch * seq
    assert n_prompt % COMBINE_TILE == 0
    n_total = n_prompt + nsamp
    lam_init = 0.8 - 0.6 * math.exp(-0.3 * 0)
    l = 0

    lams = tuple(a[l].reshape(1, A_QK) for a in (a_lambda_q1, a_lambda_k1, a_lambda_q2, a_lambda_k2))
    w_in_bf = w_in[l].astype(BF16)
    w_o_bf = w_o[l].astype(BF16)
    g1 = norm1[l].reshape(1, D_MODEL)
    n2 = norm2[l].reshape(1, D_MODEL)
    qg = jnp.tile(a_q_norm[l], LANES // A_QK).reshape(1, LANES)
    kg = jnp.tile(a_k_norm[l], LANES // A_QK).reshape(1, LANES)
    asub = a_subln[l].reshape(1, A_ROW)
    rnorm = r_norm[l].reshape(1, R_V)
    ones = _segment_ones()
    w_r = jnp.concatenate([w_group_router[l], w_expert_router[l]], axis=1)
    w_r = jnp.pad(w_r, ((0, 0), (0, LANES - w_r.shape[1])))
    wr_hi = w_r.astype(BF16)
    wr_lo = (w_r - wr_hi.astype(F32)).astype(BF16)
    wr_cat = jnp.concatenate([wr_hi, wr_lo], axis=1)
    b_r = jnp.concatenate([b_group_router[l], b_expert_router[l]])
    b_r = jnp.pad(b_r, (0, LANES - b_r.shape[0])).reshape(1, LANES)

    pos_s = jnp.full((nsamp,), past, dtype=jnp.int32)
    xs = x_sample.reshape(nsamp, D_MODEL)
    (sq_bf, k_s, _, v_s, _, srq, srk, srv, srg) = _proj(
        xs, g1, w_in_bf, qg, kg, _rope_tables(pos_s), _ret_tables(pos_s), ones, PLAN_TILE)
    pos_p = jnp.arange(seq)
    xp = x_prompt.reshape(n_prompt, D_MODEL)
    (q_bf, k_p, k_bf, v_p, v_bf, rq, rk, rv, rg) = _proj(
        xp, g1, w_in_bf, qg, kg, _rope_tables(pos_p), _ret_tables(pos_p), ones, TOKEN_TILE)
    a_o, a_o_s, r_o, r_state_p, r_o_s, r_state_s = _mixers(
        q_bf, k_bf, v_bf, lams, batch, seq, lam_init, page_table, sq_bf.astype(F32), k_s, v_s,
        cache_k[l], cache_v[l], rq, rk, rv, srq, srk, srv, state_ret[l].reshape(nsamp, R_QW, R_V))

    sample_rows = _mix_out(xs, a_o_s, r_o_s, srg, asub, rnorm, w_o_bf, n2, wr_cat, wr_hi, b_r,
                           lam_init, PLAN_TILE)
    h_pool, xn_pool, route = _mix_out(xp, a_o, r_o, rg, asub, rnorm, w_o_bf, n2, wr_cat, wr_hi, b_r,
                                      lam_init, TOKEN_TILE, tail=sample_rows)
    n_pool = h_pool.shape[0]

    n_assign = 2 * n_total
    n_tiles = -(-(n_assign + N_EXPERTS * (SLOT_TILE - 1)) // SLOT_TILE)
    n_slots = n_tiles * SLOT_TILE
    dest128, te128, meta = _plan(route, n_total, n_tiles)
    dest_flat = dest128[:, :2].reshape(-1)
    te = te128[:n_tiles, 0]
    meta_flat = meta.reshape(-1)
    xs = _dispatch(dest_flat, meta_flat, xn_pool, n_total, n_tiles)
    ys = _experts(te, meta_flat, xs,
                  w_gate[l].reshape(N_EXPERTS, D_MODEL, D_EXPERT),
                  w_up[l].reshape(N_EXPERTS, D_MODEL, D_EXPERT),
                  w_down[l].reshape(N_EXPERTS, D_EXPERT, D_MODEL), n_tiles)
    ys3d = ys.reshape(n_slots, CHUNKS, LANES)
    y_p = _combine(dest_flat, ys3d, h_pool, route, COMBINE_TILE, 0, n_prompt)
    y_s = _combine(dest_flat, ys3d, h_pool, route, PLAN_TILE, n_prompt, nsamp)

    return (y_p.reshape(batch, seq, D_MODEL),
            y_s.reshape(nsamp, 1, D_MODEL),
            k_p.reshape(1, batch, seq, A_HEADS, A_ROW),
            v_p.reshape(1, batch, seq, A_HEADS, A_ROW),
            r_state_p.reshape(1, batch, R_HEADS, R_QK, R_V),
            k_s.reshape(1, nsamp, 1, A_HEADS, A_ROW),
            v_s.reshape(1, nsamp, 1, A_HEADS, A_ROW),
            r_state_s.reshape(1, nsamp, R_HEADS, R_QK, R_V))
```

```python
import functools
import math

import numpy as np
import jax
import jax.numpy as jnp
from jax import lax
from jax.experimental import pallas as pl
from jax.experimental.pallas import tpu as pltpu

F32 = jnp.float32
BF16 = jnp.bfloat16
I32 = jnp.int32

LANES = 128
SUBLANES = 8
CHUNKS = 8

D_MODEL = 1024
A_HEADS = 4
A_QK = 64
A_ROW = 2 * A_QK
A_WIDTH = A_HEADS * A_ROW
ROPE_THETA = 500000.0
ROPE_DIM = A_QK // 4
R_HEADS = 4
R_QK = 64
R_V = 128
R_QW = R_HEADS * R_QK
R_WIDTH = R_HEADS * R_V
R_THETA = 10000.0
N_GROUPS = 4
EXPERTS_PER_GROUP = 8
N_EXPERTS = N_GROUPS * EXPERTS_PER_GROUP
D_EXPERT = 512
EPS = 1e-6
NEG = -1e30

TOKEN_TILE = 512
COMBINE_TILE = 512
DMA_PRIORITIES = 2
ATTN_TILE = 512
ONES_ROWS = 16
LOG2E = 1.4426950408889634
PLAN_TILE = 128
PLAN_UNROLL = 3
SLOT_TILE = 256
VMEM_LIMIT = 56 * 1024 * 1024


def _cparams(sem, vmem=VMEM_LIMIT):
    return pltpu.CompilerParams(dimension_semantics=sem, vmem_limit_bytes=vmem)


def _rope_tables(pos):
    half = ROPE_DIM // 2
    inv = ROPE_THETA ** (-jnp.arange(half, dtype=F32) / half)
    ang = pos.astype(F32)[:, None] * inv[None, :]
    cos, sin = jnp.cos(ang), jnp.sin(ang)
    n = pos.shape[0]
    ones = jnp.ones((n, A_QK - ROPE_DIM), F32)
    zeros = jnp.zeros((n, A_QK - ROPE_DIM), F32)
    zh = jnp.zeros((n, half), F32)
    c = jnp.concatenate([cos, cos, ones], axis=1)
    s_next = jnp.concatenate([-sin, zh, zeros], axis=1)
    s_prev = jnp.concatenate([zh, sin, zeros], axis=1)
    rep = LANES // A_QK
    return tuple(jnp.tile(t, (1, rep)) for t in (c, s_next, s_prev))


def _ret_tables(pos):
    half = R_QK // 2
    inv = 1.0 / (R_THETA ** jnp.linspace(0.0, 1.0, half, dtype=F32))
    ang = pos.astype(F32)[:, None] * inv[None, :]
    cos, sin = jnp.cos(ang), jnp.sin(ang)
    z = jnp.zeros_like(sin)
    c = jnp.stack([cos, cos], axis=-1).reshape(-1, R_QK)
    s_next = jnp.stack([-sin, z], axis=-1).reshape(-1, R_QK)
    s_prev = jnp.stack([z, sin], axis=-1).reshape(-1, R_QK)
    rep = LANES // R_QK
    return tuple(jnp.tile(t, (1, rep)) for t in (c, s_next, s_prev))


def _segment_ones():
    seg = np.arange(LANES) // A_QK
    return jnp.asarray((seg[:, None] == seg[None, :]).astype(np.float32), dtype=BF16)


def _proj_kernel(x_ref, g1_ref, w_ref, qg_ref, kg_ref, rc_ref, rn_ref, rp_ref,
                 tc_ref, tn_ref, tp_ref, ones_ref,
                 qbf_ref, k_ref, kbf_ref, v_ref, vbf_ref, rq_ref, rk_ref, rv_ref, rg_ref):
    tm = x_ref.shape[0]
    x = x_ref[...]
    ms = jnp.mean(x * x, axis=-1, keepdims=True)
    n = (x * lax.rsqrt(ms + EPS) * g1_ref[...]).astype(BF16)
    h = jnp.dot(n, w_ref[...], preferred_element_type=F32)
    ones = ones_ref[...]
    rc, rn, rp = rc_ref[...], rn_ref[...], rp_ref[...]

    def head_norm_rope(xh, gain):
        sq = xh * xh
        hi = sq.astype(BF16)
        lo = (sq - hi.astype(F32)).astype(BF16)
        ssq = (jnp.dot(hi, ones, preferred_element_type=F32)
               + jnp.dot(lo, ones, preferred_element_type=F32))
        y = xh * lax.rsqrt(ssq * (1.0 / A_QK) + EPS) * gain
        half = ROPE_DIM // 2
        return (y * rc + pltpu.roll(y, LANES - half, 1) * rn + pltpu.roll(y, half, 1) * rp)

    for hd in range(A_HEADS):
        sl = slice(hd * A_ROW, (hd + 1) * A_ROW)
        q = head_norm_rope(h[:, sl], qg_ref[...])
        qbf_ref[:, sl] = (q * (A_QK ** -0.5 * LOG2E)).astype(BF16)
        k = head_norm_rope(h[:, A_WIDTH + hd * A_ROW:A_WIDTH + (hd + 1) * A_ROW], kg_ref[...])
        k_ref[pl.ds(hd, tm, stride=A_HEADS), :] = k
        kbf_ref[:, sl] = k.astype(BF16)
    o = 2 * A_WIDTH
    v = h[:, o:o + A_WIDTH]
    for hd in range(A_HEADS):
        v_ref[pl.ds(hd, tm, stride=A_HEADS), :] = v[:, hd * A_ROW:(hd + 1) * A_ROW]
    vbf_ref[...] = v.astype(BF16)
    o += A_WIDTH
    tc, tn, tp = tc_ref[...], tn_ref[...], tp_ref[...]

    def pair_rotate(xs):
        return xs * tc + pltpu.roll(xs, LANES - 1, 1) * tn + pltpu.roll(xs, 1, 1) * tp

    for j in range(R_QW // LANES):
        sl = slice(j * LANES, (j + 1) * LANES)
        rq_ref[:, sl] = pair_rotate(h[:, o + j * LANES:o + (j + 1) * LANES])
        rk_ref[:, sl] = pair_rotate(h[:, o + R_QW + j * LANES:o + R_QW + (j + 1) * LANES]) * (R_QK ** -0.5)
    o += 2 * R_QW
    rv_ref[...] = h[:, o:o + R_WIDTH]
    rg_ref[...] = h[:, o + R_WIDTH:o + 2 * R_WIDTH]


def _proj(x2d, g1, w_bf, qg, kg, rope_t, ret_t, ones, tm):
    t = x2d.shape[0]
    table_tiles = rope_t[0].shape[0] // tm
    row = lambda w: pl.BlockSpec((tm, w), lambda i: (i, 0))
    table = pl.BlockSpec((tm, LANES), lambda i: (i % table_tiles, 0))
    full = lambda a: pl.BlockSpec(a.shape, lambda i: (0,) * a.ndim)
    out_shape = (
        jax.ShapeDtypeStruct((t, A_WIDTH), BF16),
        jax.ShapeDtypeStruct((t * A_HEADS, A_ROW), F32),
        jax.ShapeDtypeStruct((t, A_WIDTH), BF16),
        jax.ShapeDtypeStruct((t * A_HEADS, A_ROW), F32),
        jax.ShapeDtypeStruct((t, A_WIDTH), BF16),
        jax.ShapeDtypeStruct((t, R_QW), F32),
        jax.ShapeDtypeStruct((t, R_QW), F32),
        jax.ShapeDtypeStruct((t, R_WIDTH), F32),
        jax.ShapeDtypeStruct((t, R_WIDTH), F32),
    )
    return pl.pallas_call(
        _proj_kernel,
        grid=(t // tm,),
        in_specs=[row(D_MODEL), full(g1), full(w_bf), full(qg), full(kg)]
                 + [table] * 6 + [full(ones)],
        out_specs=tuple(pl.BlockSpec((tm * s.shape[0] // t, s.shape[1]), lambda i: (i, 0)) for s in out_shape),
        out_shape=out_shape,
        compiler_params=_cparams(("parallel",)),
        name="proj",
    )(x2d, g1, w_bf, qg, kg, *rope_t, *ret_t, ones)


def _lambda(lq1_ref, lk1_ref, lq2_ref, lk2_ref, lam_init):
    s1 = jnp.sum(lq1_ref[...] * lk1_ref[...], axis=-1, keepdims=True)
    s2 = jnp.sum(lq2_ref[...] * lk2_ref[...], axis=-1, keepdims=True)
    return jnp.exp(s1) - jnp.exp(s2) + lam_init


def _attn_kernel(pt_ref, q_ref, k_ref, v_ref, lq1_ref, lk1_ref, lq2_ref, lk2_ref,
                 sq_ref, skn_ref, svn_ref, bias_ref, nbias_ref,
                 rq_ref, rk_ref, rv_ref, dmask_ref, qdec_ref, kdec_ref, grow_ref,
                 srq_ref, srk_ref, srv_ref, sstate_ref, sgrow_ref, *rest, tile, lam_init, npages):
    k_pages = rest[:npages]
    v_pages = rest[npages:2 * npages]
    (o_ref, os_ref, ro_ref, rst_ref, sro_ref, sns_ref,
     m_ref, l_ref, acc_ref, qq_ref, sa_ref, sb_ref, rstate_ref) = rest[2 * npages:]
    del pt_ref
    lam = _lambda(lq1_ref, lk1_ref, lq2_ref, lk2_ref, lam_init)
    _paged_step(sq_ref, skn_ref, svn_ref, bias_ref, nbias_ref, lam, k_pages, v_pages, os_ref)
    _ret_token(srq_ref, srk_ref, srv_ref, sstate_ref, sgrow_ref, sro_ref, sns_ref)
    first = (pl.program_id(1) == 0) & (pl.program_id(2) == 0)
    _ret_chunk(rq_ref, rk_ref, rv_ref, dmask_ref, qdec_ref, kdec_ref, grow_ref, ro_ref, rst_ref, rstate_ref, first)

    i = pl.program_id(2)
    q = q_ref[...]
    lane = lax.broadcasted_iota(I32, q.shape, 1)
    zero = jnp.zeros_like(q)
    qq = jnp.concatenate([jnp.where(lane < A_QK, q, zero), jnp.where(lane >= A_QK, q, zero)], axis=0)
    qq_ref[...] = qq
    m_ref[...] = jnp.full(m_ref.shape, NEG, F32)
    l_ref[...] = jnp.zeros(l_ref.shape, F32)
    acc_ref[...] = jnp.zeros(acc_ref.shape, F32)

    ones_rows = jnp.ones((ONES_ROWS, tile), BF16)

    def scores(j, s_ref):
        k = k_ref[pl.ds(pl.multiple_of(j * tile, tile), tile), :]
        s_ref[...] = lax.dot_general(k, qq_ref[...], (((1,), (1,)), ((), ())), preferred_element_type=F32)

    def accumulate(j, s_ref, masked):
        v = v_ref[pl.ds(pl.multiple_of(j * tile, tile), tile), :]
        vt = jnp.concatenate([v.T, ones_rows], axis=0)
        s = s_ref[...]
        if masked:
            key = lax.broadcasted_iota(I32, s.shape, 0)
            qry = lax.broadcasted_iota(I32, s.shape, 1)
            qry = jnp.where(qry >= tile, qry - tile, qry)
            s = jnp.where(key <= qry, s, NEG)
        m_prev = m_ref[...]
        m_new = jnp.maximum(m_prev, jnp.max(s, axis=0, keepdims=True))
        alpha = jnp.exp2(m_prev - m_new)
        p = jnp.exp2(s - m_new).astype(BF16)
        pv = jnp.dot(vt, p, preferred_element_type=F32)
        acc_ref[...] = alpha * acc_ref[...] + pv[:A_ROW, :]
        l_ref[...] = alpha * l_ref[...] + pv[A_ROW:A_ROW + 1, :]
        m_ref[...] = m_new

    scores(0, sa_ref)

    def pair(t, carry):
        j = 2 * t
        scores(j + 1, sb_ref)
        accumulate(j, sa_ref, False)
        scores(j + 2, sa_ref)
        accumulate(j + 1, sb_ref, False)
        return carry

    lax.fori_loop(0, i // 2, pair, 0)

    @pl.when(i % 2 == 0)
    def _():
        accumulate(i, sa_ref, True)

    @pl.when(i % 2 == 1)
    def _():
        scores(i, sb_ref)
        accumulate(i - 1, sa_ref, False)
        accumulate(i, sb_ref, True)

    o1 = acc_ref[:, :tile] / l_ref[:, :tile]
    o2 = acc_ref[:, tile:] / l_ref[:, tile:]
    o_ref[...] = (o1 - lam * o2).T


def _mixers(q_bf, k_bf, v_bf, lams, batch, seq, lam_init, page_table, sq, k_new, v_new, cache_k, cache_v,
            rq, rk, rv, srq, srk, srv, sstate):
    tile = ATTN_TILE
    nq = seq // tile
    nseq, npages = page_table.shape
    steps_per_seq = A_HEADS * nq
    assert nseq == batch * steps_per_seq, "one decode sequence per prompt query block"
    assert seq % (steps_per_seq * SUBLANES) == 0, "one retention chunk per grid step"
    chunk = seq // steps_per_seq
    dmask, q_dec, k_dec, g_rows = _ret_tables_chunk(chunk)
    _, _, _, g_token = _ret_tables_chunk(1)
    n_phys, page = cache_k.shape[0], cache_k.shape[1]
    prow = page * A_HEADS
    ck = cache_k.reshape(n_phys, prow, A_ROW)
    cv = cache_v.reshape(n_phys, prow, A_ROW)
    r = np.arange(2 * A_HEADS)[:, None] % A_HEADS
    bias = jnp.asarray(np.where(np.arange(prow)[None, :] % A_HEADS == r, 0.0, NEG).astype(np.float32))
    nbias = jnp.asarray(np.where(np.arange(LANES)[None, :] == r, 0.0, NEG).astype(np.float32))
    step = lambda b, h, i: (b * A_HEADS + h) * nq + i
    lam_spec = pl.BlockSpec((1, A_QK), lambda b, h, i, pt: (0, 0))
    tok = pl.BlockSpec((None, 1, A_WIDTH), lambda b, h, i, pt: (step(b, h, i), 0, 0))
    full = lambda a: pl.BlockSpec(a.shape, lambda b, h, i, pt: (0,) * a.ndim)

    def page_spec(jj):
        return pl.BlockSpec((None, prow, A_ROW), lambda b, h, i, pt: (pt[step(b, h, i) * npages + jj], 0, 0))

    crow = lambda w: pl.BlockSpec((chunk, w), lambda b, h, i, pt: (step(b, h, i), 0))
    stok = lambda w: pl.BlockSpec((None, 1, w), lambda b, h, i, pt: (step(b, h, i), 0, 0))
    sstate_spec = pl.BlockSpec((None, R_QW, R_V), lambda b, h, i, pt: (step(b, h, i), 0, 0))
    kernel = functools.partial(_attn_kernel, tile=tile, lam_init=lam_init, npages=npages)
    grid_spec = pltpu.PrefetchScalarGridSpec(
        num_scalar_prefetch=1,
        grid=(batch, A_HEADS, nq),
        in_specs=[pl.BlockSpec((tile, A_ROW), lambda b, h, i, pt: (b * nq + i, h)),
                  pl.BlockSpec((seq, A_ROW), lambda b, h, i, pt: (b, h)),
                  pl.BlockSpec((seq, A_ROW), lambda b, h, i, pt: (b, h))] + [lam_spec] * 4
                 + [tok, tok, tok, full(bias), full(nbias)]
                 + [crow(R_QW), crow(R_QW), crow(R_WIDTH), full(dmask), full(q_dec), full(k_dec), full(g_rows)]
                 + [stok(R_QW), stok(R_QW), stok(R_WIDTH), sstate_spec, full(g_token)]
                 + [page_spec(jj) for jj in range(npages)] * 2,
        out_specs=(pl.BlockSpec((tile, A_ROW), lambda b, h, i, pt: (b * nq + i, h)), tok,
                   crow(R_WIDTH), pl.BlockSpec((None, R_QW, R_V), lambda b, h, i, pt: (b, 0, 0)),
                   stok(R_WIDTH), sstate_spec),
        scratch_shapes=[pltpu.VMEM((1, 2 * tile), F32), pltpu.VMEM((1, 2 * tile), F32),
                        pltpu.VMEM((A_ROW, 2 * tile), F32), pltpu.VMEM((2 * tile, A_ROW), BF16),
                        pltpu.VMEM((tile, 2 * tile), F32), pltpu.VMEM((tile, 2 * tile), F32),
                        pltpu.VMEM((R_QW, R_V), F32)],
    )
    a_o, a_o_s, r_o, r_state, r_o_s, r_state_s = pl.pallas_call(
        kernel,
        grid_spec=grid_spec,
        out_shape=(jax.ShapeDtypeStruct((batch * seq, A_WIDTH), F32),
                   jax.ShapeDtypeStruct((nseq, 1, A_WIDTH), F32),
                   jax.ShapeDtypeStruct((batch * seq, R_WIDTH), F32),
                   jax.ShapeDtypeStruct((batch, R_QW, R_V), F32),
                   jax.ShapeDtypeStruct((nseq, 1, R_WIDTH), F32),
                   jax.ShapeDtypeStruct((nseq, R_QW, R_V), F32)),
        compiler_params=_cparams(("arbitrary", "arbitrary", "arbitrary")),
        name="mixers",
    )(page_table.reshape(-1), q_bf, k_bf, v_bf, *lams,
      sq.reshape(nseq, 1, A_WIDTH), k_new.reshape(nseq, 1, A_WIDTH), v_new.reshape(nseq, 1, A_WIDTH),
      bias, nbias, rq, rk, rv, dmask, q_dec, k_dec, g_rows,
      srq.reshape(nseq, 1, R_QW), srk.reshape(nseq, 1, R_QW), srv.reshape(nseq, 1, R_WIDTH), sstate, g_token,
      *([ck] * npages), *([cv] * npages))
    return (a_o, a_o_s.reshape(nseq, A_WIDTH), r_o, r_state,
            r_o_s.reshape(nseq, R_WIDTH), r_state_s)


def _ret_decay():
    return [math.log(1.0 - 2.0 ** (-5.0 - h)) for h in range(R_HEADS)]


def _ret_tables_chunk(chunk):
    log_g = jnp.log(1.0 - 2.0 ** (-5.0 - jnp.arange(R_HEADS, dtype=F32)))
    idx = jnp.arange(chunk, dtype=F32)
    diff = idx[:, None] - idx[None, :]
    dmask = jnp.where(diff >= 0, jnp.exp(jnp.maximum(diff, 0.0)[None] * log_g[:, None, None]), 0.0)
    q_dec = jnp.exp((idx + 1.0)[:, None] * log_g[None, :])
    k_dec = jnp.exp((chunk - 1.0 - idx)[:, None] * log_g[None, :])
    q_dec = jnp.repeat(q_dec, R_QK, axis=1)
    k_dec = jnp.repeat(k_dec, R_QK, axis=1)
    g_chunk = jnp.exp(chunk * log_g)
    g_rows = jnp.broadcast_to(jnp.repeat(g_chunk, R_QK)[:, None], (R_QW, R_V))
    return dmask, q_dec, k_dec, g_rows


def _ret_chunk(q_ref, k_ref, v_ref, dmask_ref, qdec_ref, kdec_ref, grow_ref, o_ref, st_ref, state_ref, first):
    q = q_ref[...]
    k = k_ref[...]
    qd = (q * qdec_ref[...]).astype(BF16)
    kd = (k * kdec_ref[...]).astype(BF16)
    qb = q.astype(BF16)
    kb = k.astype(BF16)
    vb = v_ref[...].astype(BF16)
    carried = jnp.where(first, 0.0, state_ref[...])
    for h in range(R_HEADS):
        ks = slice(h * R_QK, (h + 1) * R_QK)
        vs = slice(h * R_V, (h + 1) * R_V)
        state = carried[ks, :]
        inner = lax.dot_general(qb[:, ks], kb[:, ks], (((1,), (1,)), ((), ())),
                                preferred_element_type=F32) * dmask_ref[h]
        o = (jnp.dot(inner.astype(BF16), vb[:, vs], preferred_element_type=F32)
             + jnp.dot(qd[:, ks], state.astype(BF16), preferred_element_type=F32))
        o_ref[:, vs] = o
        upd = lax.dot_general(kd[:, ks], vb[:, vs], (((0,), (0,)), ((), ())),
                              preferred_element_type=F32)
        state_ref[ks, :] = grow_ref[ks, :] * state + upd
    st_ref[...] = state_ref[...]


def _ret_token(q_ref, k_ref, v_ref, state_ref, grow_ref, o_ref, ns_ref):
    kcol = jnp.broadcast_to(k_ref[...], (LANES, R_QW)).T
    qcol = jnp.broadcast_to(q_ref[...], (LANES, R_QW)).T
    vrows = jnp.concatenate(
        [jnp.broadcast_to(v_ref[:, h * R_V:(h + 1) * R_V], (R_QK, R_V)) for h in range(R_HEADS)], axis=0)
    new = grow_ref[...] * state_ref[...] + kcol * vrows
    ns_ref[...] = new
    qn = qcol * new
    for h in range(R_HEADS):
        o_ref[:, h * R_V:(h + 1) * R_V] = jnp.sum(qn[h * R_QK:(h + 1) * R_QK, :], axis=0, keepdims=True)


def _paged_step(q_ref, kn_ref, vn_ref, bias_ref, nbias_ref, lam, k_refs, v_refs, o_ref):
    npages = len(k_refs)
    q = q_ref[...]
    nrow = 2 * A_HEADS

    def head_rows(x, n):
        row = lax.broadcasted_iota(I32, (n, A_ROW), 0)
        out = jnp.zeros((n, A_ROW), F32)
        for h in range(A_HEADS):
            out = jnp.where(row == h, jnp.broadcast_to(x[:, h * A_ROW:(h + 1) * A_ROW], (n, A_ROW)), out)
        return out

    row8 = lax.broadcasted_iota(I32, (nrow, A_ROW), 0)
    lane8 = lax.broadcasted_iota(I32, (nrow, A_ROW), 1)
    q4 = head_rows(q, nrow)
    q8 = q4 + pltpu.roll(q4, A_HEADS, 0)
    qm = jnp.where((row8 < A_HEADS) == (lane8 < A_QK), q8, 0.0).astype(BF16)

    nt = (((1,), (1,)), ((), ()))
    bias = bias_ref[...]
    s = [lax.dot_general(qm, k_refs[j][...].astype(BF16), nt, preferred_element_type=F32) + bias
         for j in range(npages)]
    kn = head_rows(kn_ref[...], LANES).astype(BF16)
    s.append(lax.dot_general(qm, kn, nt, preferred_element_type=F32) + nbias_ref[...])
    m = functools.reduce(jnp.maximum, [jnp.max(x, axis=-1, keepdims=True) for x in s])
    p = [jnp.exp2(x - m) for x in s]
    l = functools.reduce(lambda a, b: a + b, [jnp.sum(x, axis=-1, keepdims=True) for x in p])
    inv = 1.0 / l
    vs = [v_refs[j][...].astype(BF16) for j in range(npages)] + [head_rows(vn_ref[...], LANES).astype(BF16)]
    out = jnp.zeros((nrow, A_ROW), F32)
    for pj, vj in zip(p, vs):
        pn = pj * inv
        first = lax.broadcasted_iota(I32, pn.shape, 0) < A_HEADS
        w8 = jnp.where(first, pn - lam * pltpu.roll(pn, A_HEADS, 0), 0.0).astype(BF16)
        out = out + jnp.dot(w8, vj, preferred_element_type=F32)
    for h in range(A_HEADS):
        o_ref[:, h * A_ROW:(h + 1) * A_ROW] = out[h:h + 1, :]


def _mix_tile(x_ref, a_ref, r_ref, g_ref, asub_ref, rnorm_ref, wo_ref, n2_ref,
              wrc_ref, wrh_ref, br_ref, h_ref, xn_ref, route_ref, *, lam_init):
    tm = x_ref.shape[0]
    parts = []
    for hd in range(A_HEADS):
        a = a_ref[:, hd * A_ROW:(hd + 1) * A_ROW]
        ms = jnp.mean(a * a, axis=-1, keepdims=True)
        parts.append((a * lax.rsqrt(ms + EPS) * asub_ref[...] * (1.0 - lam_init)).astype(BF16))
    for hd in range(R_HEADS):
        sl = slice(hd * R_V, (hd + 1) * R_V)
        r = r_ref[:, sl]
        ms = jnp.mean(r * r, axis=-1, keepdims=True)
        gate = g_ref[:, sl]
        gate = gate * (1.0 / (1.0 + jnp.exp(-gate)))
        parts.append((r * lax.rsqrt(ms + EPS) * rnorm_ref[...] * gate).astype(BF16))
    merged = jnp.concatenate(parts, axis=1)
    h = x_ref[...] + jnp.dot(merged, wo_ref[...], preferred_element_type=F32)
    h_ref[...] = h
    ms = jnp.mean(h * h, axis=-1, keepdims=True)
    xn = h * lax.rsqrt(ms + EPS) * n2_ref[...]
    for c in range(CHUNKS):
        xn_ref[pl.ds(c, tm, stride=CHUNKS), :] = xn[:, c * LANES:(c + 1) * LANES]
    xh = xn.astype(BF16)
    xl = (xn - xh.astype(F32)).astype(BF16)
    both = jnp.dot(xh, wrc_ref[...], preferred_element_type=F32)
    logits = (both[:, :LANES] + jnp.dot(xl, wrh_ref[...], preferred_element_type=F32)
              + both[:, LANES:]) + br_ref[...]
    lane = lax.broadcasted_iota(I32, logits.shape, 1).astype(F32)
    big = float(LANES)
    gl = jnp.where(lane < N_GROUPS, logits, NEG)
    gmax = jnp.max(gl, axis=-1, keepdims=True)
    gidx = jnp.min(jnp.where(gl == gmax, lane, big), axis=-1, keepdims=True)
    gsum = jnp.sum(jnp.where(lane < N_GROUPS, jnp.exp(gl - gmax), 0.0), axis=-1, keepdims=True)
    gprob = 1.0 / gsum
    lo = N_GROUPS + EXPERTS_PER_GROUP * gidx
    el = jnp.where((lane >= lo) & (lane < lo + EXPERTS_PER_GROUP), logits, NEG)
    v1 = jnp.max(el, axis=-1, keepdims=True)
    i1 = jnp.min(jnp.where(el == v1, lane, big), axis=-1, keepdims=True)
    el2 = jnp.where(lane == i1, NEG, el)
    v2 = jnp.max(el2, axis=-1, keepdims=True)
    i2 = jnp.min(jnp.where(el2 == v2, lane, big), axis=-1, keepdims=True)
    e = jnp.exp(v2 - v1)
    w1 = gprob / (1.0 + e)
    w2 = gprob * e / (1.0 + e)
    e1 = i1 - N_GROUPS
    e2 = i2 - N_GROUPS
    route_ref[...] = jnp.where(lane == 0, e1, jnp.where(lane == 1, e2, jnp.where(
        lane == 2, w1, jnp.where(lane == 3, w2, 0.0))))


def _mix_kernel(*refs, lam_init, n_tiles, has_tail):
    if not has_tail:
        _mix_tile(*refs, lam_init=lam_init)
        return
    ins, (th_ref, txn_ref, troute_ref), outs = refs[:11], refs[11:14], refs[14:]
    h_ref, xn_ref, route_ref = outs
    i = pl.program_id(0)

    @pl.when(i < n_tiles)
    def _():
        _mix_tile(*ins, *outs, lam_init=lam_init)

    @pl.when(i == n_tiles)
    def _():
        tm = h_ref.shape[0]
        nt = th_ref.shape[0]
        h_ref[:nt, :] = th_ref[...]
        h_ref[nt:, :] = jnp.zeros((tm - nt, D_MODEL), F32)
        xn_ref[:nt * CHUNKS, :] = txn_ref[...]
        xn_ref[nt * CHUNKS:, :] = jnp.zeros(((tm - nt) * CHUNKS, LANES), F32)
        route_ref[:nt, :] = troute_ref[...]
        route_ref[nt:, :] = jnp.zeros((tm - nt, LANES), F32)


def _mix_out(x2d, a_o, r_o, rg, asub, rnorm, wo_bf, n2, wr_cat, wr_hi, br, lam_init, tm, tail=None):
    t = x2d.shape[0]
    n_tiles = t // tm
    has_tail = tail is not None
    n_out = n_tiles + (1 if has_tail else 0)
    row = lambda w: pl.BlockSpec((tm, w), lambda i: (jnp.minimum(i, n_tiles - 1), 0))
    full = lambda a: pl.BlockSpec(a.shape, lambda i: (0,) * a.ndim)
    out_shape = (jax.ShapeDtypeStruct((n_out * tm, D_MODEL), F32),
                 jax.ShapeDtypeStruct((n_out * tm * CHUNKS, LANES), F32),
                 jax.ShapeDtypeStruct((n_out * tm, LANES), F32))
    out_specs = (pl.BlockSpec((tm, D_MODEL), lambda i: (i, 0)),
                 pl.BlockSpec((tm * CHUNKS, LANES), lambda i: (i, 0)),
                 pl.BlockSpec((tm, LANES), lambda i: (i, 0)))
    in_specs = [row(D_MODEL), row(A_WIDTH), row(R_WIDTH), row(R_WIDTH), full(asub), full(rnorm),
                full(wo_bf), full(n2), full(wr_cat), full(wr_hi), full(br)]
    args = [x2d, a_o, r_o, rg, asub, rnorm, wo_bf, n2, wr_cat, wr_hi, br]
    if has_tail:
        assert tail[0].shape[0] <= tm
        in_specs += [full(a) for a in tail]
        args += list(tail)
    kernel = functools.partial(_mix_kernel, lam_init=lam_init, n_tiles=n_tiles, has_tail=has_tail)
    return pl.pallas_call(
        kernel,
        grid=(n_out,),
        in_specs=in_specs,
        out_specs=out_specs,
        out_shape=out_shape,
        compiler_params=_cparams(("arbitrary",)),
        name="mix_out",
    )(*args)


def _plan_kernel(route_ref, ltri_ref, utri_ref, dest_ref, te_ref, meta_ref, *, n, slot_tile):
    tile = PLAN_TILE
    lane = lax.broadcasted_iota(I32, (tile, LANES), 1)

    def block(b):
        r = route_ref[pl.ds(pl.multiple_of(b * tile, tile), tile), :]
        e1 = r[:, 0:1].astype(I32)
        e2 = r[:, 1:2].astype(I32)
        return jnp.where((lane == e1) | (lane == e2), 1.0, 0.0), e1, e2

    def count(b, c):
        return c + jnp.sum(block(b)[0], axis=0, keepdims=True)

    cnt = lax.fori_loop(0, n // tile, count, jnp.zeros((1, LANES), F32), unroll=PLAN_UNROLL)
    ntile = jnp.floor((cnt + (slot_tile - 1)) * (1.0 / slot_tile))
    nt8 = jnp.broadcast_to(ntile, (SUBLANES, LANES)).astype(BF16)
    base_t = jnp.dot(nt8, utri_ref[...], preferred_element_type=F32)[0:1, :]
    base = base_t * slot_tile
    ends = base_t + ntile
    tl = lax.broadcasted_iota(I32, te_ref.shape, 0).astype(F32)
    el = lax.broadcasted_iota(I32, te_ref.shape, 1)
    hit = jnp.where((el < N_EXPERTS) & (ends <= tl), 1.0, 0.0)
    te_ref[...] = jnp.broadcast_to(jnp.sum(hit, axis=-1, keepdims=True), te_ref.shape).astype(I32)
    mrow = lax.broadcasted_iota(I32, meta_ref.shape, 0)
    meta_ref[...] = jnp.where(mrow == 0, cnt, jnp.where(mrow == 1, base, jnp.where(
        mrow == 2, ntile * slot_tile, 0.0))).astype(I32)

    def place(b, run):
        onehot, e1, e2 = block(b)
        rank = jnp.dot(ltri_ref[...], onehot.astype(BF16), preferred_element_type=F32)
        pos = base + run + rank
        d1 = jnp.sum(jnp.where(lane == e1, pos, 0.0), axis=-1, keepdims=True)
        d2 = jnp.sum(jnp.where(lane == e2, pos, 0.0), axis=-1, keepdims=True)
        dest_ref[pl.ds(pl.multiple_of(b * tile, tile), tile), :] = jnp.where(
            lane == 0, d1, jnp.where(lane == 1, d2, 0.0)).astype(I32)
        return run + jnp.sum(onehot, axis=0, keepdims=True)

    lax.fori_loop(0, n // tile, place, jnp.zeros((1, LANES), F32), unroll=PLAN_UNROLL)


def _plan(route, n, n_slot_tiles):
    tile = PLAN_TILE
    te_rows = -(-n_slot_tiles // SUBLANES) * SUBLANES
    ii = np.arange(tile)
    ltri = jnp.asarray((ii[None, :] < ii[:, None]).astype(np.float32), dtype=BF16)
    ee = np.arange(LANES)
    utri = jnp.asarray((ee[:, None] < ee[None, :]).astype(np.float32), dtype=BF16)
    kernel = functools.partial(_plan_kernel, n=n, slot_tile=SLOT_TILE)
    return pl.pallas_call(
        kernel,
        out_shape=(jax.ShapeDtypeStruct((n, LANES), I32),
                   jax.ShapeDtypeStruct((te_rows, LANES), I32),
                   jax.ShapeDtypeStruct((SUBLANES, LANES), I32)),
        compiler_params=pltpu.CompilerParams(vmem_limit_bytes=VMEM_LIMIT),
        name="plan",
    )(route, ltri, utri)


def _total_tiles(meta_ref):
    last = N_EXPERTS - 1
    return (meta_ref[LANES + last] + meta_ref[2 * LANES + last]) // SLOT_TILE


def _dispatch_kernel(dest_ref, meta_ref, xn_ref, xs_hbm, zeros_ref, zsem, ssem, *, tm, n_steps, n_total, n_tiles):
    i = pl.program_id(0)
    ts = SLOT_TILE

    def zero_tiles(start):
        def one(tile):
            cp = pltpu.make_async_copy(zeros_ref, xs_hbm.at[pl.ds(tile * ts, ts)], zsem)
            if start:
                cp.start()
            else:
                cp.wait()

        def per_expert(e, c):
            first = meta_ref[LANES + e]
            reserved = meta_ref[2 * LANES + e]

            @pl.when(meta_ref[e] < reserved)
            def _():
                one((first + reserved) // ts - 1)

            return c

        lax.fori_loop(0, N_EXPERTS, per_expert, 0)

        def trailing(t, c):
            one(t)
            return c

        lax.fori_loop(_total_tiles(meta_ref), n_tiles, trailing, 0)

    @pl.when(i == 0)
    def _():
        zeros_ref[...] = jnp.zeros(zeros_ref.shape, F32)
        zero_tiles(True)
        zero_tiles(False)

    def scatter(n_tok):
        base = 2 * i * tm

        def start(r, c):
            src = xn_ref.at[pl.ds(pl.multiple_of(r * CHUNKS, CHUNKS), CHUNKS), :]
            for k in range(2):
                pltpu.make_async_copy(src, xs_hbm.at[dest_ref[base + 2 * r + k]], ssem).start(
                    priority=k % DMA_PRIORITIES)
            return c

        lax.fori_loop(0, n_tok, start, 0, unroll=8)

        def wait(r, c):
            pltpu.make_async_copy(xn_ref.at[pl.ds(0, CHUNKS), :], xs_hbm.at[0], ssem).wait()
            return c

        lax.fori_loop(0, 2 * n_tok, wait, 0, unroll=8)

    rem = n_total - (n_steps - 1) * tm
    if rem == tm:
        scatter(tm)
    else:
        pl.when(i < n_steps - 1)(functools.partial(scatter, tm))
        pl.when(i == n_steps - 1)(functools.partial(scatter, rem))


def _dispatch(dest_flat, meta_flat, xn_pool, n_total, n_tiles):
    tm = TOKEN_TILE
    n_steps = -(-n_total // tm)
    assert xn_pool.shape[0] >= n_steps * tm * CHUNKS
    kernel = functools.partial(_dispatch_kernel, tm=tm, n_steps=n_steps, n_total=n_total, n_tiles=n_tiles)
    grid_spec = pltpu.PrefetchScalarGridSpec(
        num_scalar_prefetch=2,
        grid=(n_steps,),
        in_specs=[pl.BlockSpec((tm * CHUNKS, LANES), lambda i, d, m: (i, 0))],
        out_specs=pl.BlockSpec(memory_space=pl.ANY),
        scratch_shapes=[pltpu.VMEM((SLOT_TILE, CHUNKS, LANES), F32),
                        pltpu.SemaphoreType.DMA, pltpu.SemaphoreType.DMA],
    )
    return pl.pallas_call(
        kernel,
        grid_spec=grid_spec,
        out_shape=jax.ShapeDtypeStruct((n_tiles * SLOT_TILE, CHUNKS, LANES), F32),
        compiler_params=_cparams(("arbitrary",)),
        name="dispatch",
    )(dest_flat, meta_flat, xn_pool)


def _row_gather(src_hbm, idx_of_row, buf, sem, n_rows):
    def body(j, c):
        for k in range(DMA_PRIORITIES):
            r = DMA_PRIORITIES * j + k
            pltpu.make_async_copy(src_hbm.at[idx_of_row(r)],
                                  buf.at[pl.ds(pl.multiple_of(r * CHUNKS, CHUNKS), CHUNKS), :],
                                  sem).start(priority=k)
        return c

    assert n_rows % DMA_PRIORITIES == 0
    lax.fori_loop(0, n_rows // DMA_PRIORITIES, body, 0, unroll=8)


def _row_wait(src_hbm, buf, sem, n_rows):
    def body(r, c):
        pltpu.make_async_copy(src_hbm.at[0], buf.at[pl.ds(0, CHUNKS), :], sem).wait()
        return c

    lax.fori_loop(0, n_rows, body, 0, unroll=8)


def _gathered_rows(buf, n_rows):
    return jnp.concatenate([buf[pl.ds(c, n_rows, stride=CHUNKS), :] for c in range(CHUNKS)], axis=1)


def _expert_kernel(te_ref, meta_ref, xs_ref, wg_ref, wu_ref, wd_ref, ys_ref, wg_bf, wu_bf, wd_bf):
    i = pl.program_id(0)
    ts = SLOT_TILE
    e = te_ref[i]
    active = e < N_EXPERTS
    prev = te_ref[jnp.maximum(i - 1, 0)]

    @pl.when(active & ((i == 0) | (e != prev)))
    def _():
        wg_bf[...] = wg_ref[...].astype(BF16)
        wu_bf[...] = wu_ref[...].astype(BF16)
        wd_bf[...] = wd_ref[...].astype(BF16)

    @pl.when(active)
    def _():
        x = _gathered_rows(xs_ref, ts).astype(BF16)
        a = jnp.dot(x, wg_bf[...], preferred_element_type=F32)
        u = jnp.dot(x, wu_bf[...], preferred_element_type=F32)
        hmid = (a * (1.0 / (1.0 + jnp.exp(-a))) * u).astype(BF16)
        y = jnp.dot(hmid, wd_bf[...], preferred_element_type=F32)
        for c in range(CHUNKS):
            ys_ref[pl.ds(c, ts, stride=CHUNKS), :] = y[:, c * LANES:(c + 1) * LANES]

    @pl.when(jnp.logical_not(active))
    def _():
        ys_ref[...] = jnp.zeros(ys_ref.shape, F32)


def _experts(te, meta_flat, xs, w_gate, w_up, w_down, n_tiles):
    ts = SLOT_TILE
    wsel = lambda i, te, meta: (jnp.minimum(te[i], N_EXPERTS - 1), 0, 0)
    xsel = lambda i, te, meta: (jnp.minimum(i, _total_tiles(meta) - 1), 0)
    grid_spec = pltpu.PrefetchScalarGridSpec(
        num_scalar_prefetch=2,
        grid=(n_tiles,),
        in_specs=[pl.BlockSpec((ts * CHUNKS, LANES), xsel),
                  pl.BlockSpec((None, D_MODEL, D_EXPERT), wsel),
                  pl.BlockSpec((None, D_MODEL, D_EXPERT), wsel),
                  pl.BlockSpec((None, D_EXPERT, D_MODEL), wsel)],
        out_specs=pl.BlockSpec((ts * CHUNKS, LANES), lambda i, te, meta: (i, 0)),
        scratch_shapes=[pltpu.VMEM((D_MODEL, D_EXPERT), BF16),
                        pltpu.VMEM((D_MODEL, D_EXPERT), BF16),
                        pltpu.VMEM((D_EXPERT, D_MODEL), BF16)],
    )
    return pl.pallas_call(
        _expert_kernel,
        grid_spec=grid_spec,
        out_shape=jax.ShapeDtypeStruct((n_tiles * ts * CHUNKS, LANES), F32),
        compiler_params=_cparams(("arbitrary",)),
        name="experts",
    )(te, meta_flat, xs.reshape(n_tiles * ts * CHUNKS, LANES), w_gate, w_up, w_down)


def _combine_kernel(dest_ref, ys_hbm, h_ref, route_ref, y_ref, gbuf, sem, *, tm, n_steps, tok_off):
    i = pl.program_id(0)
    slot = i % 2
    nrow = 2 * tm

    def start(step, sl):
        base = (tok_off + step * tm) * 2
        _row_gather(ys_hbm, lambda r: dest_ref[base + r], gbuf.at[sl], sem.at[sl], nrow)

    @pl.when(i == 0)
    def _():
        start(0, 0)

    @pl.when(i + 1 < n_steps)
    def _():
        start(i + 1, 1 - slot)

    _row_wait(ys_hbm, gbuf.at[slot], sem.at[slot], nrow)
    buf = gbuf.at[slot]
    route = route_ref[...]
    w1 = route[:, 2:3]
    w2 = route[:, 3:4]
    for c in range(CHUNKS):
        g1 = buf[pl.ds(c, tm, stride=2 * CHUNKS), :]
        g2 = buf[pl.ds(CHUNKS + c, tm, stride=2 * CHUNKS), :]
        sl = slice(c * LANES, (c + 1) * LANES)
        y_ref[:, sl] = h_ref[:, sl] + w1 * g1 + w2 * g2


def _combine(dest_flat, ys3d, h_pool, route, tm, tok_off, n_tok):
    n_steps = n_tok // tm
    boff = tok_off // tm
    kernel = functools.partial(_combine_kernel, tm=tm, n_steps=n_steps, tok_off=tok_off)
    grid_spec = pltpu.PrefetchScalarGridSpec(
        num_scalar_prefetch=1,
        grid=(n_steps,),
        in_specs=[pl.BlockSpec(memory_space=pl.ANY),
                  pl.BlockSpec((tm, D_MODEL), lambda i, d: (i + boff, 0)),
                  pl.BlockSpec((tm, LANES), lambda i, d: (i + boff, 0))],
        out_specs=pl.BlockSpec((tm, D_MODEL), lambda i, d: (i, 0)),
        scratch_shapes=[pltpu.VMEM((2, 2 * tm * CHUNKS, LANES), F32), pltpu.SemaphoreType.DMA((2,))],
    )
    return pl.pallas_call(
        kernel,
        grid_spec=grid_spec,
        out_shape=jax.ShapeDtypeStruct((n_tok, D_MODEL), F32),
        compiler_params=_cparams(("arbitrary",)),
        name="combine",
    )(dest_flat, ys3d, h_pool, route)


def kernel(x_prompt, x_sample, cache_k, cache_v, state_ret, page_table, norm1, w_in, a_q_norm, a_k_norm,
           a_lambda_q1, a_lambda_k1, a_lambda_q2, a_lambda_k2, a_subln, r_norm, w_o, norm2,
           w_group_router, b_group_router, w_expert_router, b_expert_router, w_gate, w_up, w_down):
    depth = norm1.shape[0]
    assert depth == 1, "single-layer step"
    batch, seq, d = x_prompt.shape
    nsamp, tdec, _ = x_sample.shape
    assert d == D_MODEL and tdec == 1
    assert seq % TOKEN_TILE == 0 and seq % ATTN_TILE == 0
    assert nsamp % PLAN_TILE == 0
    past = page_table.shape[1] * cache_k.shape[2]
    n_prompt = batch * seq
    assert n_prompt % COMBINE_TILE == 0
    n_total = n_prompt + nsamp
    lam_init = 0.8 - 0.6 * math.exp(-0.3 * 0)
    l = 0

    lams = tuple(a[l].reshape(1, A_QK) for a in (a_lambda_q1, a_lambda_k1, a_lambda_q2, a_lambda_k2))
    w_in_bf = w_in[l].astype(BF16)
    w_o_bf = w_o[l].astype(BF16)
    g1 = norm1[l].reshape(1, D_MODEL)
    n2 = norm2[l].reshape(1, D_MODEL)
    qg = jnp.tile(a_q_norm[l], LANES // A_QK).reshape(1, LANES)
    kg = jnp.tile(a_k_norm[l], LANES // A_QK).reshape(1, LANES)
    asub = a_subln[l].reshape(1, A_ROW)
    rnorm = r_norm[l].reshape(1, R_V)
    ones = _segment_ones()
    w_r = jnp.concatenate([w_group_router[l], w_expert_router[l]], axis=1)
    w_r = jnp.pad(w_r, ((0, 0), (0, LANES - w_r.shape[1])))
    wr_hi = w_r.astype(BF16)
    wr_lo = (w_r - wr_hi.astype(F32)).astype(BF16)
    wr_cat = jnp.concatenate([wr_hi, wr_lo], axis=1)
    b_r = jnp.concatenate([b_group_router[l], b_expert_router[l]])
    b_r = jnp.pad(b_r, (0, LANES - b_r.shape[0])).reshape(1, LANES)

    pos_s = jnp.full((nsamp,), past, dtype=jnp.int32)
    xs = x_sample.reshape(nsamp, D_MODEL)
    (sq_bf, k_s, _, v_s, _, srq, srk, srv, srg) = _proj(
        xs, g1, w_in_bf, qg, kg, _rope_tables(pos_s), _ret_tables(pos_s), ones, PLAN_TILE)
    pos_p = jnp.arange(seq)
    xp = x_prompt.reshape(n_prompt, D_MODEL)
    (q_bf, k_p, k_bf, v_p, v_bf, rq, rk, rv, rg) = _proj(
        xp, g1, w_in_bf, qg, kg, _rope_tables(pos_p), _ret_tables(pos_p), ones, TOKEN_TILE)
    a_o, a_o_s, r_o, r_state_p, r_o_s, r_state_s = _mixers(
        q_bf, k_bf, v_bf, lams, batch, seq, lam_init, page_table, sq_bf.astype(F32), k_s, v_s,
        cache_k[l], cache_v[l], rq, rk, rv, srq, srk, srv, state_ret[l].reshape(nsamp, R_QW, R_V))

    sample_rows = _mix_out(xs, a_o_s, r_o_s, srg, asub, rnorm, w_o_bf, n2, wr_cat, wr_hi, b_r,
                           lam_init, PLAN_TILE)
    h_pool, xn_pool, route = _mix_out(xp, a_o, r_o, rg, asub, rnorm, w_o_bf, n2, wr_cat, wr_hi, b_r,
                                      lam_init, TOKEN_TILE, tail=sample_rows)
    n_pool = h_pool.shape[0]

    n_assign = 2 * n_total
    n_tiles = -(-(n_assign + N_EXPERTS * (SLOT_TILE - 1)) // SLOT_TILE)
    n_slots = n_tiles * SLOT_TILE
    dest128, te128, meta = _plan(route, n_total, n_tiles)
    dest_flat = dest128[:, :2].reshape(-1)
    te = te128[:n_tiles, 0]
    meta_flat = meta.reshape(-1)
    xs = _dispatch(dest_flat, meta_flat, xn_pool, n_total, n_tiles)
    ys = _experts(te, meta_flat, xs,
                  w_gate[l].reshape(N_EXPERTS, D_MODEL, D_EXPERT),
                  w_up[l].reshape(N_EXPERTS, D_MODEL, D_EXPERT),
                  w_down[l].reshape(N_EXPERTS, D_EXPERT, D_MODEL), n_tiles)
    ys3d = ys.reshape(n_slots, CHUNKS, LANES)
    y_p = _combine(dest_flat, ys3d, h_pool, route, COMBINE_TILE, 0, n_prompt)
    y_s = _combine(dest_flat, ys3d, h_pool, route, PLAN_TILE, n_prompt, nsamp)

    return (y_p.reshape(batch, seq, D_MODEL),
            y_s.reshape(nsamp, 1, D_MODEL),
            k_p.reshape(1, batch, seq, A_HEADS, A_ROW),
            v_p.reshape(1, batch, seq, A_HEADS, A_ROW),
            r_state_p.reshape(1, batch, R_HEADS, R_QK, R_V),
            k_s.reshape(1, nsamp, 1, A_HEADS, A_ROW),
            v_s.reshape(1, nsamp, 1, A_HEADS, A_ROW),
            r_state_s.reshape(1, nsamp, R_HEADS, R_QK, R_V))
```

```python
import functools
import math

import numpy as np
import jax
import jax.numpy as jnp
from jax import lax
from jax.experimental import pallas as pl
from jax.experimental.pallas import tpu as pltpu

F32 = jnp.float32
BF16 = jnp.bfloat16
I32 = jnp.int32

LANES = 128
SUBLANES = 8
CHUNKS = 8

D_MODEL = 1024
A_HEADS = 4
A_QK = 64
A_ROW = 2 * A_QK
A_WIDTH = A_HEADS * A_ROW
ROPE_THETA = 500000.0
ROPE_DIM = A_QK // 4
R_HEADS = 4
R_QK = 64
R_V = 128
R_QW = R_HEADS * R_QK
R_WIDTH = R_HEADS * R_V
R_THETA = 10000.0
N_GROUPS = 4
EXPERTS_PER_GROUP = 8
N_EXPERTS = N_GROUPS * EXPERTS_PER_GROUP
D_EXPERT = 512
EPS = 1e-6
NEG = -1e30

TOKEN_TILE = 512
MIX_TILE = 1024
COMBINE_TILE = 512
DMA_PRIORITIES = 2
ATTN_TILE = 512
ONES_ROWS = 16
LOG2E = 1.4426950408889634
PLAN_TILE = 128
PLAN_UNROLL = 3
SLOT_TILE = 256
VMEM_LIMIT = 56 * 1024 * 1024


def _cparams(sem, vmem=VMEM_LIMIT):
    return pltpu.CompilerParams(dimension_semantics=sem, vmem_limit_bytes=vmem)


def _rope_tables(pos):
    half = ROPE_DIM // 2
    inv = ROPE_THETA ** (-jnp.arange(half, dtype=F32) / half)
    ang = pos.astype(F32)[:, None] * inv[None, :]
    cos, sin = jnp.cos(ang), jnp.sin(ang)
    n = pos.shape[0]
    ones = jnp.ones((n, A_QK - ROPE_DIM), F32)
    zeros = jnp.zeros((n, A_QK - ROPE_DIM), F32)
    zh = jnp.zeros((n, half), F32)
    c = jnp.concatenate([cos, cos, ones], axis=1)
    s_next = jnp.concatenate([-sin, zh, zeros], axis=1)
    s_prev = jnp.concatenate([zh, sin, zeros], axis=1)
    rep = LANES // A_QK
    return tuple(jnp.tile(t, (1, rep)) for t in (c, s_next, s_prev))


def _ret_tables(pos):
    half = R_QK // 2
    inv = 1.0 / (R_THETA ** jnp.linspace(0.0, 1.0, half, dtype=F32))
    ang = pos.astype(F32)[:, None] * inv[None, :]
    cos, sin = jnp.cos(ang), jnp.sin(ang)
    z = jnp.zeros_like(sin)
    c = jnp.stack([cos, cos], axis=-1).reshape(-1, R_QK)
    s_next = jnp.stack([-sin, z], axis=-1).reshape(-1, R_QK)
    s_prev = jnp.stack([z, sin], axis=-1).reshape(-1, R_QK)
    rep = LANES // R_QK
    return tuple(jnp.tile(t, (1, rep)) for t in (c, s_next, s_prev))


def _segment_ones():
    seg = np.arange(LANES) // A_QK
    return jnp.asarray((seg[:, None] == seg[None, :]).astype(np.float32), dtype=BF16)


def _proj_kernel(x_ref, g1_ref, w_ref, qg_ref, kg_ref, rc_ref, rn_ref, rp_ref,
                 tc_ref, tn_ref, tp_ref, ones_ref,
                 qbf_ref, k_ref, kbf_ref, v_ref, vbf_ref, rq_ref, rk_ref, rv_ref, rg_ref):
    tm = x_ref.shape[0]
    x = x_ref[...]
    ms = jnp.mean(x * x, axis=-1, keepdims=True)
    n = (x * lax.rsqrt(ms + EPS) * g1_ref[...]).astype(BF16)
    h = jnp.dot(n, w_ref[...], preferred_element_type=F32)
    ones = ones_ref[...]
    rc, rn, rp = rc_ref[...], rn_ref[...], rp_ref[...]

    def head_norm_rope(xh, gain):
        sq = xh * xh
        hi = sq.astype(BF16)
        lo = (sq - hi.astype(F32)).astype(BF16)
        ssq = (jnp.dot(hi, ones, preferred_element_type=F32)
               + jnp.dot(lo, ones, preferred_element_type=F32))
        y = xh * lax.rsqrt(ssq * (1.0 / A_QK) + EPS) * gain
        half = ROPE_DIM // 2
        return (y * rc + pltpu.roll(y, LANES - half, 1) * rn + pltpu.roll(y, half, 1) * rp)

    for hd in range(A_HEADS):
        sl = slice(hd * A_ROW, (hd + 1) * A_ROW)
        q = head_norm_rope(h[:, sl], qg_ref[...])
        qbf_ref[:, sl] = (q * (A_QK ** -0.5 * LOG2E)).astype(BF16)
        k = head_norm_rope(h[:, A_WIDTH + hd * A_ROW:A_WIDTH + (hd + 1) * A_ROW], kg_ref[...])
        k_ref[pl.ds(hd, tm, stride=A_HEADS), :] = k
        kbf_ref[:, sl] = k.astype(BF16)
    o = 2 * A_WIDTH
    v = h[:, o:o + A_WIDTH]
    for hd in range(A_HEADS):
        v_ref[pl.ds(hd, tm, stride=A_HEADS), :] = v[:, hd * A_ROW:(hd + 1) * A_ROW]
    vbf_ref[...] = v.astype(BF16)
    o += A_WIDTH
    tc, tn, tp = tc_ref[...], tn_ref[...], tp_ref[...]

    def pair_rotate(xs):
        return xs * tc + pltpu.roll(xs, LANES - 1, 1) * tn + pltpu.roll(xs, 1, 1) * tp

    for j in range(R_QW // LANES):
        sl = slice(j * LANES, (j + 1) * LANES)
        rq_ref[:, sl] = pair_rotate(h[:, o + j * LANES:o + (j + 1) * LANES])
        rk_ref[:, sl] = pair_rotate(h[:, o + R_QW + j * LANES:o + R_QW + (j + 1) * LANES]) * (R_QK ** -0.5)
    o += 2 * R_QW
    rv_ref[...] = h[:, o:o + R_WIDTH]
    rg_ref[...] = h[:, o + R_WIDTH:o + 2 * R_WIDTH]


def _proj(x2d, g1, w_bf, qg, kg, rope_t, ret_t, ones, tm):
    t = x2d.shape[0]
    table_tiles = rope_t[0].shape[0] // tm
    row = lambda w: pl.BlockSpec((tm, w), lambda i: (i, 0))
    table = pl.BlockSpec((tm, LANES), lambda i: (i % table_tiles, 0))
    full = lambda a: pl.BlockSpec(a.shape, lambda i: (0,) * a.ndim)
    out_shape = (
        jax.ShapeDtypeStruct((t, A_WIDTH), BF16),
        jax.ShapeDtypeStruct((t * A_HEADS, A_ROW), F32),
        jax.ShapeDtypeStruct((t, A_WIDTH), BF16),
        jax.ShapeDtypeStruct((t * A_HEADS, A_ROW), F32),
        jax.ShapeDtypeStruct((t, A_WIDTH), BF16),
        jax.ShapeDtypeStruct((t, R_QW), F32),
        jax.ShapeDtypeStruct((t, R_QW), F32),
        jax.ShapeDtypeStruct((t, R_WIDTH), F32),
        jax.ShapeDtypeStruct((t, R_WIDTH), F32),
    )
    return pl.pallas_call(
        _proj_kernel,
        grid=(t // tm,),
        in_specs=[row(D_MODEL), full(g1), full(w_bf), full(qg), full(kg)]
                 + [table] * 6 + [full(ones)],
        out_specs=tuple(pl.BlockSpec((tm * s.shape[0] // t, s.shape[1]), lambda i: (i, 0)) for s in out_shape),
        out_shape=out_shape,
        compiler_params=_cparams(("parallel",)),
        name="proj",
    )(x2d, g1, w_bf, qg, kg, *rope_t, *ret_t, ones)


def _lambda(lq1_ref, lk1_ref, lq2_ref, lk2_ref, lam_init):
    s1 = jnp.sum(lq1_ref[...] * lk1_ref[...], axis=-1, keepdims=True)
    s2 = jnp.sum(lq2_ref[...] * lk2_ref[...], axis=-1, keepdims=True)
    return jnp.exp(s1) - jnp.exp(s2) + lam_init


def _attn_kernel(pt_ref, q_ref, k_ref, v_ref, lq1_ref, lk1_ref, lq2_ref, lk2_ref,
                 sq_ref, skn_ref, svn_ref, bias_ref, nbias_ref,
                 rq_ref, rk_ref, rv_ref, dmask_ref, qdec_ref, kdec_ref, grow_ref,
                 srq_ref, srk_ref, srv_ref, sstate_ref, sgrow_ref, *rest, tile, lam_init, npages):
    k_pages = rest[:npages]
    v_pages = rest[npages:2 * npages]
    (o_ref, os_ref, ro_ref, rst_ref, sro_ref, sns_ref,
     m_ref, l_ref, acc_ref, qq_ref, sa_ref, sb_ref, rstate_ref) = rest[2 * npages:]
    del pt_ref
    lam = _lambda(lq1_ref, lk1_ref, lq2_ref, lk2_ref, lam_init)
    _paged_step(sq_ref, skn_ref, svn_ref, bias_ref, nbias_ref, lam, k_pages, v_pages, os_ref)
    _ret_token(srq_ref, srk_ref, srv_ref, sstate_ref, sgrow_ref, sro_ref, sns_ref)
    first = (pl.program_id(1) == 0) & (pl.program_id(2) == 0)
    _ret_chunk(rq_ref, rk_ref, rv_ref, dmask_ref, qdec_ref, kdec_ref, grow_ref, ro_ref, rst_ref, rstate_ref, first)

    i = pl.program_id(2)
    q = q_ref[...]
    lane = lax.broadcasted_iota(I32, q.shape, 1)
    zero = jnp.zeros_like(q)
    qq = jnp.concatenate([jnp.where(lane < A_QK, q, zero), jnp.where(lane >= A_QK, q, zero)], axis=0)
    qq_ref[...] = qq
    m_ref[...] = jnp.full(m_ref.shape, NEG, F32)
    l_ref[...] = jnp.zeros(l_ref.shape, F32)
    acc_ref[...] = jnp.zeros(acc_ref.shape, F32)

    ones_rows = jnp.ones((ONES_ROWS, tile), BF16)

    def scores(j, s_ref):
        k = k_ref[pl.ds(pl.multiple_of(j * tile, tile), tile), :]
        s_ref[...] = lax.dot_general(k, qq_ref[...], (((1,), (1,)), ((), ())), preferred_element_type=F32)

    def accumulate(j, s_ref, masked):
        v = v_ref[pl.ds(pl.multiple_of(j * tile, tile), tile), :]
        vt = jnp.concatenate([v.T, ones_rows], axis=0)
        s = s_ref[...]
        if masked:
            key = lax.broadcasted_iota(I32, s.shape, 0)
            qry = lax.broadcasted_iota(I32, s.shape, 1)
            qry = jnp.where(qry >= tile, qry - tile, qry)
            s = jnp.where(key <= qry, s, NEG)
        m_prev = m_ref[...]
        m_new = jnp.maximum(m_prev, jnp.max(s, axis=0, keepdims=True))
        alpha = jnp.exp2(m_prev - m_new)
        p = jnp.exp2(s - m_new).astype(BF16)
        pv = jnp.dot(vt, p, preferred_element_type=F32)
        acc_ref[...] = alpha * acc_ref[...] + pv[:A_ROW, :]
        l_ref[...] = alpha * l_ref[...] + pv[A_ROW:A_ROW + 1, :]
        m_ref[...] = m_new

    scores(0, sa_ref)

    def pair(t, carry):
        j = 2 * t
        scores(j + 1, sb_ref)
        accumulate(j, sa_ref, False)
        scores(j + 2, sa_ref)
        accumulate(j + 1, sb_ref, False)
        return carry

    lax.fori_loop(0, i // 2, pair, 0)

    @pl.when(i % 2 == 0)
    def _():
        accumulate(i, sa_ref, True)

    @pl.when(i % 2 == 1)
    def _():
        scores(i, sb_ref)
        accumulate(i - 1, sa_ref, False)
        accumulate(i, sb_ref, True)

    o1 = acc_ref[:, :tile] / l_ref[:, :tile]
    o2 = acc_ref[:, tile:] / l_ref[:, tile:]
    o_ref[...] = (o1 - lam * o2).T


def _mixers(q_bf, k_bf, v_bf, lams, batch, seq, lam_init, page_table, sq, k_new, v_new, cache_k, cache_v,
            rq, rk, rv, srq, srk, srv, sstate):
    tile = ATTN_TILE
    nq = seq // tile
    nseq, npages = page_table.shape
    steps_per_seq = A_HEADS * nq
    assert nseq == batch * steps_per_seq, "one decode sequence per prompt query block"
    assert seq % (steps_per_seq * SUBLANES) == 0, "one retention chunk per grid step"
    chunk = seq // steps_per_seq
    dmask, q_dec, k_dec, g_rows = _ret_tables_chunk(chunk)
    _, _, _, g_token = _ret_tables_chunk(1)
    n_phys, page = cache_k.shape[0], cache_k.shape[1]
    prow = page * A_HEADS
    ck = cache_k.reshape(n_phys, prow, A_ROW)
    cv = cache_v.reshape(n_phys, prow, A_ROW)
    r = np.arange(2 * A_HEADS)[:, None] % A_HEADS
    bias = jnp.asarray(np.where(np.arange(prow)[None, :] % A_HEADS == r, 0.0, NEG).astype(np.float32))
    nbias = jnp.asarray(np.where(np.arange(LANES)[None, :] == r, 0.0, NEG).astype(np.float32))
    step = lambda b, h, i: (b * A_HEADS + h) * nq + i
    lam_spec = pl.BlockSpec((1, A_QK), lambda b, h, i, pt: (0, 0))
    tok = pl.BlockSpec((None, 1, A_WIDTH), lambda b, h, i, pt: (step(b, h, i), 0, 0))
    full = lambda a: pl.BlockSpec(a.shape, lambda b, h, i, pt: (0,) * a.ndim)

    def page_spec(jj):
        return pl.BlockSpec((None, prow, A_ROW), lambda b, h, i, pt: (pt[step(b, h, i) * npages + jj], 0, 0))

    crow = lambda w: pl.BlockSpec((chunk, w), lambda b, h, i, pt: (step(b, h, i), 0))
    stok = lambda w: pl.BlockSpec((None, 1, w), lambda b, h, i, pt: (step(b, h, i), 0, 0))
    sstate_spec = pl.BlockSpec((None, R_QW, R_V), lambda b, h, i, pt: (step(b, h, i), 0, 0))
    kernel = functools.partial(_attn_kernel, tile=tile, lam_init=lam_init, npages=npages)
    grid_spec = pltpu.PrefetchScalarGridSpec(
        num_scalar_prefetch=1,
        grid=(batch, A_HEADS, nq),
        in_specs=[pl.BlockSpec((tile, A_ROW), lambda b, h, i, pt: (b * nq + i, h)),
                  pl.BlockSpec((seq, A_ROW), lambda b, h, i, pt: (b, h)),
                  pl.BlockSpec((seq, A_ROW), lambda b, h, i, pt: (b, h))] + [lam_spec] * 4
                 + [tok, tok, tok, full(bias), full(nbias)]
                 + [crow(R_QW), crow(R_QW), crow(R_WIDTH), full(dmask), full(q_dec), full(k_dec), full(g_rows)]
                 + [stok(R_QW), stok(R_QW), stok(R_WIDTH), sstate_spec, full(g_token)]
                 + [page_spec(jj) for jj in range(npages)] * 2,
        out_specs=(pl.BlockSpec((tile, A_ROW), lambda b, h, i, pt: (b * nq + i, h)), tok,
                   crow(R_WIDTH), pl.BlockSpec((None, R_QW, R_V), lambda b, h, i, pt: (b, 0, 0)),
                   stok(R_WIDTH), sstate_spec),
        scratch_shapes=[pltpu.VMEM((1, 2 * tile), F32), pltpu.VMEM((1, 2 * tile), F32),
                        pltpu.VMEM((A_ROW, 2 * tile), F32), pltpu.VMEM((2 * tile, A_ROW), BF16),
                        pltpu.VMEM((tile, 2 * tile), F32), pltpu.VMEM((tile, 2 * tile), F32),
                        pltpu.VMEM((R_QW, R_V), F32)],
    )
    a_o, a_o_s, r_o, r_state, r_o_s, r_state_s = pl.pallas_call(
        kernel,
        grid_spec=grid_spec,
        out_shape=(jax.ShapeDtypeStruct((batch * seq, A_WIDTH), F32),
                   jax.ShapeDtypeStruct((nseq, 1, A_WIDTH), F32),
                   jax.ShapeDtypeStruct((batch * seq, R_WIDTH), F32),
                   jax.ShapeDtypeStruct((batch, R_QW, R_V), F32),
                   jax.ShapeDtypeStruct((nseq, 1, R_WIDTH), F32),
                   jax.ShapeDtypeStruct((nseq, R_QW, R_V), F32)),
        compiler_params=_cparams(("arbitrary", "arbitrary", "arbitrary")),
        name="mixers",
    )(page_table.reshape(-1), q_bf, k_bf, v_bf, *lams,
      sq.reshape(nseq, 1, A_WIDTH), k_new.reshape(nseq, 1, A_WIDTH), v_new.reshape(nseq, 1, A_WIDTH),
      bias, nbias, rq, rk, rv, dmask, q_dec, k_dec, g_rows,
      srq.reshape(nseq, 1, R_QW), srk.reshape(nseq, 1, R_QW), srv.reshape(nseq, 1, R_WIDTH), sstate, g_token,
      *([ck] * npages), *([cv] * npages))
    return (a_o, a_o_s.reshape(nseq, A_WIDTH), r_o, r_state,
            r_o_s.reshape(nseq, R_WIDTH), r_state_s)


def _ret_decay():
    return [math.log(1.0 - 2.0 ** (-5.0 - h)) for h in range(R_HEADS)]


def _ret_tables_chunk(chunk):
    log_g = jnp.log(1.0 - 2.0 ** (-5.0 - jnp.arange(R_HEADS, dtype=F32)))
    idx = jnp.arange(chunk, dtype=F32)
    diff = idx[:, None] - idx[None, :]
    dmask = jnp.where(diff >= 0, jnp.exp(jnp.maximum(diff, 0.0)[None] * log_g[:, None, None]), 0.0)
    q_dec = jnp.exp((idx + 1.0)[:, None] * log_g[None, :])
    k_dec = jnp.exp((chunk - 1.0 - idx)[:, None] * log_g[None, :])
    q_dec = jnp.repeat(q_dec, R_QK, axis=1)
    k_dec = jnp.repeat(k_dec, R_QK, axis=1)
    g_chunk = jnp.exp(chunk * log_g)
    g_rows = jnp.broadcast_to(jnp.repeat(g_chunk, R_QK)[:, None], (R_QW, R_V))
    return dmask, q_dec, k_dec, g_rows


def _ret_chunk(q_ref, k_ref, v_ref, dmask_ref, qdec_ref, kdec_ref, grow_ref, o_ref, st_ref, state_ref, first):
    q = q_ref[...]
    k = k_ref[...]
    qd = (q * qdec_ref[...]).astype(BF16)
    kd = (k * kdec_ref[...]).astype(BF16)
    qb = q.astype(BF16)
    kb = k.astype(BF16)
    vb = v_ref[...].astype(BF16)
    carried = jnp.where(first, 0.0, state_ref[...])
    for h in range(R_HEADS):
        ks = slice(h * R_QK, (h + 1) * R_QK)
        vs = slice(h * R_V, (h + 1) * R_V)
        state = carried[ks, :]
        inner = lax.dot_general(qb[:, ks], kb[:, ks], (((1,), (1,)), ((), ())),
                                preferred_element_type=F32) * dmask_ref[h]
        o = (jnp.dot(inner.astype(BF16), vb[:, vs], preferred_element_type=F32)
             + jnp.dot(qd[:, ks], state.astype(BF16), preferred_element_type=F32))
        o_ref[:, vs] = o
        upd = lax.dot_general(kd[:, ks], vb[:, vs], (((0,), (0,)), ((), ())),
                              preferred_element_type=F32)
        state_ref[ks, :] = grow_ref[ks, :] * state + upd
    st_ref[...] = state_ref[...]


def _ret_token(q_ref, k_ref, v_ref, state_ref, grow_ref, o_ref, ns_ref):
    kcol = jnp.broadcast_to(k_ref[...], (LANES, R_QW)).T
    qcol = jnp.broadcast_to(q_ref[...], (LANES, R_QW)).T
    vrows = jnp.concatenate(
        [jnp.broadcast_to(v_ref[:, h * R_V:(h + 1) * R_V], (R_QK, R_V)) for h in range(R_HEADS)], axis=0)
    new = grow_ref[...] * state_ref[...] + kcol * vrows
    ns_ref[...] = new
    qn = qcol * new
    for h in range(R_HEADS):
        o_ref[:, h * R_V:(h + 1) * R_V] = jnp.sum(qn[h * R_QK:(h + 1) * R_QK, :], axis=0, keepdims=True)


def _paged_step(q_ref, kn_ref, vn_ref, bias_ref, nbias_ref, lam, k_refs, v_refs, o_ref):
    npages = len(k_refs)
    q = q_ref[...]
    nrow = 2 * A_HEADS

    def head_rows(x, n):
        row = lax.broadcasted_iota(I32, (n, A_ROW), 0)
        out = jnp.zeros((n, A_ROW), F32)
        for h in range(A_HEADS):
            out = jnp.where(row == h, jnp.broadcast_to(x[:, h * A_ROW:(h + 1) * A_ROW], (n, A_ROW)), out)
        return out

    row8 = lax.broadcasted_iota(I32, (nrow, A_ROW), 0)
    lane8 = lax.broadcasted_iota(I32, (nrow, A_ROW), 1)
    q4 = head_rows(q, nrow)
    q8 = q4 + pltpu.roll(q4, A_HEADS, 0)
    qm = jnp.where((row8 < A_HEADS) == (lane8 < A_QK), q8, 0.0).astype(BF16)

    nt = (((1,), (1,)), ((), ()))
    bias = bias_ref[...]
    s = [lax.dot_general(qm, k_refs[j][...].astype(BF16), nt, preferred_element_type=F32) + bias
         for j in range(npages)]
    kn = head_rows(kn_ref[...], LANES).astype(BF16)
    s.append(lax.dot_general(qm, kn, nt, preferred_element_type=F32) + nbias_ref[...])
    m = functools.reduce(jnp.maximum, [jnp.max(x, axis=-1, keepdims=True) for x in s])
    p = [jnp.exp2(x - m) for x in s]
    l = functools.reduce(lambda a, b: a + b, [jnp.sum(x, axis=-1, keepdims=True) for x in p])
    inv = 1.0 / l
    vs = [v_refs[j][...].astype(BF16) for j in range(npages)] + [head_rows(vn_ref[...], LANES).astype(BF16)]
    out = jnp.zeros((nrow, A_ROW), F32)
    for pj, vj in zip(p, vs):
        pn = pj * inv
        first = lax.broadcasted_iota(I32, pn.shape, 0) < A_HEADS
        w8 = jnp.where(first, pn - lam * pltpu.roll(pn, A_HEADS, 0), 0.0).astype(BF16)
        out = out + jnp.dot(w8, vj, preferred_element_type=F32)
    for h in range(A_HEADS):
        o_ref[:, h * A_ROW:(h + 1) * A_ROW] = out[h:h + 1, :]


def _mix_tile(x_ref, a_ref, r_ref, g_ref, asub_ref, rnorm_ref, wo_ref, n2_ref,
              wrc_ref, wrh_ref, br_ref, h_ref, xn_ref, route_ref, *, lam_init):
    tm = x_ref.shape[0]
    parts = []
    for hd in range(A_HEADS):
        a = a_ref[:, hd * A_ROW:(hd + 1) * A_ROW]
        ms = jnp.mean(a * a, axis=-1, keepdims=True)
        parts.append((a * lax.rsqrt(ms + EPS) * asub_ref[...] * (1.0 - lam_init)).astype(BF16))
    for hd in range(R_HEADS):
        sl = slice(hd * R_V, (hd + 1) * R_V)
        r = r_ref[:, sl]
        ms = jnp.mean(r * r, axis=-1, keepdims=True)
        gate = g_ref[:, sl]
        gate = gate * (1.0 / (1.0 + jnp.exp(-gate)))
        parts.append((r * lax.rsqrt(ms + EPS) * rnorm_ref[...] * gate).astype(BF16))
    merged = jnp.concatenate(parts, axis=1)
    h = x_ref[...] + jnp.dot(merged, wo_ref[...], preferred_element_type=F32)
    h_ref[...] = h
    ms = jnp.mean(h * h, axis=-1, keepdims=True)
    xn = h * lax.rsqrt(ms + EPS) * n2_ref[...]
    for c in range(CHUNKS):
        xn_ref[pl.ds(c, tm, stride=CHUNKS), :] = xn[:, c * LANES:(c + 1) * LANES]
    xh = xn.astype(BF16)
    xl = (xn - xh.astype(F32)).astype(BF16)
    both = jnp.dot(xh, wrc_ref[...], preferred_element_type=F32)
    logits = (both[:, :LANES] + jnp.dot(xl, wrh_ref[...], preferred_element_type=F32)
              + both[:, LANES:]) + br_ref[...]
    lane = lax.broadcasted_iota(I32, logits.shape, 1).astype(F32)
    big = float(LANES)
    gl = jnp.where(lane < N_GROUPS, logits, NEG)
    gmax = jnp.max(gl, axis=-1, keepdims=True)
    gidx = jnp.min(jnp.where(gl == gmax, lane, big), axis=-1, keepdims=True)
    gsum = jnp.sum(jnp.where(lane < N_GROUPS, jnp.exp(gl - gmax), 0.0), axis=-1, keepdims=True)
    gprob = 1.0 / gsum
    lo = N_GROUPS + EXPERTS_PER_GROUP * gidx
    el = jnp.where((lane >= lo) & (lane < lo + EXPERTS_PER_GROUP), logits, NEG)
    v1 = jnp.max(el, axis=-1, keepdims=True)
    i1 = jnp.min(jnp.where(el == v1, lane, big), axis=-1, keepdims=True)
    el2 = jnp.where(lane == i1, NEG, el)
    v2 = jnp.max(el2, axis=-1, keepdims=True)
    i2 = jnp.min(jnp.where(el2 == v2, lane, big), axis=-1, keepdims=True)
    e = jnp.exp(v2 - v1)
    w1 = gprob / (1.0 + e)
    w2 = gprob * e / (1.0 + e)
    e1 = i1 - N_GROUPS
    e2 = i2 - N_GROUPS
    route_ref[...] = jnp.where(lane == 0, e1, jnp.where(lane == 1, e2, jnp.where(
        lane == 2, w1, jnp.where(lane == 3, w2, 0.0))))


def _mix_kernel(*refs, lam_init, n_tiles, has_tail):
    if not has_tail:
        _mix_tile(*refs, lam_init=lam_init)
        return
    ins, (th_ref, txn_ref, troute_ref), outs = refs[:11], refs[11:14], refs[14:]
    h_ref, xn_ref, route_ref = outs
    i = pl.program_id(0)

    @pl.when(i < n_tiles)
    def _():
        _mix_tile(*ins, *outs, lam_init=lam_init)

    @pl.when(i == n_tiles)
    def _():
        tm = h_ref.shape[0]
        nt = th_ref.shape[0]
        h_ref[:nt, :] = th_ref[...]
        h_ref[nt:, :] = jnp.zeros((tm - nt, D_MODEL), F32)
        xn_ref[:nt * CHUNKS, :] = txn_ref[...]
        xn_ref[nt * CHUNKS:, :] = jnp.zeros(((tm - nt) * CHUNKS, LANES), F32)
        route_ref[:nt, :] = troute_ref[...]
        route_ref[nt:, :] = jnp.zeros((tm - nt, LANES), F32)


def _mix_out(x2d, a_o, r_o, rg, asub, rnorm, wo_bf, n2, wr_cat, wr_hi, br, lam_init, tm, tail=None):
    t = x2d.shape[0]
    n_tiles = t // tm
    has_tail = tail is not None
    n_out = n_tiles + (1 if has_tail else 0)
    row = lambda w: pl.BlockSpec((tm, w), lambda i: (jnp.minimum(i, n_tiles - 1), 0))
    full = lambda a: pl.BlockSpec(a.shape, lambda i: (0,) * a.ndim)
    out_shape = (jax.ShapeDtypeStruct((n_out * tm, D_MODEL), F32),
                 jax.ShapeDtypeStruct((n_out * tm * CHUNKS, LANES), F32),
                 jax.ShapeDtypeStruct((n_out * tm, LANES), F32))
    out_specs = (pl.BlockSpec((tm, D_MODEL), lambda i: (i, 0)),
                 pl.BlockSpec((tm * CHUNKS, LANES), lambda i: (i, 0)),
                 pl.BlockSpec((tm, LANES), lambda i: (i, 0)))
    in_specs = [row(D_MODEL), row(A_WIDTH), row(R_WIDTH), row(R_WIDTH), full(asub), full(rnorm),
                full(wo_bf), full(n2), full(wr_cat), full(wr_hi), full(br)]
    args = [x2d, a_o, r_o, rg, asub, rnorm, wo_bf, n2, wr_cat, wr_hi, br]
    if has_tail:
        assert tail[0].shape[0] <= tm
        in_specs += [full(a) for a in tail]
        args += list(tail)
    kernel = functools.partial(_mix_kernel, lam_init=lam_init, n_tiles=n_tiles, has_tail=has_tail)
    return pl.pallas_call(
        kernel,
        grid=(n_out,),
        in_specs=in_specs,
        out_specs=out_specs,
        out_shape=out_shape,
        compiler_params=_cparams(("arbitrary",)),
        name="mix_out",
    )(*args)


def _plan_kernel(route_ref, ltri_ref, utri_ref, dest_ref, te_ref, meta_ref, *, n, slot_tile):
    tile = PLAN_TILE
    lane = lax.broadcasted_iota(I32, (tile, LANES), 1)

    def block(b):
        r = route_ref[pl.ds(pl.multiple_of(b * tile, tile), tile), :]
        e1 = r[:, 0:1].astype(I32)
        e2 = r[:, 1:2].astype(I32)
        return jnp.where((lane == e1) | (lane == e2), 1.0, 0.0), e1, e2

    def count(b, c):
        return c + jnp.sum(block(b)[0], axis=0, keepdims=True)

    cnt = lax.fori_loop(0, n // tile, count, jnp.zeros((1, LANES), F32), unroll=PLAN_UNROLL)
    ntile = jnp.floor((cnt + (slot_tile - 1)) * (1.0 / slot_tile))
    nt8 = jnp.broadcast_to(ntile, (SUBLANES, LANES)).astype(BF16)
    base_t = jnp.dot(nt8, utri_ref[...], preferred_element_type=F32)[0:1, :]
    base = base_t * slot_tile
    ends = base_t + ntile
    tl = lax.broadcasted_iota(I32, te_ref.shape, 0).astype(F32)
    el = lax.broadcasted_iota(I32, te_ref.shape, 1)
    hit = jnp.where((el < N_EXPERTS) & (ends <= tl), 1.0, 0.0)
    te_ref[...] = jnp.broadcast_to(jnp.sum(hit, axis=-1, keepdims=True), te_ref.shape).astype(I32)
    mrow = lax.broadcasted_iota(I32, meta_ref.shape, 0)
    meta_ref[...] = jnp.where(mrow == 0, cnt, jnp.where(mrow == 1, base, jnp.where(
        mrow == 2, ntile * slot_tile, 0.0))).astype(I32)

    def place(b, run):
        onehot, e1, e2 = block(b)
        rank = jnp.dot(ltri_ref[...], onehot.astype(BF16), preferred_element_type=F32)
        pos = base + run + rank
        d1 = jnp.sum(jnp.where(lane == e1, pos, 0.0), axis=-1, keepdims=True)
        d2 = jnp.sum(jnp.where(lane == e2, pos, 0.0), axis=-1, keepdims=True)
        dest_ref[pl.ds(pl.multiple_of(b * tile, tile), tile), :] = jnp.where(
            lane == 0, d1, jnp.where(lane == 1, d2, 0.0)).astype(I32)
        return run + jnp.sum(onehot, axis=0, keepdims=True)

    lax.fori_loop(0, n // tile, place, jnp.zeros((1, LANES), F32), unroll=PLAN_UNROLL)


def _plan(route, n, n_slot_tiles):
    tile = PLAN_TILE
    te_rows = -(-n_slot_tiles // SUBLANES) * SUBLANES
    ii = np.arange(tile)
    ltri = jnp.asarray((ii[None, :] < ii[:, None]).astype(np.float32), dtype=BF16)
    ee = np.arange(LANES)
    utri = jnp.asarray((ee[:, None] < ee[None, :]).astype(np.float32), dtype=BF16)
    kernel = functools.partial(_plan_kernel, n=n, slot_tile=SLOT_TILE)
    return pl.pallas_call(
        kernel,
        out_shape=(jax.ShapeDtypeStruct((n, LANES), I32),
                   jax.ShapeDtypeStruct((te_rows, LANES), I32),
                   jax.ShapeDtypeStruct((SUBLANES, LANES), I32)),
        compiler_params=pltpu.CompilerParams(vmem_limit_bytes=VMEM_LIMIT),
        name="plan",
    )(route, ltri, utri)


def _total_tiles(meta_ref):
    last = N_EXPERTS - 1
    return (meta_ref[LANES + last] + meta_ref[2 * LANES + last]) // SLOT_TILE


def _dispatch_kernel(dest_ref, meta_ref, xn_ref, xs_hbm, zeros_ref, zsem, ssem, *, tm, n_steps, n_total, n_tiles):
    i = pl.program_id(0)
    ts = SLOT_TILE

    def zero_tiles(start):
        def one(tile):
            cp = pltpu.make_async_copy(zeros_ref, xs_hbm.at[pl.ds(tile * ts, ts)], zsem)
            if start:
                cp.start()
            else:
                cp.wait()

        def per_expert(e, c):
            first = meta_ref[LANES + e]
            reserved = meta_ref[2 * LANES + e]

            @pl.when(meta_ref[e] < reserved)
            def _():
                one((first + reserved) // ts - 1)

            return c

        lax.fori_loop(0, N_EXPERTS, per_expert, 0)

        def trailing(t, c):
            one(t)
            return c

        lax.fori_loop(_total_tiles(meta_ref), n_tiles, trailing, 0)

    @pl.when(i == 0)
    def _():
        zeros_ref[...] = jnp.zeros(zeros_ref.shape, F32)
        zero_tiles(True)
        zero_tiles(False)

    def scatter(n_tok):
        base = 2 * i * tm

        def start(r, c):
            src = xn_ref.at[pl.ds(pl.multiple_of(r * CHUNKS, CHUNKS), CHUNKS), :]
            for k in range(2):
                pltpu.make_async_copy(src, xs_hbm.at[dest_ref[base + 2 * r + k]], ssem).start(
                    priority=k % DMA_PRIORITIES)
            return c

        lax.fori_loop(0, n_tok, start, 0, unroll=8)

        def wait(r, c):
            pltpu.make_async_copy(xn_ref.at[pl.ds(0, CHUNKS), :], xs_hbm.at[0], ssem).wait()
            return c

        lax.fori_loop(0, 2 * n_tok, wait, 0, unroll=8)

    rem = n_total - (n_steps - 1) * tm
    if rem == tm:
        scatter(tm)
    else:
        pl.when(i < n_steps - 1)(functools.partial(scatter, tm))
        pl.when(i == n_steps - 1)(functools.partial(scatter, rem))


def _dispatch(dest_flat, meta_flat, xn_pool, n_total, n_tiles):
    tm = TOKEN_TILE
    n_steps = -(-n_total // tm)
    assert xn_pool.shape[0] >= n_steps * tm * CHUNKS
    kernel = functools.partial(_dispatch_kernel, tm=tm, n_steps=n_steps, n_total=n_total, n_tiles=n_tiles)
    grid_spec = pltpu.PrefetchScalarGridSpec(
        num_scalar_prefetch=2,
        grid=(n_steps,),
        in_specs=[pl.BlockSpec((tm * CHUNKS, LANES), lambda i, d, m: (i, 0))],
        out_specs=pl.BlockSpec(memory_space=pl.ANY),
        scratch_shapes=[pltpu.VMEM((SLOT_TILE, CHUNKS, LANES), F32),
                        pltpu.SemaphoreType.DMA, pltpu.SemaphoreType.DMA],
    )
    return pl.pallas_call(
        kernel,
        grid_spec=grid_spec,
        out_shape=jax.ShapeDtypeStruct((n_tiles * SLOT_TILE, CHUNKS, LANES), F32),
        compiler_params=_cparams(("arbitrary",)),
        name="dispatch",
    )(dest_flat, meta_flat, xn_pool)


def _row_gather(src_hbm, idx_of_row, buf, sem, n_rows):
    def body(j, c):
        for k in range(DMA_PRIORITIES):
            r = DMA_PRIORITIES * j + k
            pltpu.make_async_copy(src_hbm.at[idx_of_row(r)],
                                  buf.at[pl.ds(pl.multiple_of(r * CHUNKS, CHUNKS), CHUNKS), :],
                                  sem).start(priority=k)
        return c

    assert n_rows % DMA_PRIORITIES == 0
    lax.fori_loop(0, n_rows // DMA_PRIORITIES, body, 0, unroll=8)


def _row_wait(src_hbm, buf, sem, n_rows):
    def body(r, c):
        pltpu.make_async_copy(src_hbm.at[0], buf.at[pl.ds(0, CHUNKS), :], sem).wait()
        return c

    lax.fori_loop(0, n_rows, body, 0, unroll=8)


def _gathered_rows(buf, n_rows):
    return jnp.concatenate([buf[pl.ds(c, n_rows, stride=CHUNKS), :] for c in range(CHUNKS)], axis=1)


def _expert_kernel(te_ref, meta_ref, xs_ref, wg_ref, wu_ref, wd_ref, ys_ref, wg_bf, wu_bf, wd_bf):
    i = pl.program_id(0)
    ts = SLOT_TILE
    e = te_ref[i]
    active = e < N_EXPERTS
    prev = te_ref[jnp.maximum(i - 1, 0)]

    @pl.when(active & ((i == 0) | (e != prev)))
    def _():
        wg_bf[...] = wg_ref[...].astype(BF16)
        wu_bf[...] = wu_ref[...].astype(BF16)
        wd_bf[...] = wd_ref[...].astype(BF16)

    @pl.when(active)
    def _():
        x = _gathered_rows(xs_ref, ts).astype(BF16)
        a = jnp.dot(x, wg_bf[...], preferred_element_type=F32)
        u = jnp.dot(x, wu_bf[...], preferred_element_type=F32)
        hmid = (a * (1.0 / (1.0 + jnp.exp(-a))) * u).astype(BF16)
        y = jnp.dot(hmid, wd_bf[...], preferred_element_type=F32)
        for c in range(CHUNKS):
            ys_ref[pl.ds(c, ts, stride=CHUNKS), :] = y[:, c * LANES:(c + 1) * LANES]

    @pl.when(jnp.logical_not(active))
    def _():
        ys_ref[...] = jnp.zeros(ys_ref.shape, F32)


def _experts(te, meta_flat, xs, w_gate, w_up, w_down, n_tiles):
    ts = SLOT_TILE
    wsel = lambda i, te, meta: (jnp.minimum(te[i], N_EXPERTS - 1), 0, 0)
    xsel = lambda i, te, meta: (jnp.minimum(i, _total_tiles(meta) - 1), 0)
    grid_spec = pltpu.PrefetchScalarGridSpec(
        num_scalar_prefetch=2,
        grid=(n_tiles,),
        in_specs=[pl.BlockSpec((ts * CHUNKS, LANES), xsel),
                  pl.BlockSpec((None, D_MODEL, D_EXPERT), wsel),
                  pl.BlockSpec((None, D_MODEL, D_EXPERT), wsel),
                  pl.BlockSpec((None, D_EXPERT, D_MODEL), wsel)],
        out_specs=pl.BlockSpec((ts * CHUNKS, LANES), lambda i, te, meta: (i, 0)),
        scratch_shapes=[pltpu.VMEM((D_MODEL, D_EXPERT), BF16),
                        pltpu.VMEM((D_MODEL, D_EXPERT), BF16),
                        pltpu.VMEM((D_EXPERT, D_MODEL), BF16)],
    )
    return pl.pallas_call(
        _expert_kernel,
        grid_spec=grid_spec,
        out_shape=jax.ShapeDtypeStruct((n_tiles * ts * CHUNKS, LANES), F32),
        compiler_params=_cparams(("arbitrary",)),
        name="experts",
    )(te, meta_flat, xs.reshape(n_tiles * ts * CHUNKS, LANES), w_gate, w_up, w_down)


def _combine_kernel(dest_ref, ys_hbm, h_ref, route_ref, y_ref, gbuf, sem, *, tm, n_steps, tok_off):
    i = pl.program_id(0)
    slot = i % 2
    nrow = 2 * tm

    def start(step, sl):
        base = (tok_off + step * tm) * 2
        _row_gather(ys_hbm, lambda r: dest_ref[base + r], gbuf.at[sl], sem.at[sl], nrow)

    @pl.when(i == 0)
    def _():
        start(0, 0)

    @pl.when(i + 1 < n_steps)
    def _():
        start(i + 1, 1 - slot)

    _row_wait(ys_hbm, gbuf.at[slot], sem.at[slot], nrow)
    buf = gbuf.at[slot]
    route = route_ref[...]
    w1 = route[:, 2:3]
    w2 = route[:, 3:4]
    for c in range(CHUNKS):
        g1 = buf[pl.ds(c, tm, stride=2 * CHUNKS), :]
        g2 = buf[pl.ds(CHUNKS + c, tm, stride=2 * CHUNKS), :]
        sl = slice(c * LANES, (c + 1) * LANES)
        y_ref[:, sl] = h_ref[:, sl] + w1 * g1 + w2 * g2


def _combine(dest_flat, ys3d, h_pool, route, tm, tok_off, n_tok):
    n_steps = n_tok // tm
    boff = tok_off // tm
    kernel = functools.partial(_combine_kernel, tm=tm, n_steps=n_steps, tok_off=tok_off)
    grid_spec = pltpu.PrefetchScalarGridSpec(
        num_scalar_prefetch=1,
        grid=(n_steps,),
        in_specs=[pl.BlockSpec(memory_space=pl.ANY),
                  pl.BlockSpec((tm, D_MODEL), lambda i, d: (i + boff, 0)),
                  pl.BlockSpec((tm, LANES), lambda i, d: (i + boff, 0))],
        out_specs=pl.BlockSpec((tm, D_MODEL), lambda i, d: (i, 0)),
        scratch_shapes=[pltpu.VMEM((2, 2 * tm * CHUNKS, LANES), F32), pltpu.SemaphoreType.DMA((2,))],
    )
    return pl.pallas_call(
        kernel,
        grid_spec=grid_spec,
        out_shape=jax.ShapeDtypeStruct((n_tok, D_MODEL), F32),
        compiler_params=_cparams(("arbitrary",)),
        name="combine",
    )(dest_flat, ys3d, h_pool, route)


def kernel(x_prompt, x_sample, cache_k, cache_v, state_ret, page_table, norm1, w_in, a_q_norm, a_k_norm,
           a_lambda_q1, a_lambda_k1, a_lambda_q2, a_lambda_k2, a_subln, r_norm, w_o, norm2,
           w_group_router, b_group_router, w_expert_router, b_expert_router, w_gate, w_up, w_down):
    depth = norm1.shape[0]
    assert depth == 1, "single-layer step"
    batch, seq, d = x_prompt.shape
    nsamp, tdec, _ = x_sample.shape
    assert d == D_MODEL and tdec == 1
    assert seq % TOKEN_TILE == 0 and seq % ATTN_TILE == 0
    assert nsamp % PLAN_TILE == 0
    past = page_table.shape[1] * cache_k.shape[2]
    n_prompt = batch * seq
    assert n_prompt % COMBINE_TILE == 0
    n_total = n_prompt + nsamp
    lam_init = 0.8 - 0.6 * math.exp(-0.3 * 0)
    l = 0

    lams = tuple(a[l].reshape(1, A_QK) for a in (a_lambda_q1, a_lambda_k1, a_lambda_q2, a_lambda_k2))
    w_in_bf = w_in[l].astype(BF16)
    w_o_bf = w_o[l].astype(BF16)
    g1 = norm1[l].reshape(1, D_MODEL)
    n2 = norm2[l].reshape(1, D_MODEL)
    qg = jnp.tile(a_q_norm[l], LANES // A_QK).reshape(1, LANES)
    kg = jnp.tile(a_k_norm[l], LANES // A_QK).reshape(1, LANES)
    asub = a_subln[l].reshape(1, A_ROW)
    rnorm = r_norm[l].reshape(1, R_V)
    ones = _segment_ones()
    w_r = jnp.concatenate([w_group_router[l], w_expert_router[l]], axis=1)
    w_r = jnp.pad(w_r, ((0, 0), (0, LANES - w_r.shape[1])))
    wr_hi = w_r.astype(BF16)
    wr_lo = (w_r - wr_hi.astype(F32)).astype(BF16)
    wr_cat = jnp.concatenate([wr_hi, wr_lo], axis=1)
    b_r = jnp.concatenate([b_group_router[l], b_expert_router[l]])
    b_r = jnp.pad(b_r, (0, LANES - b_r.shape[0])).reshape(1, LANES)

    pos_s = jnp.full((nsamp,), past, dtype=jnp.int32)
    xs = x_sample.reshape(nsamp, D_MODEL)
    (sq_bf, k_s, _, v_s, _, srq, srk, srv, srg) = _proj(
        xs, g1, w_in_bf, qg, kg, _rope_tables(pos_s), _ret_tables(pos_s), ones, PLAN_TILE)
    pos_p = jnp.arange(seq)
    xp = x_prompt.reshape(n_prompt, D_MODEL)
    (q_bf, k_p, k_bf, v_p, v_bf, rq, rk, rv, rg) = _proj(
        xp, g1, w_in_bf, qg, kg, _rope_tables(pos_p), _ret_tables(pos_p), ones, TOKEN_TILE)
    a_o, a_o_s, r_o, r_state_p, r_o_s, r_state_s = _mixers(
        q_bf, k_bf, v_bf, lams, batch, seq, lam_init, page_table, sq_bf.astype(F32), k_s, v_s,
        cache_k[l], cache_v[l], rq, rk, rv, srq, srk, srv, state_ret[l].reshape(nsamp, R_QW, R_V))

    sample_rows = _mix_out(xs, a_o_s, r_o_s, srg, asub, rnorm, w_o_bf, n2, wr_cat, wr_hi, b_r,
                           lam_init, PLAN_TILE)
    h_pool, xn_pool, route = _mix_out(xp, a_o, r_o, rg, asub, rnorm, w_o_bf, n2, wr_cat, wr_hi, b_r,
                                      lam_init, MIX_TILE, tail=sample_rows)
    n_pool = h_pool.shape[0]

    n_assign = 2 * n_total
    n_tiles = -(-(n_assign + N_EXPERTS * (SLOT_TILE - 1)) // SLOT_TILE)
    n_slots = n_tiles * SLOT_TILE
    dest128, te128, meta = _plan(route, n_total, n_tiles)
    dest_flat = dest128[:, :2].reshape(-1)
    te = te128[:n_tiles, 0]
    meta_flat = meta.reshape(-1)
    xs = _dispatch(dest_flat, meta_flat, xn_pool, n_total, n_tiles)
    ys = _experts(te, meta_flat, xs,
                  w_gate[l].reshape(N_EXPERTS, D_MODEL, D_EXPERT),
                  w_up[l].reshape(N_EXPERTS, D_MODEL, D_EXPERT),
                  w_down[l].reshape(N_EXPERTS, D_EXPERT, D_MODEL), n_tiles)
    ys3d = ys.reshape(n_slots, CHUNKS, LANES)
    y_p = _combine(dest_flat, ys3d, h_pool, route, COMBINE_TILE, 0, n_prompt)
    y_s = _combine(dest_flat, ys3d, h_pool, route, PLAN_TILE, n_prompt, nsamp)

    return (y_p.reshape(batch, seq, D_MODEL),
            y_s.reshape(nsamp, 1, D_MODEL),
            k_p.reshape(1, batch, seq, A_HEADS, A_ROW),
            v_p.reshape(1, batch, seq, A_HEADS, A_ROW),
            r_state_p.reshape(1, batch, R_HEADS, R_QK, R_V),
            k_s.reshape(1, nsamp, 1, A_HEADS, A_ROW),
            v_s.reshape(1, nsamp, 1, A_HEADS, A_ROW),
            r_state_s.reshape(1, nsamp, R_HEADS, R_QK, R_V))
```
